```python
import jax, jax.numpy as jnp
from jax import lax
import numpy as np

D_MODEL = 2048
BATCH = 2
SEQ = 16384
DEPTH = 2

GRID_W = 64
CTX_LEN = 256
HEAD_DIM = 128
N_MIX_HEADS = D_MODEL // HEAD_DIM
GM_GROUPS = N_MIX_HEADS // 4
ATT_HEADS = (3 * N_MIX_HEADS) // 8
ML_HEADS = N_MIX_HEADS - GM_GROUPS - ATT_HEADS
ATT_KV_HEADS = ATT_HEADS // 3
ATT_GROUP = ATT_HEADS // ATT_KV_HEADS
GM_WIDTH = GM_GROUPS * HEAD_DIM
ATT_WIDTH = ATT_HEADS * HEAD_DIM
ATT_KV_WIDTH = ATT_KV_HEADS * HEAD_DIM
ML_WIDTH = ML_HEADS * HEAD_DIM
MIX_WIDTH = GM_WIDTH + ATT_WIDTH + ML_WIDTH
GM_CHUNK = 128
ATT_BLOCK = 128
WINDOW = 128
ROPE_BASE = 10000.0
ML_CHUNK = 128
ML_CONV = 3
FFN_HIDDEN = 4 * D_MODEL
IN_SPLITS = (GM_WIDTH, GM_WIDTH, ATT_WIDTH, ATT_KV_WIDTH, ATT_KV_WIDTH,
             ML_WIDTH, ML_WIDTH, ML_WIDTH, ML_WIDTH, 4 * ML_HEADS)
N_IN = sum(IN_SPLITS)
EPS = 1e-6

kernel_name = 'hybrid_parallel_gmlp_swa_mlstm_dit'


def rms_norm(x, gain):
    x32 = x.astype(jnp.float32)
    y = x32 * lax.rsqrt(jnp.mean(x32 * x32, axis=-1, keepdims=True) + EPS)
    return (y * gain.astype(jnp.float32)).astype(x.dtype)


def modulate(h, shift, scale):
    return h * (1 + scale) + shift


def axial_angles(rows):
    row = jnp.broadcast_to(jnp.arange(rows)[:, None], (rows, GRID_W)).reshape(-1).astype(jnp.float32)
    col = jnp.broadcast_to(jnp.arange(GRID_W)[None, :], (rows, GRID_W)).reshape(-1).astype(jnp.float32)
    half = HEAD_DIM // 2
    freqs = 1.0 / (ROPE_BASE ** (jnp.arange(0, half, 2, dtype=jnp.float32) / half))
    return row[:, None] * freqs, col[:, None] * freqs


def rotate(x, ang):
    m = ang.shape[-1]
    cos = jnp.cos(ang)[:, None, :].astype(x.dtype)
    sin = jnp.sin(ang)[:, None, :].astype(x.dtype)
    x1, x2 = x[..., :m], x[..., m:]
    return jnp.concatenate([x1 * cos - x2 * sin, x1 * sin + x2 * cos], axis=-1)


def axial_rope(x, ang_row, ang_col):
    half = HEAD_DIM // 2
    return jnp.concatenate([rotate(x[..., :half], ang_row), rotate(x[..., half:], ang_col)], axis=-1)


def short_conv(x, w, b):
    pad = ML_CONV // 2
    L = x.shape[1]
    xp = jnp.pad(x, ((0, 0), (pad, pad), (0, 0)))
    y = b + xp[:, 0:L] * w[0]
    for i in range(1, ML_CONV):
        y = y + xp[:, i:i + L] * w[i]
    return jax.nn.silu(y)


def chunk_gmlp(u, v, v_gain, w_s, b_s):
    B, L, _ = u.shape
    n = L // GM_CHUNK
    vh = rms_norm(v.reshape(B, n, GM_CHUNK, GM_GROUPS, HEAD_DIM), v_gain.reshape(GM_GROUPS, HEAD_DIM))
    s = jnp.einsum('gpq,bnqgc->bnpgc', w_s, vh) + b_s.T[None, None, :, :, None]
    return u * s.reshape(B, L, GM_WIDTH)


def window_attention(q, k, v, kc, vc, sink):
    B, L = q.shape[:2]
    nb = L // ATT_BLOCK
    scale = HEAD_DIM ** -0.5
    qb = q.reshape(B, nb, ATT_BLOCK, ATT_KV_HEADS, ATT_GROUP, HEAD_DIM)

    def band(t):
        tp = jnp.pad(t, ((0, 0), (ATT_BLOCK, ATT_BLOCK), (0, 0), (0, 0)))
        tp = tp.reshape(B, nb + 2, ATT_BLOCK, ATT_KV_HEADS, HEAD_DIM)
        return jnp.concatenate([tp[:, :-2], tp[:, 1:-1], tp[:, 2:]], axis=2)

    kw, vw = band(k), band(v)
    blk = jnp.arange(nb)[:, None, None]
    qpos = blk * ATT_BLOCK + jnp.arange(ATT_BLOCK)[None, :, None]
    kpos = (blk - 1) * ATT_BLOCK + jnp.arange(3 * ATT_BLOCK)[None, None, :]
    valid = (jnp.abs(kpos - qpos) <= WINDOW) & (kpos >= 0) & (kpos < L)
    s_loc = jnp.einsum('bnqhgd,bnkhd->bnhgqk', qb, kw).astype(jnp.float32) * scale
    s_loc = jnp.where(valid[None, :, None, None], s_loc, -jnp.inf)
    s_ctx = jnp.einsum('bnqhgd,bchd->bnhgqc', qb, kc).astype(jnp.float32) * scale
    sk = sink.astype(jnp.float32).reshape(ATT_KV_HEADS, ATT_GROUP)[:, :, None]
    m = jnp.maximum(jnp.maximum(s_loc.max(-1), s_ctx.max(-1)), sk)
    p_loc = jnp.exp(s_loc - m[..., None])
    p_ctx = jnp.exp(s_ctx - m[..., None])
    den = p_loc.sum(-1) + p_ctx.sum(-1) + jnp.exp(sk - m)
    o = (jnp.einsum('bnhgqk,bnkhd->bnqhgd', p_loc.astype(v.dtype), vw)
         + jnp.einsum('bnhgqc,bchd->bnqhgd', p_ctx.astype(vc.dtype), vc))
    o = o / jnp.moveaxis(den, -1, 2)[..., None].astype(o.dtype)
    return o.reshape(B, L, ATT_WIDTH)


def context_attention(qc, kc, vc, sink):
    B, Lc = qc.shape[:2]
    scale = HEAD_DIM ** -0.5
    qg = qc.reshape(B, Lc, ATT_KV_HEADS, ATT_GROUP, HEAD_DIM)
    s = jnp.einsum('bqhgd,bkhd->bhgqk', qg, kc).astype(jnp.float32) * scale
    sk = sink.astype(jnp.float32).reshape(ATT_KV_HEADS, ATT_GROUP)[:, :, None]
    m = jnp.maximum(s.max(-1), sk)
    p = jnp.exp(s - m[..., None])
    den = p.sum(-1) + jnp.exp(sk - m)
    o = jnp.einsum('bhgqk,bkhd->bqhgd', p.astype(vc.dtype), vc)
    o = o / jnp.moveaxis(den, -1, 1)[..., None].astype(o.dtype)
    return o.reshape(B, Lc, ATT_WIDTH)


def mlstm_states(k, v, log_i, log_f, state):
    B, H, L, d = k.shape
    nc = L // ML_CHUNK
    kb = k.reshape(B, H, nc, ML_CHUNK, d)
    vb = v.reshape(B, H, nc, ML_CHUNK, d)
    li = log_i.reshape(B, H, nc, ML_CHUNK)
    b = jnp.cumsum(log_f.reshape(B, H, nc, ML_CHUNK), axis=-1)
    g = b[..., -1]
    a = g[..., None] - b + li
    m_loc = a.max(-1)
    w = jnp.exp(a - m_loc[..., None])
    dC = jnp.einsum('bhntv,bhntk->bhnvk', vb * w[..., None], kb)
    dn = jnp.einsum('bhnt,bhntk->bhnk', w, kb)

    def step(carry, xs):
        C, n, m = carry
        g_j, ml_j, dC_j, dn_j = xs
        m_new = jnp.maximum(g_j + m, ml_j)
        a_old = jnp.exp(g_j + m - m_new)
        a_new = jnp.exp(ml_j - m_new)
        C_new = a_old[..., None, None] * C + a_new[..., None, None] * dC_j
        n_new = a_old[..., None] * n + a_new[..., None] * dn_j
        return (C_new, n_new, m_new), (C, n, m)

    xs = (jnp.moveaxis(g, 2, 0), jnp.moveaxis(m_loc, 2, 0), jnp.moveaxis(dC, 2, 0), jnp.moveaxis(dn, 2, 0))
    final, starts = lax.scan(step, state, xs)
    starts = tuple(jnp.moveaxis(s, 0, 2) for s in starts)
    return b, starts, final


def mlstm_outputs(q, k, v, log_i, b, starts):
    C0, n0, m0 = starts
    B, H, L, d = q.shape
    nc = L // ML_CHUNK
    qb = q.reshape(B, H, nc, ML_CHUNK, d)
    kb = k.reshape(B, H, nc, ML_CHUNK, d)
    vb = v.reshape(B, H, nc, ML_CHUNK, d)
    li = log_i.reshape(B, H, nc, ML_CHUNK)
    order = jnp.tril(jnp.ones((ML_CHUNK, ML_CHUNK), dtype=bool))
    D = jnp.where(order, b[..., :, None] - b[..., None, :] + li[..., None, :], -jnp.inf)
    e = b + m0[..., None]
    m_t = jnp.maximum(e, D.max(-1))
    S = jnp.einsum('bhntd,bhnsd->bhnts', qb, kb) * jnp.exp(D - m_t[..., None])
    inter = jnp.exp(e - m_t)
    num = jnp.einsum('bhnts,bhnsv->bhntv', S, vb) + inter[..., None] * jnp.einsum('bhnvk,bhntk->bhntv', C0, qb)
    den = S.sum(-1) + inter * jnp.einsum('bhnk,bhntk->bhnt', n0, qb)
    h = num / jnp.maximum(jnp.abs(den), jnp.exp(-m_t))[..., None]
    return h.reshape(B, H, L, d)


def mlstm_mixer(q, k, v, o, gates, kc, vc, gates_c, b_i, b_f, out_gain, qc=None, oc=None):
    f32 = jnp.float32
    scale = HEAD_DIM ** -0.5

    def heads(t):
        B, L, _ = t.shape
        return t.reshape(B, L, ML_HEADS, HEAD_DIM).transpose(0, 2, 1, 3).astype(f32)

    def gate_logs(gt, direction):
        B, L, _ = gt.shape
        gt = gt.reshape(B, L, 2, 2, ML_HEADS).astype(f32)
        log_i = gt[:, :, direction, 0] + b_i[direction].astype(f32)
        log_f = jax.nn.log_sigmoid(gt[:, :, direction, 1] + b_f[direction].astype(f32))
        return log_i.transpose(0, 2, 1), log_f.transpose(0, 2, 1)

    qh, kh, vh = heads(q), heads(k) * scale, heads(v)
    kch, vch = heads(kc) * scale, heads(vc)
    B = q.shape[0]
    zero = (jnp.zeros((B, ML_HEADS, HEAD_DIM, HEAD_DIM), f32),
            jnp.zeros((B, ML_HEADS, HEAD_DIM), f32),
            jnp.zeros((B, ML_HEADS), f32))
    h_dirs, hc_dirs = [], []
    for direction in range(2):
        if direction == 0:
            rev = lambda t: t
        else:
            rev = lambda t: jnp.flip(t, axis=2)
        lic, lfc = (rev(t) for t in gate_logs(gates_c, direction))
        b_c, starts_c, final_c = mlstm_states(rev(kch), rev(vch), lic, lfc, zero)
        if qc is not None:
            hc_dirs.append(rev(mlstm_outputs(rev(heads(qc)), rev(kch), rev(vch), lic, b_c, starts_c)))
        li, lf = (rev(t) for t in gate_logs(gates, direction))
        b_l, starts_l, _ = mlstm_states(rev(kh), rev(vh), li, lf, final_c)
        h_dirs.append(rev(mlstm_outputs(rev(qh), rev(kh), rev(vh), li, b_l, starts_l)))

    def merge(hs, og):
        h = (hs[0] + hs[1]).transpose(0, 2, 1, 3)
        Bm, L = h.shape[:2]
        h = rms_norm(h, out_gain.reshape(ML_HEADS, HEAD_DIM)).reshape(Bm, L, ML_WIDTH)
        return (h * jax.nn.sigmoid(og.astype(f32))).astype(og.dtype)

    y = merge(h_dirs, o)
    yc = merge(hc_dirs, oc) if qc is not None else None
    return y, yc


def sq_relu_mlp(h, w1, w2):
    return jnp.square(jax.nn.relu(h @ w1)) @ w2


def hybrid_layer(x, xc, mod, mod_c, g_mix, g_ffn, w_in, gm_v_gain, gm_w_s, gm_b_s, attn_sink,
                 ml_conv_w, ml_conv_b, ml_b_i, ml_b_f, ml_out_gain, w_out, w_ff1, w_ff2,
                 ang_row, ang_col, need_ctx):
    B, L, _ = x.shape
    Lc = xc.shape[1]
    sh1, sc1, gt1, sh2, sc2, gt2 = jnp.split(mod, 6, axis=-1)
    csh1, csc1, cgt1, csh2, csc2, cgt2 = jnp.split(mod_c, 6, axis=-1)
    cuts = np.cumsum(IN_SPLITS)[:-1].tolist()
    gu, gv, aq, ak, av, mq, mk, mv, mo, mg = jnp.split(
        modulate(rms_norm(x, g_mix), sh1, sc1) @ w_in, cuts, axis=-1)
    gu_c, gv_c, aq_c, ak_c, av_c, mq_c, mk_c, mv_c, mo_c, mg_c = jnp.split(
        modulate(rms_norm(xc, g_mix), csh1, csc1) @ w_in, cuts, axis=-1)

    ya = chunk_gmlp(jax.nn.gelu(gu), jax.nn.gelu(gv), gm_v_gain, gm_w_s, gm_b_s)

    k_c = ak_c.reshape(B, Lc, ATT_KV_HEADS, HEAD_DIM)
    v_c = av_c.reshape(B, Lc, ATT_KV_HEADS, HEAD_DIM)
    yb = window_attention(axial_rope(aq.reshape(B, L, ATT_HEADS, HEAD_DIM), ang_row, ang_col),
                          axial_rope(ak.reshape(B, L, ATT_KV_HEADS, HEAD_DIM), ang_row, ang_col),
                          av.reshape(B, L, ATT_KV_HEADS, HEAD_DIM), k_c, v_c, attn_sink)

    wq, wk = ml_conv_w[:, :ML_WIDTH], ml_conv_w[:, ML_WIDTH:]
    bq, bk = ml_conv_b[:ML_WIDTH], ml_conv_b[ML_WIDTH:]
    yc, yc_c = mlstm_mixer(short_conv(mq, wq, bq), short_conv(mk, wk, bk), mv, mo, mg,
                           short_conv(mk_c, wk, bk), mv_c, mg_c, ml_b_i, ml_b_f, ml_out_gain,
                           qc=short_conv(mq_c, wq, bq) if need_ctx else None,
                           oc=mo_c if need_ctx else None)

    x = x + gt1 * (jnp.concatenate([ya, yb, yc], axis=-1) @ w_out)
    x = x + gt2 * sq_relu_mlp(modulate(rms_norm(x, g_ffn), sh2, sc2), w_ff1, w_ff2)

    if need_ctx:
        ya_c = chunk_gmlp(jax.nn.gelu(gu_c), jax.nn.gelu(gv_c), gm_v_gain, gm_w_s, gm_b_s)
        yb_c = context_attention(aq_c.reshape(B, Lc, ATT_HEADS, HEAD_DIM), k_c, v_c, attn_sink)
        xc = xc + cgt1 * (jnp.concatenate([ya_c, yb_c, yc_c], axis=-1) @ w_out)
        xc = xc + cgt2 * sq_relu_mlp(modulate(rms_norm(xc, g_ffn), csh2, csc2), w_ff1, w_ff2)
    return x, xc


def setup_inputs(seed: int = 0) -> dict:
    key = jax.random.key(seed)
    ks = jax.random.split(key, 24)
    f32 = jnp.float32

    def nrm(k, shape, s):
        return jax.random.normal(k, shape, f32) * s

    return {
        'x': nrm(ks[0], (BATCH, SEQ, D_MODEL), 1.0),
        'c': nrm(ks[1], (BATCH, D_MODEL), 1.0),
        'ctx': nrm(ks[2], (BATCH, CTX_LEN, D_MODEL), 1.0),
        'c_ctx': nrm(ks[3], (D_MODEL,), 1.0),
        'w_ada': nrm(ks[4], (DEPTH, D_MODEL, 6 * D_MODEL), 0.5 * D_MODEL ** -0.5),
        'b_ada': nrm(ks[5], (DEPTH, 6 * D_MODEL), 0.02),
        'g_mix': 1.0 + nrm(ks[6], (DEPTH, D_MODEL), 0.02),
        'g_ffn': 1.0 + nrm(ks[7], (DEPTH, D_MODEL), 0.02),
        'w_in': nrm(ks[8], (DEPTH, D_MODEL, N_IN), D_MODEL ** -0.5),
        'gm_v_gain': 1.0 + nrm(ks[9], (DEPTH, GM_WIDTH), 0.02),
        'gm_w_s': nrm(ks[10], (DEPTH, GM_GROUPS, GM_CHUNK, GM_CHUNK), GM_CHUNK ** -0.5),
        'gm_b_s': 1.0 + nrm(ks[11], (DEPTH, GM_GROUPS, GM_CHUNK), 0.02),
        'attn_sink': nrm(ks[12], (DEPTH, ATT_HEADS), 0.5),
        'ml_conv_w': nrm(ks[13], (DEPTH, ML_CONV, 2 * ML_WIDTH), ML_CONV ** -0.5),
        'ml_conv_b': nrm(ks[14], (DEPTH, 2 * ML_WIDTH), 0.02),
        'ml_b_i': nrm(ks[15], (DEPTH, 2, ML_HEADS), 0.1),
        'ml_b_f': jax.random.uniform(ks[16], (DEPTH, 2, ML_HEADS), f32, 3.0, 6.0),
        'ml_out_gain': 1.0 + nrm(ks[17], (DEPTH, ML_WIDTH), 0.02),
        'w_out': nrm(ks[18], (DEPTH, MIX_WIDTH, D_MODEL), MIX_WIDTH ** -0.5),
        'w_ff1': nrm(ks[19], (DEPTH, D_MODEL, FFN_HIDDEN), D_MODEL ** -0.5),
        'w_ff2': nrm(ks[20], (DEPTH, FFN_HIDDEN, D_MODEL), FFN_HIDDEN ** -0.5),
        'g_final': 1.0 + nrm(ks[21], (D_MODEL,), 0.02),
    }


def reference(x, c, ctx, c_ctx, w_ada, b_ada, g_mix, g_ffn, w_in, gm_v_gain, gm_w_s, gm_b_s, attn_sink,
              ml_conv_w, ml_conv_b, ml_b_i, ml_b_f, ml_out_gain, w_out, w_ff1, w_ff2, g_final):
    rows = x.shape[1] // GRID_W
    ang_row, ang_col = axial_angles(rows)
    xc = ctx
    sc = jax.nn.silu(c)
    scc = jax.nn.silu(c_ctx)
    for layer in range(DEPTH):
        mod = (sc @ w_ada[layer] + b_ada[layer])[:, None, :]
        mod_c = scc @ w_ada[layer] + b_ada[layer]
        x, xc = hybrid_layer(x, xc, mod, mod_c, g_mix[layer], g_ffn[layer], w_in[layer],
                             gm_v_gain[layer], gm_w_s[layer], gm_b_s[layer], attn_sink[layer],
                             ml_conv_w[layer], ml_conv_b[layer], ml_b_i[layer], ml_b_f[layer],
                             ml_out_gain[layer], w_out[layer], w_ff1[layer], w_ff2[layer],
                             ang_row, ang_col, layer < DEPTH - 1)
    return rms_norm(x, g_final)
```

```python
import functools
import math

import jax
import jax.numpy as jnp
import numpy as np
from jax import lax
from jax.experimental import pallas as pl
from jax.experimental.pallas import tpu as pltpu

F32 = jnp.float32
BF16 = jnp.bfloat16

HEAD_DIM = 128
CHUNK = 128
GM_GROUPS = 4
ATT_HEADS = 6
ATT_KV_HEADS = 2
ATT_GROUP = ATT_HEADS // ATT_KV_HEADS
ML_HEADS = 6
GM_WIDTH = GM_GROUPS * HEAD_DIM
ATT_WIDTH = ATT_HEADS * HEAD_DIM
ATT_KV_WIDTH = ATT_KV_HEADS * HEAD_DIM
ML_WIDTH = ML_HEADS * HEAD_DIM
MIX_WIDTH = GM_WIDTH + ATT_WIDTH + ML_WIDTH
GRID_W = 64
ROPE_BASE = 10000.0
EPS = 1e-6
NEG_BIG = -1e30

_SRC = dict(gu=0, gv=512, aq=1024, ak=1792, av=2048, mq=2304, mk=3072, mv=3840, mo=4608, mg=5376)
_P_ORDER = (("aq", 768), ("mq", 768), ("gu", 512), ("gv", 512), ("ak", 256), ("av", 256),
            ("mk", 768), ("mv", 768), ("mo", 768))
P_WIDTH = sum(w for _, w in _P_ORDER)
_P_OFF = {}
_o = 0
for _n, _w in _P_ORDER:
    _P_OFF[_n] = _o
    _o += _w
GATE_WIDTH = 2 * HEAD_DIM

VMEM_LIMIT = 56 * 1024 * 1024


def _cparams(sem):
    return pltpu.CompilerParams(dimension_semantics=sem, vmem_limit_bytes=VMEM_LIMIT)


def _ada_kernel(c_ref, w_ref, b_ref, o_ref):
    c = c_ref[...]
    s = (c * jax.nn.sigmoid(c)).astype(BF16)
    o_ref[...] = jnp.dot(s, w_ref[...].astype(BF16), preferred_element_type=F32) + b_ref[...]


def _ada_call(cc, w_ada, b_ada):
    depth, d, n = w_ada.shape
    tn = math.gcd(n, 1024)
    return pl.pallas_call(
        _ada_kernel,
        out_shape=jax.ShapeDtypeStruct((depth, cc.shape[0], n), F32),
        grid=(depth, n // tn),
        in_specs=[pl.BlockSpec((cc.shape[0], d), lambda l, j: (0, 0)),
                  pl.BlockSpec((None, d, tn), lambda l, j: (l, 0, j)),
                  pl.BlockSpec((None, 1, tn), lambda l, j: (l, 0, j))],
        out_specs=pl.BlockSpec((None, cc.shape[0], tn), lambda l, j: (l, 0, j)),
        compiler_params=_cparams(("arbitrary", "arbitrary")),
        name="ada_mod",
    )(cc, w_ada, b_ada.reshape(depth, 1, n))


def _inproj_kernel(x_ref, g_ref, sh_ref, sc_ref, w_ref, wg_ref, p_ref, gate_ref, h_scr):
    @pl.when(pl.program_id(1) == 0)
    def _():
        x = x_ref[...]
        y = x * lax.rsqrt(jnp.mean(x * x, axis=-1, keepdims=True) + EPS) * g_ref[...]
        hb = (y * (1.0 + sc_ref[0]) + sh_ref[0]).astype(BF16)
        h_scr[...] = hb
        gate_ref[...] = jnp.dot(hb, wg_ref[...], preferred_element_type=F32)

    p_ref[...] = jnp.dot(h_scr[...], w_ref[...], preferred_element_type=F32).astype(BF16)


def _inproj_call(xs, gain, mod3, w, wg, geom, layer_mod_base):
    rows, d = xs.shape
    tm, seg = geom["tm"], geom["seg"]
    tn = 768
    return pl.pallas_call(
        _inproj_kernel,
        out_shape=(jax.ShapeDtypeStruct((rows, P_WIDTH), BF16),
                   jax.ShapeDtypeStruct((rows, GATE_WIDTH), F32)),
        grid=(pl.cdiv(rows, tm), P_WIDTH // tn),
        in_specs=[pl.BlockSpec((tm, d), lambda i, j: (i, 0)),
                  pl.BlockSpec((1, d), lambda i, j: (0, 0)),
                  pl.BlockSpec((1, 1, d), lambda i, j: (layer_mod_base + seg(i) * 6 + 0, 0, 0)),
                  pl.BlockSpec((1, 1, d), lambda i, j: (layer_mod_base + seg(i) * 6 + 1, 0, 0)),
                  pl.BlockSpec((d, tn), lambda i, j: (0, j)),
                  pl.BlockSpec((d, GATE_WIDTH), lambda i, j: (0, 0))],
        out_specs=(pl.BlockSpec((tm, tn), lambda i, j: (i, j)),
                   pl.BlockSpec((tm, GATE_WIDTH), lambda i, j: (i, 0))),
        scratch_shapes=[pltpu.VMEM((tm, d), BF16)],
        compiler_params=_cparams(("arbitrary", "arbitrary")),
        name="in_proj",
    )(xs, gain, mod3, mod3, w, wg)


def _gelu(x):
    return jax.nn.gelu(x, approximate=True)


def _gmlp_kernel(gu_ref, gv_ref, gain_ref, ws_ref, bs_ref, o_ref):
    tr = gu_ref.shape[0]
    for c in range(tr // CHUNK):
        rs = slice(c * CHUNK, (c + 1) * CHUNK)
        u = _gelu(gu_ref[rs, :].astype(F32))
        v = _gelu(gv_ref[rs, :].astype(F32))
        for g in range(GM_GROUPS):
            cs = slice(g * HEAD_DIM, (g + 1) * HEAD_DIM)
            vg = v[:, cs]
            vn = vg * lax.rsqrt(jnp.mean(vg * vg, axis=-1, keepdims=True) + EPS) * gain_ref[:, cs]
            s = jnp.dot(ws_ref[g], vn.astype(BF16), preferred_element_type=F32) + bs_ref[:, cs]
            o_ref[rs, cs] = (u[:, cs] * s).astype(BF16)


def _gmlp_call(p, v_gain, ws, bs_full, tr):
    rows = p.shape[0]
    gu_blk = _P_OFF["gu"] // GM_WIDTH
    gv_blk = _P_OFF["gv"] // GM_WIDTH
    return pl.pallas_call(
        _gmlp_kernel,
        out_shape=jax.ShapeDtypeStruct((rows, GM_WIDTH), BF16),
        grid=(rows // tr,),
        in_specs=[pl.BlockSpec((tr, GM_WIDTH), lambda i: (i, gu_blk)),
                  pl.BlockSpec((tr, GM_WIDTH), lambda i: (i, gv_blk)),
                  pl.BlockSpec((1, GM_WIDTH), lambda i: (0, 0)),
                  pl.BlockSpec((GM_GROUPS, CHUNK, CHUNK), lambda i: (0, 0, 0)),
                  pl.BlockSpec((CHUNK, GM_WIDTH), lambda i: (0, 0))],
        out_specs=pl.BlockSpec((tr, GM_WIDTH), lambda i: (i, 0)),
        compiler_params=_cparams(("arbitrary",)),
        name="gmlp",
    )(p, p, v_gain, ws, bs_full)


def _rope(x_bf, cos, sin, prot):
    rot = jnp.dot(x_bf, prot, preferred_element_type=F32)
    return (x_bf.astype(F32) * cos + rot * sin).astype(BF16)


def _attn_kernel(nb, n_lat, sink_ref, q_ref, kp_ref, kc_ref, kn_ref, vp_ref, vc_ref, vn_ref,
                 kx_ref, vx_ref, cp_ref, sp_ref, cc_ref, sc_ref, cn_ref, sn_ref, prot_ref, o_ref):
    n = pl.program_id(0)
    is_lat = n < n_lat
    pos = n % nb
    has_cur = is_lat.astype(jnp.int32)
    has_prev = jnp.logical_and(is_lat, pos != 0).astype(jnp.int32)
    has_next = jnp.logical_and(is_lat, pos != nb - 1).astype(jnp.int32)
    prot = prot_ref[...]
    scale = HEAD_DIM ** -0.5

    qi = lax.broadcasted_iota(jnp.int32, (ATT_GROUP * CHUNK, 3 * CHUNK), 0) & (CHUNK - 1)
    kj = lax.broadcasted_iota(jnp.int32, (ATT_GROUP * CHUNK, 3 * CHUNK), 1)
    band = jnp.where(jnp.abs(kj - CHUNK - qi) <= CHUNK, 1, 0)
    valid = band * jnp.where(kj < CHUNK, has_prev, jnp.where(kj < 2 * CHUNK, has_cur, has_next)) > 0
    hrow = lax.broadcasted_iota(jnp.int32, (ATT_GROUP * CHUNK, 1), 0)

    cos_q, sin_q = cc_ref[...], sc_ref[...]
    for g in range(ATT_KV_HEADS):
        cs = slice(g * HEAD_DIM, (g + 1) * HEAD_DIM)
        k_all = jnp.concatenate([_rope(kp_ref[:, cs], cp_ref[...], sp_ref[...], prot),
                                 _rope(kc_ref[:, cs], cos_q, sin_q, prot),
                                 _rope(kn_ref[:, cs], cn_ref[...], sn_ref[...], prot),
                                 kx_ref[:, cs]], axis=0)
        v_all = jnp.concatenate([vp_ref[:, cs], vc_ref[:, cs], vn_ref[:, cs], vx_ref[:, cs]], axis=0)
        q3 = jnp.concatenate(
            [_rope(q_ref[:, (g * ATT_GROUP + h) * HEAD_DIM:(g * ATT_GROUP + h + 1) * HEAD_DIM],
                   cos_q, sin_q, prot) for h in range(ATT_GROUP)], axis=0)
        s = lax.dot_general(q3, k_all, (((1,), (1,)), ((), ())), preferred_element_type=F32) * scale
        s_loc = jnp.where(valid, s[:, :3 * CHUNK], NEG_BIG)
        s_ctx = s[:, 3 * CHUNK:]
        sk = jnp.where(hrow < CHUNK, sink_ref[g * ATT_GROUP],
                       jnp.where(hrow < 2 * CHUNK, sink_ref[g * ATT_GROUP + 1], sink_ref[g * ATT_GROUP + 2]))
        m = jnp.maximum(jnp.maximum(jnp.max(s_loc, axis=1, keepdims=True),
                                    jnp.max(s_ctx, axis=1, keepdims=True)), sk)
        p_loc = jnp.exp(s_loc - m)
        p_ctx = jnp.exp(s_ctx - m)
        den = jnp.sum(p_loc, axis=1, keepdims=True) + jnp.sum(p_ctx, axis=1, keepdims=True) + jnp.exp(sk - m)
        p_all = jnp.concatenate([p_loc, p_ctx], axis=1).astype(BF16)
        o = jnp.dot(p_all, v_all, preferred_element_type=F32) / den
        for h in range(ATT_GROUP):
            hs = (g * ATT_GROUP + h) * HEAD_DIM
            o_ref[:, hs:hs + HEAD_DIM] = o[h * CHUNK:(h + 1) * CHUNK, :].astype(BF16)


def _attn_call(p, sink, cos_tab, sin_tab, prot, geom):
    rows = p.shape[0]
    nb, n_lat, n_tok, lc = geom["nb"], geom["n_lat"], geom["n_tok"], geom["lc"]
    ncs = lc // CHUNK
    ak_blk = _P_OFF["ak"] // ATT_KV_WIDTH
    av_blk = _P_OFF["av"] // ATT_KV_WIDTH
    ctx_base = (n_lat * CHUNK) // lc

    def is_lat(n):
        return n < n_lat

    def prev_ok(n):
        return jnp.logical_and(is_lat(n), n % nb != 0)

    def next_ok(n):
        return jnp.logical_and(is_lat(n), n % nb != nb - 1)

    def prev_i(n):
        return jnp.where(prev_ok(n), n - 1, n)

    def next_i(n):
        return jnp.where(next_ok(n), n + 1, n)

    def tab(ok, n):
        return jnp.where(ok, n % nb, nb)

    def ctx_i(n):
        return ctx_base + jnp.where(is_lat(n), n // nb, (n - n_lat) // ncs)

    kv = lambda f, blk: pl.BlockSpec((CHUNK, ATT_KV_WIDTH), lambda n: (f(n), blk))
    tb = lambda f: pl.BlockSpec((CHUNK, HEAD_DIM), lambda n: (f(n), 0))
    return pl.pallas_call(
        functools.partial(_attn_kernel, nb, n_lat),
        out_shape=jax.ShapeDtypeStruct((rows, ATT_WIDTH), BF16),
        grid=(n_tok,),
        in_specs=[pl.BlockSpec(memory_space=pltpu.SMEM),
                  pl.BlockSpec((CHUNK, ATT_WIDTH), lambda n: (n, 0)),
                  kv(prev_i, ak_blk), kv(lambda n: n, ak_blk), kv(next_i, ak_blk),
                  kv(prev_i, av_blk), kv(lambda n: n, av_blk), kv(next_i, av_blk),
                  pl.BlockSpec((lc, ATT_KV_WIDTH), lambda n: (ctx_i(n), ak_blk)),
                  pl.BlockSpec((lc, ATT_KV_WIDTH), lambda n: (ctx_i(n), av_blk)),
                  tb(lambda n: tab(prev_ok(n), n - 1)), tb(lambda n: tab(prev_ok(n), n - 1)),
                  tb(lambda n: tab(is_lat(n), n)), tb(lambda n: tab(is_lat(n), n)),
                  tb(lambda n: tab(next_ok(n), n + 1)), tb(lambda n: tab(next_ok(n), n + 1)),
                  pl.BlockSpec((HEAD_DIM, HEAD_DIM), lambda n: (0, 0))],
        out_specs=pl.BlockSpec((CHUNK, ATT_WIDTH), lambda n: (n, 0)),
        compiler_params=_cparams(("arbitrary",)),
        name="window_attn",
    )(sink, p, p, p, p, p, p, p, p, p, cos_tab, sin_tab, cos_tab, sin_tab, cos_tab, sin_tab, prot)


def _conv_kernel(starts, ends, x_ref, xp_ref, xn_ref, w_ref, b_ref, o_ref):
    tr = x_ref.shape[0]
    x = x_ref[...].astype(F32)
    rg = pl.program_id(0) * tr + lax.broadcasted_iota(jnp.int32, (tr, 1), 0)
    rl = lax.broadcasted_iota(jnp.int32, (tr, 1), 0)
    is_start = functools.reduce(jnp.logical_or, [rg == s for s in starts])
    is_end = functools.reduce(jnp.logical_or, [rg == e for e in ends])
    prev_row = xp_ref[7:8, :].astype(F32)
    next_row = xn_ref[0:1, :].astype(F32)
    x_prev = jnp.where(rl == 0, prev_row, pltpu.roll(x, 1, 0))
    x_prev = jnp.where(is_start, 0.0, x_prev)
    x_next = jnp.where(rl == tr - 1, next_row, pltpu.roll(x, tr - 1, 0))
    x_next = jnp.where(is_end, 0.0, x_next)
    y = b_ref[0:1, :] + x_prev * w_ref[0:1, :] + x * w_ref[1:2, :] + x_next * w_ref[2:3, :]
    y = y * jax.nn.sigmoid(y) * w_ref[3:4, :]
    o_ref[...] = y.astype(BF16)


def _conv_call(p, w4, b, geom, tr):
    rows = p.shape[0]
    nt = rows // tr
    hb = tr // 8
    mq_blk = _P_OFF["mq"] // ML_WIDTH
    mk_blk = _P_OFF["mk"] // ML_WIDTH

    def col(j):
        return jnp.where(j == 0, mq_blk, mk_blk)

    return pl.pallas_call(
        functools.partial(_conv_kernel, geom["starts"], geom["ends"]),
        out_shape=jax.ShapeDtypeStruct((rows, 2 * ML_WIDTH), BF16),
        grid=(nt, 2),
        in_specs=[pl.BlockSpec((tr, ML_WIDTH), lambda i, j: (i, col(j))),
                  pl.BlockSpec((8, ML_WIDTH), lambda i, j: (jnp.maximum(i * hb - 1, 0), col(j))),
                  pl.BlockSpec((8, ML_WIDTH), lambda i, j: (jnp.minimum((i + 1) * hb, rows // 8 - 1), col(j))),
                  pl.BlockSpec((8, ML_WIDTH), lambda i, j: (0, j)),
                  pl.BlockSpec((8, ML_WIDTH), lambda i, j: (0, j))],
        out_specs=pl.BlockSpec((tr, ML_WIDTH), lambda i, j: (i, j)),
        compiler_params=_cparams(("arbitrary", "arbitrary")),
        name="short_conv",
    )(p, p, p, w4, b)


def _split3(x):
    hi = x.astype(BF16)
    r1 = x - hi.astype(F32)
    mid = r1.astype(BF16)
    lo = (r1 - mid.astype(F32)).astype(BF16)
    return hi, mid, lo


def _mlstm_kernel(qf_ref, kf_ref, vf_ref, gf_ref, qb_ref, kb_ref, vb_ref, gb_ref, bias_ref,
                  hf_ref, hb_ref, ct_scr, m_scr):
    @pl.when(pl.program_id(1) == 0)
    def _():
        ct_scr[...] = jnp.zeros_like(ct_scr)
        m_scr[...] = jnp.zeros_like(m_scr)

    row = lax.broadcasted_iota(jnp.int32, (CHUNK, CHUNK), 0)
    col = lax.broadcasted_iota(jnp.int32, (CHUNK, CHUNK), 1)
    e1 = (col == 0).astype(F32)
    dirs = ((qf_ref, kf_ref, vf_ref, gf_ref, hf_ref), (qb_ref, kb_ref, vb_ref, gb_ref, hb_ref))
    for d, (q_ref, k_ref, v_ref, g_ref, o_ref) in enumerate(dirs):
        order = (col <= row) if d == 0 else (col >= row)
        tri = order.astype(BF16)
        pre_i = g_ref[:, :HEAD_DIM] + bias_ref[0:1, :]
        pre_f = g_ref[:, HEAD_DIM:] + bias_ref[1:2, :]
        li = pre_i
        lf = jnp.minimum(pre_f, 0.0) - jnp.log1p(jnp.exp(-jnp.abs(pre_f)))
        hi, mid, lo = _split3(lf)
        cs3 = jnp.dot(tri, jnp.concatenate([hi, mid, lo], axis=1), preferred_element_type=F32)
        b_all = cs3[:, :CHUNK] + cs3[:, CHUNK:2 * CHUNK] + cs3[:, 2 * CHUNK:]
        g_row = b_all[CHUNK - 1:CHUNK, :] if d == 0 else b_all[0:1, :]
        m_row = m_scr[d, 0:1, :]
        a_all = g_row - b_all + li
        mloc_row = jnp.max(a_all, axis=0, keepdims=True)
        w_all = jnp.exp(a_all - mloc_row)
        u_t = (li - b_all).T
        e_all = b_all + m_row
        m_new = jnp.maximum(g_row + m_row, mloc_row)
        a_old = jnp.exp(g_row + m_row - m_new)
        a_new = jnp.exp(mloc_row - m_new)
        m_scr[d, 0:1, :] = m_new
        for h in range(ML_HEADS):
            l = d * ML_HEADS + h
            cs = slice(h * HEAD_DIM, (h + 1) * HEAD_DIM)
            qh, kh, vh = q_ref[:, cs], k_ref[:, cs], v_ref[:, cs]
            bc, ec, wc = b_all[:, l:l + 1], e_all[:, l:l + 1], w_all[:, l:l + 1]
            dm = jnp.where(order, bc + u_t[l:l + 1, :], NEG_BIG)
            m_t = jnp.maximum(ec, jnp.max(dm, axis=1, keepdims=True))
            qk = lax.dot_general(qh, kh, (((1,), (1,)), ((), ())), preferred_element_type=F32)
            s = (qk * jnp.exp(dm - m_t)).astype(BF16)
            inter = jnp.exp(ec - m_t)
            v_aug = jnp.concatenate([vh, e1.astype(BF16)], axis=1)
            ct = ct_scr[l]
            r = (jnp.dot(s, v_aug, preferred_element_type=F32)
                 + inter * jnp.dot(qh, ct.astype(BF16), preferred_element_type=F32))
            den = r[:, HEAD_DIM:HEAD_DIM + 1]
            hh = r[:, :HEAD_DIM] / jnp.maximum(jnp.abs(den), jnp.exp(-m_t))
            o_ref[:, cs] = hh.astype(BF16)
            vw_aug = jnp.concatenate([(vh.astype(F32) * wc).astype(BF16), (e1 * wc).astype(BF16)], axis=1)
            dct = lax.dot_general(kh, vw_aug, (((0,), (0,)), ((), ())), preferred_element_type=F32)
            ct_scr[l] = a_old[:, l:l + 1] * ct + a_new[:, l:l + 1] * dct


def _mlstm_call(qk, p, gates, bias, geom):
    rows = p.shape[0]
    nb, n_lat, lc, batch = geom["nb"], geom["n_lat"], geom["lc"], geom["batch"]
    ncs = lc // CHUNK
    mv_blk = _P_OFF["mv"] // ML_WIDTH

    def fwd(b, s):
        return jnp.where(s < ncs, n_lat + b * ncs + s, b * nb + (s - ncs))

    def bwd(b, s):
        return jnp.where(s < ncs, n_lat + b * ncs + (ncs - 1 - s), b * nb + (nb - 1 - (s - ncs)))

    def specs(f):
        return [pl.BlockSpec((CHUNK, ML_WIDTH), lambda b, s: (f(b, s), 0)),
                pl.BlockSpec((CHUNK, ML_WIDTH), lambda b, s: (f(b, s), 1)),
                pl.BlockSpec((CHUNK, ML_WIDTH), lambda b, s: (f(b, s), mv_blk)),
                pl.BlockSpec((CHUNK, GATE_WIDTH), lambda b, s: (f(b, s), 0))]

    out = jax.ShapeDtypeStruct((rows, ML_WIDTH), BF16)
    return pl.pallas_call(
        _mlstm_kernel,
        out_shape=(out, out),
        grid=(batch, ncs + nb),
        in_specs=specs(fwd) + specs(bwd) + [pl.BlockSpec((8, HEAD_DIM), lambda b, s: (0, 0))],
        out_specs=(pl.BlockSpec((CHUNK, ML_WIDTH), lambda b, s: (fwd(b, s), 0)),
                   pl.BlockSpec((CHUNK, ML_WIDTH), lambda b, s: (bwd(b, s), 0))),
        scratch_shapes=[pltpu.VMEM((2 * ML_HEADS, HEAD_DIM, 2 * HEAD_DIM), F32),
                        pltpu.VMEM((2, 8, HEAD_DIM), F32)],
        compiler_params=_cparams(("arbitrary", "arbitrary")),
        name="mlstm",
    )(qk, qk, p, gates, qk, qk, p, gates, bias)


def _mixout_kernel(x_ref, ya_ref, yb_ref, hf_ref, hb_ref, mo_ref, og_ref, gt_ref, w_ref, o_ref):
    hs = hf_ref[...].astype(F32) + hb_ref[...].astype(F32)
    parts = [ya_ref[...], yb_ref[...]]
    for h in range(ML_HEADS):
        cs = slice(h * HEAD_DIM, (h + 1) * HEAD_DIM)
        xh = hs[:, cs]
        y = xh * lax.rsqrt(jnp.mean(xh * xh, axis=-1, keepdims=True) + EPS) * og_ref[:, cs]
        parts.append((y * jax.nn.sigmoid(mo_ref[:, cs].astype(F32))).astype(BF16))
    lhs = jnp.concatenate(parts, axis=1)
    acc = jnp.dot(lhs, w_ref[...], preferred_element_type=F32)
    o_ref[...] = x_ref[...] + gt_ref[0] * acc


def _mixout_call(xs, ya, yb, hf, hb, p, out_gain, mod3, w_out, geom, layer_mod_base, n_tiles):
    rows, d = xs.shape
    tm, seg = geom["tm_mix"], geom["seg_mix"]
    mo_blk = _P_OFF["mo"] // ML_WIDTH
    row = lambda w, blk=0: pl.BlockSpec((tm, w), lambda i: (i, blk))
    return pl.pallas_call(
        _mixout_kernel,
        out_shape=jax.ShapeDtypeStruct((rows, d), F32),
        grid=(n_tiles,),
        in_specs=[row(d), row(GM_WIDTH), row(ATT_WIDTH), row(ML_WIDTH), row(ML_WIDTH), row(ML_WIDTH, mo_blk),
                  pl.BlockSpec((1, ML_WIDTH), lambda i: (0, 0)),
                  pl.BlockSpec((1, 1, d), lambda i: (layer_mod_base + seg(i) * 6 + 2, 0, 0)),
                  pl.BlockSpec((MIX_WIDTH, d), lambda i: (0, 0))],
        out_specs=row(d),
        input_output_aliases={0: 0},
        compiler_params=_cparams(("arbitrary",)),
        name="mix_out",
    )(xs, ya, yb, hf, hb, p, out_gain, mod3, w_out)


def _ffn_kernel(final, x_ref, g_ref, sh_ref, sc_ref, gt_ref, w1_ref, w2_ref, gf_ref, o_ref, h_scr):
    j = pl.program_id(1)

    @pl.when(j == 0)
    def _():
        x = x_ref[...]
        y = x * lax.rsqrt(jnp.mean(x * x, axis=-1, keepdims=True) + EPS) * g_ref[...]
        h_scr[...] = (y * (1.0 + sc_ref[0]) + sh_ref[0]).astype(BF16)
        o_ref[...] = jnp.zeros_like(o_ref)

    a = jnp.maximum(jnp.dot(h_scr[...], w1_ref[...], preferred_element_type=F32), 0.0)
    o_ref[...] += jnp.dot((a * a).astype(BF16), w2_ref[...], preferred_element_type=F32)

    @pl.when(j == pl.num_programs(1) - 1)
    def _():
        xo = x_ref[...] + gt_ref[0] * o_ref[...]
        if final:
            xo = xo * lax.rsqrt(jnp.mean(xo * xo, axis=-1, keepdims=True) + EPS) * gf_ref[...]
        o_ref[...] = xo


def _ffn_call(xs, gain, mod3, w1, w2, g_final, geom, layer_mod_base, n_tiles, final):
    rows, d = xs.shape
    hidden = w1.shape[1]
    tm, seg = geom["tm_mix"], geom["seg_mix"]
    th = min(1024, hidden)
    out_rows = n_tiles * tm if final else rows
    modspec = lambda k: pl.BlockSpec((1, 1, d), lambda i, j: (layer_mod_base + seg(i) * 6 + k, 0, 0))
    return pl.pallas_call(
        functools.partial(_ffn_kernel, final),
        out_shape=jax.ShapeDtypeStruct((out_rows, d), F32),
        grid=(n_tiles, hidden // th),
        in_specs=[pl.BlockSpec((tm, d), lambda i, j: (i, 0)),
                  pl.BlockSpec((1, d), lambda i, j: (0, 0)),
                  modspec(3), modspec(4), modspec(5),
                  pl.BlockSpec((d, th), lambda i, j: (0, j)),
                  pl.BlockSpec((th, d), lambda i, j: (j, 0)),
                  pl.BlockSpec((1, d), lambda i, j: (0, 0))],
        out_specs=pl.BlockSpec((tm, d), lambda i, j: (i, 0)),
        scratch_shapes=[pltpu.VMEM((tm, d), BF16)],
        compiler_params=_cparams(("arbitrary", "arbitrary")),
        name="ffn_final" if final else "ffn",
    )(xs, gain, mod3, mod3, mod3, w1, w2, g_final)


def _rope_tables(seq):
    rows = seq // GRID_W
    row = jnp.broadcast_to(jnp.arange(rows)[:, None], (rows, GRID_W)).reshape(-1).astype(F32)
    col = jnp.broadcast_to(jnp.arange(GRID_W)[None, :], (rows, GRID_W)).reshape(-1).astype(F32)
    half = HEAD_DIM // 2
    freqs = 1.0 / (ROPE_BASE ** (jnp.arange(0, half, 2, dtype=F32) / half))
    ar, ac = row[:, None] * freqs, col[:, None] * freqs
    cos = jnp.concatenate([jnp.cos(ar), jnp.cos(ar), jnp.cos(ac), jnp.cos(ac)], axis=-1)
    sin = jnp.concatenate([jnp.sin(ar), jnp.sin(ar), jnp.sin(ac), jnp.sin(ac)], axis=-1)
    cos = jnp.concatenate([cos, jnp.ones((CHUNK, HEAD_DIM), F32)], axis=0)
    sin = jnp.concatenate([sin, jnp.zeros((CHUNK, HEAD_DIM), F32)], axis=0)
    i = np.arange(HEAD_DIM)
    quarter = HEAD_DIM // 4
    prot = np.zeros((HEAD_DIM, HEAD_DIM), np.float32)
    first = (i % (2 * quarter)) < quarter
    prot[(i + quarter)[first], i[first]] = -1.0
    prot[(i - quarter)[~first], i[~first]] = 1.0
    return cos, sin, jnp.asarray(prot, BF16)


def _geometry(batch, seq, lc, d):
    tm = 1024 if seq % 1024 == 0 else 512
    tm_mix = 512
    assert seq % tm == 0 and seq % CHUNK == 0 and lc % CHUNK == 0 and (batch * seq) % lc == 0
    r_lat, r_ctx = batch * seq, batch * lc
    rows = r_lat + r_ctx
    assert rows % tm_mix == 0
    starts = tuple(b * seq for b in range(batch)) + tuple(r_lat + b * lc for b in range(batch))
    ends = tuple((b + 1) * seq - 1 for b in range(batch)) + tuple(r_lat + (b + 1) * lc - 1 for b in range(batch))
    return dict(tm=tm, tm_mix=tm_mix, rows=rows, r_lat=r_lat, r_ctx=r_ctx, batch=batch, lc=lc,
                nb=seq // CHUNK, n_lat=r_lat // CHUNK, n_tok=(r_lat + r_ctx) // CHUNK,
                seg=lambda i: jnp.minimum(i // (seq // tm), batch),
                seg_mix=lambda i: jnp.minimum(i // (seq // tm_mix), batch),
                starts=starts, ends=ends)


def _permute_w_in(w):
    cols = [w[:, _SRC[n]:_SRC[n] + wd] for n, wd in _P_ORDER]
    return jnp.concatenate(cols, axis=1).astype(BF16)


def _gate_w(w):
    d = w.shape[0]
    mg = w[:, _SRC["mg"]:_SRC["mg"] + 4 * ML_HEADS].reshape(d, 2, 2, ML_HEADS)
    halves = []
    for t in range(2):
        part = mg[:, :, t, :].reshape(d, 2 * ML_HEADS)
        halves.append(jnp.pad(part, ((0, 0), (0, HEAD_DIM - 2 * ML_HEADS))))
    return jnp.concatenate(halves, axis=1).astype(BF16)


def kernel(x, c, ctx, c_ctx, w_ada, b_ada, g_mix, g_ffn, w_in, gm_v_gain, gm_w_s, gm_b_s, attn_sink,
           ml_conv_w, ml_conv_b, ml_b_i, ml_b_f, ml_out_gain, w_out, w_ff1, w_ff2, g_final):
    batch, seq, d = x.shape
    lc = ctx.shape[1]
    depth = w_ada.shape[0]
    geom = _geometry(batch, seq, lc, d)
    rows, tm = geom["rows"], geom["tm"]

    xs = jnp.concatenate([x.reshape(-1, d), ctx.reshape(-1, d)], axis=0)

    cc = jnp.concatenate([c, c_ctx[None, :], jnp.zeros((8 - batch - 1, d), F32)], axis=0)
    mod = _ada_call(cc, w_ada, b_ada)
    mod3 = mod[:, :batch + 1, :].reshape(depth * (batch + 1) * 6, 1, d)

    cos_tab, sin_tab, prot = _rope_tables(seq)
    scale = HEAD_DIM ** -0.5
    q_scale = jnp.concatenate([jnp.full((ML_WIDTH,), scale, F32), jnp.ones((ML_WIDTH,), F32)])

    for layer in range(depth):
        last = layer == depth - 1
        base = layer * (batch + 1) * 6
        w_p = _permute_w_in(w_in[layer])
        w_g = _gate_w(w_in[layer])
        p, gates = _inproj_call(xs, g_mix[layer][None, :], mod3, w_p, w_g, geom, base)

        bs_full = jnp.repeat(gm_b_s[layer].T, HEAD_DIM, axis=1)
        ya = _gmlp_call(p, gm_v_gain[layer][None, :], gm_w_s[layer].astype(BF16), bs_full, 512)

        yb = _attn_call(p, attn_sink[layer], cos_tab, sin_tab, prot, geom)

        w4 = jnp.concatenate([ml_conv_w[layer], q_scale[None, :], jnp.zeros((4, 2 * ML_WIDTH), F32)], axis=0)
        b8 = jnp.concatenate([ml_conv_b[layer][None, :], jnp.zeros((7, 2 * ML_WIDTH), F32)], axis=0)
        qk = _conv_call(p, w4, b8, geom, 256)
        bias = jnp.zeros((8, HEAD_DIM), F32)
        bias = bias.at[0, :2 * ML_HEADS].set(ml_b_i[layer].reshape(-1))
        bias = bias.at[1, :2 * ML_HEADS].set(ml_b_f[layer].reshape(-1))
        hf, hb = _mlstm_call(qk, p, gates, bias, geom)

        n_mix = (geom["r_lat"] if last else rows) // geom["tm_mix"]
        xs = _mixout_call(xs, ya, yb, hf, hb, p, ml_out_gain[layer][None, :], mod3, w_out[layer].astype(BF16),
                          geom, base, n_mix)
        xs = _ffn_call(xs, g_ffn[layer][None, :], mod3, w_ff1[layer].astype(BF16), w_ff2[layer].astype(BF16),
                       g_final[None, :], geom, base, n_mix, last)
    return xs.reshape(batch, seq, d)
```

```python
import functools
import math

import jax
import jax.numpy as jnp
import numpy as np
from jax import lax
from jax.experimental import pallas as pl
from jax.experimental.pallas import tpu as pltpu

F32 = jnp.float32
BF16 = jnp.bfloat16

HEAD_DIM = 128
CHUNK = 128
GM_GROUPS = 4
ATT_HEADS = 6
ATT_KV_HEADS = 2
ATT_GROUP = ATT_HEADS // ATT_KV_HEADS
ML_HEADS = 6
GM_WIDTH = GM_GROUPS * HEAD_DIM
ATT_WIDTH = ATT_HEADS * HEAD_DIM
ATT_KV_WIDTH = ATT_KV_HEADS * HEAD_DIM
ML_WIDTH = ML_HEADS * HEAD_DIM
MIX_WIDTH = GM_WIDTH + ATT_WIDTH + ML_WIDTH
GRID_W = 64
ROPE_BASE = 10000.0
EPS = 1e-6
NEG_BIG = -1e30
LOG2E = math.log2(math.e)
LN_INV_SCALE = 0.5 * math.log(HEAD_DIM)

ATT_TILE = 2 * CHUNK
CONV_TILE = 4 * CHUNK
GATE_LANES = 16
GP_ROWS = 6 * GATE_LANES

_SRC = dict(gu=0, gv=512, aq=1024, ak=1792, av=2048, mq=2304, mk=3072, mv=3840, mo=4608, mg=5376)
_P_ORDER = (("aq", 768), ("mq", 768), ("gu", 512), ("gv", 512), ("ak", 256), ("av", 256),
            ("mk", 768), ("mv", 768), ("mo", 768))
P_WIDTH = sum(w for _, w in _P_ORDER)
_P_OFF = {}
_o = 0
for _n, _w in _P_ORDER:
    _P_OFF[_n] = _o
    _o += _w
GATE_WIDTH = HEAD_DIM

VMEM_LIMIT = 56 * 1024 * 1024


def _cparams(sem):
    return pltpu.CompilerParams(dimension_semantics=sem, vmem_limit_bytes=VMEM_LIMIT)


def _any(preds):
    return functools.reduce(jnp.logical_or, preds)


def _ada_kernel(c_ref, w_ref, b_ref, o_ref):
    c = c_ref[...]
    s = (c * jax.nn.sigmoid(c)).astype(BF16)
    o_ref[...] = jnp.dot(s, w_ref[...].astype(BF16), preferred_element_type=F32) + b_ref[...]


def _ada_call(cc, w_ada, b_ada):
    depth, d, n = w_ada.shape
    tn = math.gcd(n, 1024)
    return pl.pallas_call(
        _ada_kernel,
        out_shape=jax.ShapeDtypeStruct((depth, cc.shape[0], n), F32),
        grid=(depth, n // tn),
        in_specs=[pl.BlockSpec((cc.shape[0], d), lambda l, j: (0, 0)),
                  pl.BlockSpec((None, d, tn), lambda l, j: (l, 0, j)),
                  pl.BlockSpec((None, 1, tn), lambda l, j: (l, 0, j))],
        out_specs=pl.BlockSpec((None, cc.shape[0], tn), lambda l, j: (l, 0, j)),
        compiler_params=_cparams(("arbitrary", "arbitrary")),
        name="ada_mod",
    )(cc, w_ada, b_ada.reshape(depth, 1, n))


def _norm_mod(x, gain, scale, shift):
    y = x * lax.rsqrt(jnp.mean(x * x, axis=-1, keepdims=True) + EPS) * gain
    return (y * (1.0 + scale) + shift).astype(BF16)


def _inproj_body(fill, w_ref, wg_ref, p_ref, gate_ref, h_scr):
    @pl.when(pl.program_id(1) == 0)
    def _():
        fill()
        gate_ref[...] = jnp.dot(h_scr[...], wg_ref[...], preferred_element_type=F32)

    p_ref[...] = jnp.dot(h_scr[...], w_ref[...], preferred_element_type=F32).astype(BF16)


def _inproj_kernel(x_ref, g_ref, sh_ref, sc_ref, w_ref, wg_ref, p_ref, gate_ref, h_scr):
    def fill():
        h_scr[...] = _norm_mod(x_ref[...], g_ref[...], sc_ref[0], sh_ref[0])

    _inproj_body(fill, w_ref, wg_ref, p_ref, gate_ref, h_scr)


def _inproj_split_kernel(n_lat_tiles, x_ref, c_ref, g_ref, sh_ref, sc_ref, w_ref, wg_ref, p_ref, gate_ref, h_scr):
    def fill():
        i = pl.program_id(0)

        @pl.when(i < n_lat_tiles)
        def _():
            h_scr[...] = _norm_mod(x_ref[...], g_ref[...], sc_ref[0], sh_ref[0])

        @pl.when(i >= n_lat_tiles)
        def _():
            h_scr[0:c_ref.shape[0], :] = _norm_mod(c_ref[...], g_ref[...], sc_ref[0], sh_ref[0])

    _inproj_body(fill, w_ref, wg_ref, p_ref, gate_ref, h_scr)


def _inproj_call(srcs, gain, mod3, w, wg, geom, layer_mod_base):
    d = srcs[0].shape[1]
    rows, tm, seg = geom["rows"], geom["tm"], geom["seg"]
    tn = 768
    modspec = lambda k: pl.BlockSpec((1, 1, d), lambda i, j: (layer_mod_base + seg(i) * 6 + k, 0, 0))
    if len(srcs) == 1:
        body = _inproj_kernel
        src_specs = [pl.BlockSpec((tm, d), lambda i, j: (i, 0))]
    else:
        n_lat_tiles = geom["r_lat"] // tm
        assert geom["r_ctx"] <= tm
        body = functools.partial(_inproj_split_kernel, n_lat_tiles)
        src_specs = [pl.BlockSpec((tm, d), lambda i, j: (jnp.minimum(i, n_lat_tiles - 1), 0)),
                     pl.BlockSpec((geom["r_ctx"], d), lambda i, j: (0, 0))]
    return pl.pallas_call(
        body,
        out_shape=(jax.ShapeDtypeStruct((rows, P_WIDTH), BF16),
                   jax.ShapeDtypeStruct((rows, GATE_WIDTH), F32)),
        grid=(pl.cdiv(rows, tm), P_WIDTH // tn),
        in_specs=src_specs + [pl.BlockSpec((1, d), lambda i, j: (0, 0)), modspec(0), modspec(1),
                              pl.BlockSpec((d, tn), lambda i, j: (0, j)),
                              pl.BlockSpec((d, GATE_WIDTH), lambda i, j: (0, 0))],
        out_specs=(pl.BlockSpec((tm, tn), lambda i, j: (i, j)),
                   pl.BlockSpec((tm, GATE_WIDTH), lambda i, j: (i, 0))),
        scratch_shapes=[pltpu.VMEM((tm, d), BF16)],
        compiler_params=_cparams(("arbitrary", "arbitrary")),
        name="in_proj",
    )(*srcs, gain, mod3, mod3, w, wg)


def _gelu(x):
    return jax.nn.gelu(x, approximate=True)


def _gmlp_kernel(gu_ref, gv_ref, gain_ref, ws_ref, bs_ref, o_ref):
    tr = gu_ref.shape[0]
    for c in range(tr // CHUNK):
        rs = slice(c * CHUNK, (c + 1) * CHUNK)
        u = _gelu(gu_ref[rs, :].astype(F32))
        v = _gelu(gv_ref[rs, :].astype(F32))
        for g in range(GM_GROUPS):
            cs = slice(g * HEAD_DIM, (g + 1) * HEAD_DIM)
            vg = v[:, cs]
            vn = vg * lax.rsqrt(jnp.mean(vg * vg, axis=-1, keepdims=True) + EPS) * gain_ref[:, cs]
            s = jnp.dot(ws_ref[g], vn.astype(BF16), preferred_element_type=F32) + bs_ref[:, cs]
            o_ref[rs, cs] = (u[:, cs] * s).astype(BF16)


def _gmlp_call(p, v_gain, ws, bs_full, tr):
    rows = p.shape[0]
    gu_blk = _P_OFF["gu"] // GM_WIDTH
    gv_blk = _P_OFF["gv"] // GM_WIDTH
    return pl.pallas_call(
        _gmlp_kernel,
        out_shape=jax.ShapeDtypeStruct((rows, GM_WIDTH), BF16),
        grid=(rows // tr,),
        in_specs=[pl.BlockSpec((tr, GM_WIDTH), lambda i: (i, gu_blk)),
                  pl.BlockSpec((tr, GM_WIDTH), lambda i: (i, gv_blk)),
                  pl.BlockSpec((1, GM_WIDTH), lambda i: (0, 0)),
                  pl.BlockSpec((GM_GROUPS, CHUNK, CHUNK), lambda i: (0, 0, 0)),
                  pl.BlockSpec((CHUNK, GM_WIDTH), lambda i: (0, 0))],
        out_specs=pl.BlockSpec((tr, GM_WIDTH), lambda i: (i, 0)),
        compiler_params=_cparams(("arbitrary",)),
        name="gmlp",
    )(p, p, v_gain, ws, bs_full)


def _rope(x_bf, cos, sin, prot):
    rot = jnp.dot(x_bf, prot, preferred_element_type=F32)
    return (x_bf.astype(F32) * cos + rot * sin).astype(BF16)


def _attn_kernel(nt_seq, n_lat_tiles, sink_ref, q_ref, kp_ref, km_ref, kn_ref, vp_ref, vm_ref, vn_ref,
                 kx_ref, vx_ref, cp_ref, sp_ref, cm_ref, sm_ref, cn_ref, sn_ref, prot_ref, o_ref):
    t = pl.program_id(0)
    is_lat = t < n_lat_tiles
    pos = t % nt_seq
    has_cur = is_lat.astype(jnp.int32)
    has_prev = jnp.logical_and(is_lat, pos != 0).astype(jnp.int32)
    has_next = jnp.logical_and(is_lat, pos != nt_seq - 1).astype(jnp.int32)
    prot = prot_ref[...]
    nq = ATT_TILE // CHUNK
    q_scale = HEAD_DIM ** -0.5 * LOG2E

    qi = lax.broadcasted_iota(jnp.int32, (ATT_GROUP * CHUNK, 3 * CHUNK), 0) & (CHUNK - 1)
    kj = lax.broadcasted_iota(jnp.int32, (ATT_GROUP * CHUNK, 3 * CHUNK), 1)
    band = jnp.where(jnp.abs(kj - CHUNK - qi) <= CHUNK, 1, 0)
    hrow = lax.broadcasted_iota(jnp.int32, (ATT_GROUP * CHUNK, 1), 0)

    cos_k = jnp.concatenate([cp_ref[...], cm_ref[...], cn_ref[...]], axis=0)
    sin_k = jnp.concatenate([sp_ref[...], sm_ref[...], sn_ref[...]], axis=0)
    cos_q, sin_q = cm_ref[...] * q_scale, sm_ref[...] * q_scale
    for g in range(ATT_KV_HEADS):
        cs = slice(g * HEAD_DIM, (g + 1) * HEAD_DIM)
        k_rot = _rope(jnp.concatenate([kp_ref[:, cs], km_ref[:, cs], kn_ref[:, cs]], axis=0), cos_k, sin_k, prot)
        v_loc = jnp.concatenate([vp_ref[:, cs], vm_ref[:, cs], vn_ref[:, cs]], axis=0)
        sk = jnp.where(hrow < CHUNK, sink_ref[g * ATT_GROUP],
                       jnp.where(hrow < 2 * CHUNK, sink_ref[g * ATT_GROUP + 1], sink_ref[g * ATT_GROUP + 2])) * LOG2E
        for a in range(nq):
            flags = [has_prev if a + j == 0 else (has_next if a + j == nq + 1 else has_cur) for j in range(3)]
            valid = band * jnp.where(kj < CHUNK, flags[0], jnp.where(kj < 2 * CHUNK, flags[1], flags[2])) > 0
            rs = slice(a * CHUNK, (a + 1) * CHUNK)
            q3 = jnp.concatenate(
                [_rope(q_ref[rs, (g * ATT_GROUP + h) * HEAD_DIM:(g * ATT_GROUP + h + 1) * HEAD_DIM],
                       cos_q[rs], sin_q[rs], prot) for h in range(ATT_GROUP)], axis=0)
            k_all = jnp.concatenate([k_rot[a * CHUNK:(a + 3) * CHUNK], kx_ref[:, cs]], axis=0)
            v_all = jnp.concatenate([v_loc[a * CHUNK:(a + 3) * CHUNK], vx_ref[:, cs]], axis=0)
            s = lax.dot_general(q3, k_all, (((1,), (1,)), ((), ())), preferred_element_type=F32)
            s_loc = jnp.where(valid, s[:, :3 * CHUNK], NEG_BIG)
            s_ctx = s[:, 3 * CHUNK:]
            m = jnp.maximum(jnp.maximum(jnp.max(s_loc, axis=1, keepdims=True),
                                        jnp.max(s_ctx, axis=1, keepdims=True)), sk)
            p_loc = jnp.exp2(s_loc - m)
            p_ctx = jnp.exp2(s_ctx - m)
            den = (jnp.sum(p_loc, axis=1, keepdims=True) + jnp.sum(p_ctx, axis=1, keepdims=True)
                   + jnp.exp2(sk - m))
            p_all = jnp.concatenate([p_loc, p_ctx], axis=1).astype(BF16)
            o = jnp.dot(p_all, v_all, preferred_element_type=F32) / den
            for h in range(ATT_GROUP):
                hs = (g * ATT_GROUP + h) * HEAD_DIM
                o_ref[rs, hs:hs + HEAD_DIM] = o[h * CHUNK:(h + 1) * CHUNK, :].astype(BF16)


def _attn_call(p, sink, cos_tab, sin_tab, prot, geom):
    rows = p.shape[0]
    seq, lc, r_lat = geom["seq"], geom["lc"], geom["r_lat"]
    assert seq % ATT_TILE == 0 and lc % ATT_TILE == 0
    nt_seq, n_lat_tiles, nq = seq // ATT_TILE, r_lat // ATT_TILE, ATT_TILE // CHUNK
    ak_blk = _P_OFF["ak"] // ATT_KV_WIDTH
    av_blk = _P_OFF["av"] // ATT_KV_WIDTH
    ctx_base = r_lat // lc

    def is_lat(t):
        return t < n_lat_tiles

    def prev_ok(t):
        return jnp.logical_and(is_lat(t), t % nt_seq != 0)

    def next_ok(t):
        return jnp.logical_and(is_lat(t), t % nt_seq != nt_seq - 1)

    def prev_i(t):
        return jnp.where(prev_ok(t), nq * t - 1, nq * t)

    def next_i(t):
        return jnp.where(next_ok(t), nq * t + nq, nq * t + nq - 1)

    def tab_main(t):
        return jnp.where(is_lat(t), t % nt_seq, nt_seq)

    def tab_prev(t):
        return jnp.where(prev_ok(t), nq * (t % nt_seq) - 1, nq * nt_seq)

    def tab_next(t):
        return jnp.where(next_ok(t), nq * (t % nt_seq) + nq, nq * nt_seq)

    def ctx_i(t):
        return ctx_base + jnp.where(is_lat(t), t // nt_seq, (t - n_lat_tiles) // (lc // ATT_TILE))

    halo = lambda f, blk: pl.BlockSpec((CHUNK, ATT_KV_WIDTH), lambda t: (f(t), blk))
    main = lambda blk: pl.BlockSpec((ATT_TILE, ATT_KV_WIDTH), lambda t: (t, blk))
    tb = lambda rws, f: pl.BlockSpec((rws, HEAD_DIM), lambda t: (f(t), 0))
    return pl.pallas_call(
        functools.partial(_attn_kernel, nt_seq, n_lat_tiles),
        out_shape=jax.ShapeDtypeStruct((rows, ATT_WIDTH), BF16),
        grid=(rows // ATT_TILE,),
        in_specs=[pl.BlockSpec(memory_space=pltpu.SMEM),
                  pl.BlockSpec((ATT_TILE, ATT_WIDTH), lambda t: (t, 0)),
                  halo(prev_i, ak_blk), main(ak_blk), halo(next_i, ak_blk),
                  halo(prev_i, av_blk), main(av_blk), halo(next_i, av_blk),
                  pl.BlockSpec((lc, ATT_KV_WIDTH), lambda t: (ctx_i(t), ak_blk)),
                  pl.BlockSpec((lc, ATT_KV_WIDTH), lambda t: (ctx_i(t), av_blk)),
                  tb(CHUNK, tab_prev), tb(CHUNK, tab_prev), tb(ATT_TILE, tab_main), tb(ATT_TILE, tab_main),
                  tb(CHUNK, tab_next), tb(CHUNK, tab_next),
                  pl.BlockSpec((HEAD_DIM, HEAD_DIM), lambda t: (0, 0))],
        out_specs=pl.BlockSpec((ATT_TILE, ATT_WIDTH), lambda t: (t, 0)),
        compiler_params=_cparams(("arbitrary",)),
        name="window_attn",
    )(sink, p, p, p, p, p, p, p, p, p, cos_tab, sin_tab, cos_tab, sin_tab, cos_tab, sin_tab, prot)


def _conv_kernel(start_chunks, end_chunks, q_ref, qp_ref, qn_ref, k_ref, kp_ref, kn_ref, w_ref, b_ref, sh_ref,
                 oq_ref, okt_ref):
    nc = q_ref.shape[0] // CHUNK
    i = pl.program_id(0)
    for part, (x_ref, xp_ref, xn_ref) in enumerate(((q_ref, qp_ref, qn_ref), (k_ref, kp_ref, kn_ref))):
        cs = slice(part * ML_WIDTH, (part + 1) * ML_WIDTH)
        w0, w1, w2, bias = w_ref[0:1, cs], w_ref[1:2, cs], w_ref[2:3, cs], b_ref[0:1, cs]
        for c in range(nc):
            cg = i * nc + c
            at_start = _any([cg == s for s in start_chunks])
            at_end = _any([cg == e for e in end_chunks])
            cur = x_ref[c * CHUNK:(c + 1) * CHUNK, :]
            prv = xp_ref[...] if c == 0 else x_ref[(c - 1) * CHUNK:c * CHUNK, :]
            nxt = xn_ref[...] if c == nc - 1 else x_ref[(c + 1) * CHUNK:(c + 2) * CHUNK, :]
            s_prev = sh_ref[jnp.where(at_start, 1, 0)]
            s_next = sh_ref[jnp.where(at_end, 3, 2)]
            x_prev = jnp.dot(s_prev, jnp.concatenate([prv, cur], axis=0), preferred_element_type=F32)
            x_next = jnp.dot(s_next, jnp.concatenate([cur, nxt], axis=0), preferred_element_type=F32)
            y = bias + x_prev * w0 + cur.astype(F32) * w1 + x_next * w2
            y = y * jax.nn.sigmoid(y)
            if part == 0:
                oq_ref[c * CHUNK:(c + 1) * CHUNK, :] = y.astype(BF16)
            else:
                for h in range(ML_HEADS):
                    r0 = (c * ML_HEADS + h) * HEAD_DIM
                    okt_ref[r0:r0 + HEAD_DIM, :] = y[:, h * HEAD_DIM:(h + 1) * HEAD_DIM].T.astype(BF16)


def _shift_matrices():
    t = np.arange(CHUNK)
    prev = np.zeros((CHUNK, 2 * CHUNK), np.float32)
    prev[t, t + CHUNK - 1] = 1.0
    nxt = np.zeros((CHUNK, 2 * CHUNK), np.float32)
    nxt[t, t + 1] = 1.0
    prev_cut, nxt_cut = prev.copy(), nxt.copy()
    prev_cut[0] = 0.0
    nxt_cut[CHUNK - 1] = 0.0
    return jnp.asarray(np.stack([prev, prev_cut, nxt, nxt_cut]), BF16)


def _conv_call(p, w8, b8, geom):
    rows = p.shape[0]
    tr = CONV_TILE
    nc = tr // CHUNK
    mq_blk = _P_OFF["mq"] // ML_WIDTH
    mk_blk = _P_OFF["mk"] // ML_WIDTH
    start_chunks = tuple(s // CHUNK for s in geom["starts"])
    end_chunks = tuple(e // CHUNK for e in geom["ends"])

    def trio(blk):
        return [pl.BlockSpec((tr, ML_WIDTH), lambda i: (i, blk)),
                pl.BlockSpec((CHUNK, ML_WIDTH), lambda i: (jnp.maximum(i * nc - 1, 0), blk)),
                pl.BlockSpec((CHUNK, ML_WIDTH), lambda i: (jnp.minimum((i + 1) * nc, rows // CHUNK - 1), blk))]

    return pl.pallas_call(
        functools.partial(_conv_kernel, start_chunks, end_chunks),
        out_shape=(jax.ShapeDtypeStruct((rows, ML_WIDTH), BF16),
                   jax.ShapeDtypeStruct((rows * ML_HEADS, CHUNK), BF16)),
        grid=(rows // tr,),
        in_specs=trio(mq_blk) + trio(mk_blk) + [
            pl.BlockSpec((8, 2 * ML_WIDTH), lambda i: (0, 0)),
            pl.BlockSpec((8, 2 * ML_WIDTH), lambda i: (0, 0)),
            pl.BlockSpec((4, CHUNK, 2 * CHUNK), lambda i: (0, 0, 0))],
        out_specs=(pl.BlockSpec((tr, ML_WIDTH), lambda i: (i, 0)),
                   pl.BlockSpec((tr * ML_HEADS, CHUNK), lambda i: (i, 0))),
        compiler_params=_cparams(("arbitrary",)),
        name="short_conv",
    )(p, p, p, p, p, p, w8, b8, _shift_matrices())


def _split3(x):
    hi = x.astype(BF16)
    r1 = x - hi.astype(F32)
    mid = r1.astype(BF16)
    lo = (r1 - mid.astype(F32)).astype(BF16)
    return hi, mid, lo


def _cummax_lanes(x, reverse):
    n = x.shape[1]
    lane = lax.broadcasted_iota(jnp.int32, x.shape, 1)
    sh = 1
    while sh < n:
        if reverse:
            x = jnp.maximum(x, jnp.where(lane < n - sh, pltpu.roll(x, n - sh, 1), NEG_BIG))
        else:
            x = jnp.maximum(x, jnp.where(lane >= sh, pltpu.roll(x, sh, 1), NEG_BIG))
        sh *= 2
    return x


def _gate_prep_kernel(g_ref, bias_ref, o_ref):
    gl = GATE_LANES
    row = lax.broadcasted_iota(jnp.int32, (CHUNK, CHUNK), 0)
    col = lax.broadcasted_iota(jnp.int32, (CHUNK, CHUNK), 1)
    for c in range(g_ref.shape[0] // CHUNK):
        pre = g_ref[c * CHUNK:(c + 1) * CHUNK, :].T[0:2 * gl, :] + bias_ref[...]
        li, pf = pre[0:gl], pre[gl:2 * gl]
        lf = jnp.minimum(pf, 0.0) - jnp.log1p(jnp.exp(-jnp.abs(pf)))
        lf3 = jnp.concatenate(_split3(lf), axis=0)
        for d in range(2):
            csum = ((row <= col) if d == 0 else (row >= col)).astype(BF16)
            cs3 = jnp.dot(lf3, csum, preferred_element_type=F32)
            b = cs3[0:gl] + cs3[gl:2 * gl] + cs3[2 * gl:3 * gl]
            g_rep = jnp.broadcast_to(b[:, CHUNK - 1:CHUNK] if d == 0 else b[:, 0:1], b.shape)
            u = li - b
            a = g_rep - b + li
            mloc = jnp.broadcast_to(jnp.max(a, axis=1, keepdims=True), b.shape)
            r0 = (c * 2 + d) * GP_ROWS
            o_ref[r0:r0 + GP_ROWS, :] = jnp.concatenate(
                [b, u, _cummax_lanes(u, d == 1), jnp.exp(a - mloc), g_rep, mloc], axis=0)


def _gate_prep_call(gates, bias, tr):
    rows = gates.shape[0]
    nc = tr // CHUNK
    return pl.pallas_call(
        _gate_prep_kernel,
        out_shape=jax.ShapeDtypeStruct((rows // CHUNK * 2 * GP_ROWS, CHUNK), F32),
        grid=(rows // tr,),
        in_specs=[pl.BlockSpec((tr, GATE_WIDTH), lambda i: (i, 0)),
                  pl.BlockSpec((2 * GATE_LANES, CHUNK), lambda i: (0, 0))],
        out_specs=pl.BlockSpec((nc * 2 * GP_ROWS, CHUNK), lambda i: (i, 0)),
        compiler_params=_cparams(("arbitrary",)),
        name="gate_prep",
    )(gates, bias)


def _mlstm_kernel(qf_ref, kf_ref, vf_ref, gf_ref, qb_ref, kb_ref, vb_ref, gb_ref, hf_ref, hb_ref, ct_scr, m_scr):
    @pl.when(pl.program_id(1) == 0)
    def _():
        ct_scr[...] = jnp.zeros_like(ct_scr)
        m_scr[...] = jnp.zeros_like(m_scr)

    row = lax.broadcasted_iota(jnp.int32, (CHUNK, CHUNK), 0)
    col = lax.broadcasted_iota(jnp.int32, (CHUNK, CHUNK), 1)
    ones_bf = jnp.ones((CHUNK, HEAD_DIM), BF16)
    gl = GATE_LANES
    dirs = ((qf_ref, kf_ref, vf_ref, gf_ref, hf_ref), (qb_ref, kb_ref, vb_ref, gb_ref, hb_ref))
    prep = []
    for d, (_, _, _, g_ref, _) in enumerate(dirs):
        b, u, umax, w, g_rep, mloc = (g_ref[k * gl:(k + 1) * gl, :] for k in range(6))
        m0 = m_scr[d]
        mx = jnp.maximum(m0, umax)
        em = jnp.exp(LN_INV_SCALE - b - mx)
        m_new = jnp.maximum(g_rep + m0, mloc)
        a_old = jnp.exp(g_rep + m0 - m_new)
        a_new = jnp.exp(mloc - m_new)
        m_scr[d] = m_new
        cols = jnp.concatenate([-mx, em, jnp.zeros((CHUNK - 2 * gl, CHUNK), F32)], axis=0).T
        prep.append((u, w, m0, a_old, a_new, cols))
    for d, (q_ref, kt_ref, v_ref, _, o_ref) in enumerate(dirs):
        order = (col <= row) if d == 0 else (col >= row)
        u, w, m0, a_old, a_new, cols = prep[d]
        for h in range(ML_HEADS):
            j = d * ML_HEADS + h
            cs = slice(h * HEAD_DIM, (h + 1) * HEAD_DIM)
            qh, kt, vh = q_ref[:, cs], kt_ref[cs, :], v_ref[:, cs]
            c_b = jnp.broadcast_to(cols[:, j:j + 1], (CHUNK, CHUNK))
            em_b = jnp.broadcast_to(cols[:, gl + j:gl + j + 1], (CHUNK, CHUNK))
            d_in = jnp.where(order, c_b + u[j:j + 1, :], NEG_BIG)
            d_st = c_b + m0[j:j + 1, :]
            qk = jnp.dot(qh, kt, preferred_element_type=F32)
            s_ext = jnp.concatenate([qk * jnp.exp(d_in), qh.astype(F32) * jnp.exp(d_st)], axis=1).astype(BF16)
            v_aug = jnp.concatenate([vh, ones_bf], axis=1)
            ct = ct_scr[j]
            r = jnp.dot(s_ext, jnp.concatenate([v_aug, ct.astype(BF16)], axis=0), preferred_element_type=F32)
            hh = r[:, :HEAD_DIM] / jnp.maximum(jnp.abs(r[:, HEAD_DIM:]), em_b)
            o_ref[:, cs] = hh.astype(BF16)
            ktw = (kt.astype(F32) * w[j:j + 1, :]).astype(BF16)
            dct = jnp.dot(ktw, v_aug, preferred_element_type=F32)
            ao = jnp.concatenate([a_old[j:j + 1, :], a_old[j:j + 1, :]], axis=1)
            an = jnp.concatenate([a_new[j:j + 1, :], a_new[j:j + 1, :]], axis=1)
            ct_scr[j] = ao * ct + an * dct


def _mlstm_call(q, kt, p, gp, geom):
    rows = p.shape[0]
    nb, n_lat, lc, batch = geom["nb"], geom["n_lat"], geom["lc"], geom["batch"]
    ncs = lc // CHUNK
    mv_blk = _P_OFF["mv"] // ML_WIDTH

    def fwd(b, s):
        return jnp.where(s < ncs, n_lat + b * ncs + s, b * nb + (s - ncs))

    def bwd(b, s):
        return jnp.where(s < ncs, n_lat + b * ncs + (ncs - 1 - s), b * nb + (nb - 1 - (s - ncs)))

    def specs(f, d):
        return [pl.BlockSpec((CHUNK, ML_WIDTH), lambda b, s: (f(b, s), 0)),
                pl.BlockSpec((ML_WIDTH, CHUNK), lambda b, s: (f(b, s), 0)),
                pl.BlockSpec((CHUNK, ML_WIDTH), lambda b, s: (f(b, s), mv_blk)),
                pl.BlockSpec((GP_ROWS, CHUNK), lambda b, s: (2 * f(b, s) + d, 0))]

    out = jax.ShapeDtypeStruct((rows, ML_WIDTH), BF16)
    return pl.pallas_call(
        _mlstm_kernel,
        out_shape=(out, out),
        grid=(batch, ncs + nb),
        in_specs=specs(fwd, 0) + specs(bwd, 1),
        out_specs=(pl.BlockSpec((CHUNK, ML_WIDTH), lambda b, s: (fwd(b, s), 0)),
                   pl.BlockSpec((CHUNK, ML_WIDTH), lambda b, s: (bwd(b, s), 0))),
        scratch_shapes=[pltpu.VMEM((2 * ML_HEADS, HEAD_DIM, 2 * HEAD_DIM), F32),
                        pltpu.VMEM((2, GATE_LANES, CHUNK), F32)],
        compiler_params=_cparams(("arbitrary", "arbitrary")),
        name="mlstm",
    )(q, kt, p, gp, q, kt, p, gp)


def _mix_update(ya_ref, yb_ref, hf_ref, hb_ref, mo_ref, og_ref, gt_ref, w_ref):
    hs = hf_ref[...].astype(F32) + hb_ref[...].astype(F32)
    parts = [ya_ref[...], yb_ref[...]]
    for h in range(ML_HEADS):
        cs = slice(h * HEAD_DIM, (h + 1) * HEAD_DIM)
        xh = hs[:, cs]
        y = xh * lax.rsqrt(jnp.mean(xh * xh, axis=-1, keepdims=True) + EPS) * og_ref[:, cs]
        parts.append((y * jax.nn.sigmoid(mo_ref[:, cs].astype(F32))).astype(BF16))
    lhs = jnp.concatenate(parts, axis=1)
    return gt_ref[0] * jnp.dot(lhs, w_ref[...], preferred_element_type=F32)


def _mixout_kernel(x_ref, ya_ref, yb_ref, hf_ref, hb_ref, mo_ref, og_ref, gt_ref, w_ref, o_ref):
    o_ref[...] = x_ref[...] + _mix_update(ya_ref, yb_ref, hf_ref, hb_ref, mo_ref, og_ref, gt_ref, w_ref)


def _mixout_split_kernel(n_lat_tiles, x_ref, c_ref, ya_ref, yb_ref, hf_ref, hb_ref, mo_ref, og_ref, gt_ref, w_ref,
                         o_ref):
    upd = _mix_update(ya_ref, yb_ref, hf_ref, hb_ref, mo_ref, og_ref, gt_ref, w_ref)
    i = pl.program_id(0)

    @pl.when(i < n_lat_tiles)
    def _():
        o_ref[...] = x_ref[...] + upd

    @pl.when(i >= n_lat_tiles)
    def _():
        o_ref[...] = c_ref[...] + upd


def _mixout_call(srcs, ya, yb, hf, hb, p, out_gain, mod3, w_out, geom, layer_mod_base, n_tiles):
    d = srcs[0].shape[1]
    rows, tm, seg = geom["rows"], geom["tm_mix"], geom["seg_mix"]
    mo_blk = _P_OFF["mo"] // ML_WIDTH
    row = lambda w, blk=0: pl.BlockSpec((tm, w), lambda i: (i, blk))
    if len(srcs) == 1:
        body, src_specs, alias = _mixout_kernel, [row(d)], {0: 0}
    else:
        n_lat_tiles = geom["r_lat"] // tm
        body = functools.partial(_mixout_split_kernel, n_lat_tiles)
        src_specs = [pl.BlockSpec((tm, d), lambda i: (jnp.minimum(i, n_lat_tiles - 1), 0)),
                     pl.BlockSpec((tm, d), lambda i: (jnp.maximum(i - n_lat_tiles, 0), 0))]
        alias = {}
    return pl.pallas_call(
        body,
        out_shape=jax.ShapeDtypeStruct((rows, d), F32),
        grid=(n_tiles,),
        in_specs=src_specs + [row(GM_WIDTH), row(ATT_WIDTH), row(ML_WIDTH), row(ML_WIDTH), row(ML_WIDTH, mo_blk),
                              pl.BlockSpec((1, ML_WIDTH), lambda i: (0, 0)),
                              pl.BlockSpec((1, 1, d), lambda i: (layer_mod_base + seg(i) * 6 + 2, 0, 0)),
                              pl.BlockSpec((MIX_WIDTH, d), lambda i: (0, 0))],
        out_specs=row(d),
        input_output_aliases=alias,
        compiler_params=_cparams(("arbitrary",)),
        name="mix_out",
    )(*srcs, ya, yb, hf, hb, p, out_gain, mod3, w_out)


def _ffn_kernel(final, x_ref, g_ref, sh_ref, sc_ref, gt_ref, w1_ref, w2_ref, gf_ref, o_ref, h_scr):
    j = pl.program_id(1)

    @pl.when(j == 0)
    def _():
        h_scr[...] = _norm_mod(x_ref[...], g_ref[...], sc_ref[0], sh_ref[0])
        o_ref[...] = jnp.zeros_like(o_ref)

    a = jnp.maximum(jnp.dot(h_scr[...], w1_ref[...], preferred_element_type=F32), 0.0)
    o_ref[...] += jnp.dot((a * a).astype(BF16), w2_ref[...], preferred_element_type=F32)

    @pl.when(j == pl.num_programs(1) - 1)
    def _():
        xo = x_ref[...] + gt_ref[0] * o_ref[...]
        if final:
            xo = xo * lax.rsqrt(jnp.mean(xo * xo, axis=-1, keepdims=True) + EPS) * gf_ref[...]
        o_ref[...] = xo


def _ffn_call(xs, gain, mod3, w1, w2, g_final, geom, layer_mod_base, n_tiles, final):
    rows, d = xs.shape
    hidden = w1.shape[1]
    tm, seg = geom["tm_mix"], geom["seg_mix"]
    th = min(1024, hidden)
    out_rows = n_tiles * tm if final else rows
    modspec = lambda k: pl.BlockSpec((1, 1, d), lambda i, j: (layer_mod_base + seg(i) * 6 + k, 0, 0))
    return pl.pallas_call(
        functools.partial(_ffn_kernel, final),
        out_shape=jax.ShapeDtypeStruct((out_rows, d), F32),
        grid=(n_tiles, hidden // th),
        in_specs=[pl.BlockSpec((tm, d), lambda i, j: (i, 0)),
                  pl.BlockSpec((1, d), lambda i, j: (0, 0)),
                  modspec(3), modspec(4), modspec(5),
                  pl.BlockSpec((d, th), lambda i, j: (0, j)),
                  pl.BlockSpec((th, d), lambda i, j: (j, 0)),
                  pl.BlockSpec((1, d), lambda i, j: (0, 0))],
        out_specs=pl.BlockSpec((tm, d), lambda i, j: (i, 0)),
        scratch_shapes=[pltpu.VMEM((tm, d), BF16)],
        compiler_params=_cparams(("arbitrary", "arbitrary")),
        name="ffn_final" if final else "ffn",
    )(xs, gain, mod3, mod3, mod3, w1, w2, g_final)


def _rope_tables(seq):
    n_rows = seq // GRID_W
    half = HEAD_DIM // 2
    freqs = (1.0 / (ROPE_BASE ** (np.arange(0, half, 2, dtype=np.float32) / np.float32(half)))).astype(np.float32)
    ang_r = (np.arange(n_rows, dtype=np.float32)[:, None] * freqs).astype(np.float32).astype(np.float64)
    ang_c = (np.arange(GRID_W, dtype=np.float32)[:, None] * freqs).astype(np.float32).astype(np.float64)

    def full(fn, ident):
        r = jnp.repeat(jnp.asarray(fn(ang_r), F32), GRID_W, axis=0)
        c = jnp.tile(jnp.asarray(fn(ang_c), F32), (n_rows, 1))
        tab = jnp.concatenate([r, r, c, c], axis=-1)
        return jnp.concatenate([tab, jnp.full((ATT_TILE, HEAD_DIM), ident, F32)], axis=0)

    i = np.arange(HEAD_DIM)
    quarter = HEAD_DIM // 4
    prot = np.zeros((HEAD_DIM, HEAD_DIM), np.float32)
    first = (i % (2 * quarter)) < quarter
    prot[(i + quarter)[first], i[first]] = -1.0
    prot[(i - quarter)[~first], i[~first]] = 1.0
    return full(np.cos, 1.0), full(np.sin, 0.0), jnp.asarray(prot, BF16)


def _geometry(batch, seq, lc, d):
    tm = 1024 if seq % 1024 == 0 else 512
    tm_mix = 512
    assert seq % tm == 0 and seq % CHUNK == 0 and lc % CHUNK == 0 and (batch * seq) % lc == 0
    r_lat, r_ctx = batch * seq, batch * lc
    rows = r_lat + r_ctx
    assert rows % tm_mix == 0 and r_ctx % tm_mix == 0 and rows % CONV_TILE == 0
    starts = tuple(b * seq for b in range(batch)) + tuple(r_lat + b * lc for b in range(batch))
    ends = tuple((b + 1) * seq - 1 for b in range(batch)) + tuple(r_lat + (b + 1) * lc - 1 for b in range(batch))
    return dict(tm=tm, tm_mix=tm_mix, rows=rows, r_lat=r_lat, r_ctx=r_ctx, batch=batch, seq=seq, lc=lc,
                nb=seq // CHUNK, n_lat=r_lat // CHUNK,
                seg=lambda i: jnp.minimum(i // (seq // tm), batch),
                seg_mix=lambda i: jnp.minimum(i // (seq // tm_mix), batch),
                starts=starts, ends=ends)


def _permute_w_in(w):
    cols = [w[:, _SRC[n]:_SRC[n] + wd] for n, wd in _P_ORDER]
    return jnp.concatenate(cols, axis=1).astype(BF16)


def _gate_w(w):
    d = w.shape[0]
    mg = w[:, _SRC["mg"]:_SRC["mg"] + 4 * ML_HEADS].reshape(d, 2, 2, ML_HEADS)
    groups = []
    for t in range(2):
        part = mg[:, :, t, :].reshape(d, 2 * ML_HEADS)
        groups.append(jnp.pad(part, ((0, 0), (0, GATE_LANES - 2 * ML_HEADS))))
    groups.append(jnp.zeros((d, GATE_WIDTH - 2 * GATE_LANES), w.dtype))
    return jnp.concatenate(groups, axis=1).astype(BF16)


def _gate_bias(b_i, b_f):
    pad = jnp.zeros((GATE_LANES - 2 * ML_HEADS,), F32)
    col = jnp.concatenate([b_i.reshape(-1), pad, b_f.reshape(-1), pad])
    return jnp.broadcast_to(col[:, None], (2 * GATE_LANES, CHUNK))


def kernel(x, c, ctx, c_ctx, w_ada, b_ada, g_mix, g_ffn, w_in, gm_v_gain, gm_w_s, gm_b_s, attn_sink,
           ml_conv_w, ml_conv_b, ml_b_i, ml_b_f, ml_out_gain, w_out, w_ff1, w_ff2, g_final):
    batch, seq, d = x.shape
    lc = ctx.shape[1]
    depth = w_ada.shape[0]
    geom = _geometry(batch, seq, lc, d)
    rows = geom["rows"]

    cc = jnp.concatenate([c, c_ctx[None, :], jnp.zeros((8 - batch - 1, d), F32)], axis=0)
    mod = _ada_call(cc, w_ada, b_ada)
    mod3 = mod[:, :batch + 1, :].reshape(depth * (batch + 1) * 6, 1, d)

    cos_tab, sin_tab, prot = _rope_tables(seq)

    srcs = (x.reshape(-1, d), ctx.reshape(-1, d))
    for layer in range(depth):
        last = layer == depth - 1
        base = layer * (batch + 1) * 6
        w_p = _permute_w_in(w_in[layer])
        w_g = _gate_w(w_in[layer])
        p, gates = _inproj_call(srcs, g_mix[layer][None, :], mod3, w_p, w_g, geom, base)

        bs_full = jnp.repeat(gm_b_s[layer].T, HEAD_DIM, axis=1)
        ya = _gmlp_call(p, gm_v_gain[layer][None, :], gm_w_s[layer].astype(BF16), bs_full, 512)

        yb = _attn_call(p, attn_sink[layer], cos_tab, sin_tab, prot, geom)

        w8 = jnp.concatenate([ml_conv_w[layer], jnp.zeros((5, 2 * ML_WIDTH), F32)], axis=0)
        b8 = jnp.concatenate([ml_conv_b[layer][None, :], jnp.zeros((7, 2 * ML_WIDTH), F32)], axis=0)
        q_conv, kt_conv = _conv_call(p, w8, b8, geom)
        gp = _gate_prep_call(gates, _gate_bias(ml_b_i[layer], ml_b_f[layer]), CONV_TILE)
        hf, hb = _mlstm_call(q_conv, kt_conv, p, gp, geom)

        n_mix = (geom["r_lat"] if last else rows) // geom["tm_mix"]
        xs = _mixout_call(srcs, ya, yb, hf, hb, p, ml_out_gain[layer][None, :], mod3, w_out[layer].astype(BF16),
                          geom, base, n_mix)
        xs = _ffn_call(xs, g_ffn[layer][None, :], mod3, w_ff1[layer].astype(BF16), w_ff2[layer].astype(BF16),
                       g_final[None, :], geom, base, n_mix, last)
        srcs = (xs,)
    return xs.reshape(batch, seq, d)
```

```python
import functools
import math

import jax
import jax.numpy as jnp
import numpy as np
from jax import lax
from jax.experimental import pallas as pl
from jax.experimental.pallas import tpu as pltpu

F32 = jnp.float32
BF16 = jnp.bfloat16

HEAD_DIM = 128
CHUNK = 128
GM_GROUPS = 4
ATT_HEADS = 6
ATT_KV_HEADS = 2
ATT_GROUP = ATT_HEADS // ATT_KV_HEADS
ML_HEADS = 6
GM_WIDTH = GM_GROUPS * HEAD_DIM
ATT_WIDTH = ATT_HEADS * HEAD_DIM
ATT_KV_WIDTH = ATT_KV_HEADS * HEAD_DIM
ML_WIDTH = ML_HEADS * HEAD_DIM
MIX_WIDTH = GM_WIDTH + ATT_WIDTH + ML_WIDTH
GRID_W = 64
ROPE_BASE = 10000.0
EPS = 1e-6
NEG_BIG = -1e30
LOG2E = math.log2(math.e)
LN_INV_SCALE = 0.5 * math.log(HEAD_DIM)

ATT_TILE = 2 * CHUNK
CONV_TILE = 4 * CHUNK
GATE_LANES = 16
GP_ROWS = 6 * GATE_LANES

_SRC = dict(gu=0, gv=512, aq=1024, ak=1792, av=2048, mq=2304, mk=3072, mv=3840, mo=4608, mg=5376)
_P_ORDER = (("aq", 768), ("mq", 768), ("gu", 512), ("gv", 512), ("ak", 256), ("av", 256),
            ("mk", 768), ("mv", 768), ("mo", 768))
P_WIDTH = sum(w for _, w in _P_ORDER)
_P_OFF = {}
_o = 0
for _n, _w in _P_ORDER:
    _P_OFF[_n] = _o
    _o += _w
GATE_WIDTH = HEAD_DIM

VMEM_LIMIT = 56 * 1024 * 1024


def _cparams(sem):
    return pltpu.CompilerParams(dimension_semantics=sem, vmem_limit_bytes=VMEM_LIMIT)


def _any(preds):
    return functools.reduce(jnp.logical_or, preds)


def _ada_kernel(c_ref, w_ref, b_ref, o_ref):
    c = c_ref[...]
    s = (c * jax.nn.sigmoid(c)).astype(BF16)
    o_ref[...] = jnp.dot(s, w_ref[...].astype(BF16), preferred_element_type=F32) + b_ref[...]


def _ada_call(cc, w_ada, b_ada):
    depth, d, n = w_ada.shape
    tn = math.gcd(n, 1024)
    return pl.pallas_call(
        _ada_kernel,
        out_shape=jax.ShapeDtypeStruct((depth, cc.shape[0], n), F32),
        grid=(depth, n // tn),
        in_specs=[pl.BlockSpec((cc.shape[0], d), lambda l, j: (0, 0)),
                  pl.BlockSpec((None, d, tn), lambda l, j: (l, 0, j)),
                  pl.BlockSpec((None, 1, tn), lambda l, j: (l, 0, j))],
        out_specs=pl.BlockSpec((None, cc.shape[0], tn), lambda l, j: (l, 0, j)),
        compiler_params=_cparams(("arbitrary", "arbitrary")),
        name="ada_mod",
    )(cc, w_ada, b_ada.reshape(depth, 1, n))


def _norm_mod(x, gain, scale, shift):
    y = x * lax.rsqrt(jnp.mean(x * x, axis=-1, keepdims=True) + EPS) * gain
    return (y * (1.0 + scale) + shift).astype(BF16)


def _inproj_fill(src_ref, g_ref, sh_ref, sc_ref, wg_ref, gate_ref, h_scr):
    n = src_ref.shape[0]
    hb = _norm_mod(src_ref[...], g_ref[...], sc_ref[0], sh_ref[0])
    h_scr[0:n, :] = hb
    gate_ref[0:n, :] = jnp.dot(hb, wg_ref[...], preferred_element_type=F32)


def _inproj_kernel(x_ref, g_ref, sh_ref, sc_ref, w_ref, wg_ref, p_ref, gate_ref, h_scr):
    @pl.when(pl.program_id(1) == 0)
    def _():
        _inproj_fill(x_ref, g_ref, sh_ref, sc_ref, wg_ref, gate_ref, h_scr)

    p_ref[...] = jnp.dot(h_scr[...], w_ref[...], preferred_element_type=F32).astype(BF16)


def _inproj_split_kernel(n_lat_tiles, x_ref, c_ref, g_ref, sh_ref, sc_ref, w_ref, wg_ref, p_ref, gate_ref, h_scr):
    i, j = pl.program_id(0), pl.program_id(1)

    @pl.when(jnp.logical_and(j == 0, i < n_lat_tiles))
    def _():
        _inproj_fill(x_ref, g_ref, sh_ref, sc_ref, wg_ref, gate_ref, h_scr)

    @pl.when(jnp.logical_and(j == 0, i >= n_lat_tiles))
    def _():
        _inproj_fill(c_ref, g_ref, sh_ref, sc_ref, wg_ref, gate_ref, h_scr)

    p_ref[...] = jnp.dot(h_scr[...], w_ref[...], preferred_element_type=F32).astype(BF16)


def _inproj_call(srcs, gain, mod3, w, wg, geom, layer_mod_base):
    d = srcs[0].shape[1]
    rows, tm, seg = geom["rows"], geom["tm"], geom["seg"]
    tn = 768
    modspec = lambda k: pl.BlockSpec((1, 1, d), lambda i, j: (layer_mod_base + seg(i) * 6 + k, 0, 0))
    if len(srcs) == 1:
        body = _inproj_kernel
        src_specs = [pl.BlockSpec((tm, d), lambda i, j: (i, 0))]
    else:
        n_lat_tiles = geom["r_lat"] // tm
        assert geom["r_ctx"] <= tm
        body = functools.partial(_inproj_split_kernel, n_lat_tiles)
        src_specs = [pl.BlockSpec((tm, d), lambda i, j: (jnp.minimum(i, n_lat_tiles - 1), 0)),
                     pl.BlockSpec((geom["r_ctx"], d), lambda i, j: (0, 0))]
    return pl.pallas_call(
        body,
        out_shape=(jax.ShapeDtypeStruct((rows, P_WIDTH), BF16),
                   jax.ShapeDtypeStruct((rows, GATE_WIDTH), F32)),
        grid=(pl.cdiv(rows, tm), P_WIDTH // tn),
        in_specs=src_specs + [pl.BlockSpec((1, d), lambda i, j: (0, 0)), modspec(0), modspec(1),
                              pl.BlockSpec((d, tn), lambda i, j: (0, j)),
                              pl.BlockSpec((d, GATE_WIDTH), lambda i, j: (0, 0))],
        out_specs=(pl.BlockSpec((tm, tn), lambda i, j: (i, j)),
                   pl.BlockSpec((tm, GATE_WIDTH), lambda i, j: (i, 0))),
        scratch_shapes=[pltpu.VMEM((tm, d), BF16)],
        compiler_params=_cparams(("arbitrary", "arbitrary")),
        name="in_proj",
    )(*srcs, gain, mod3, mod3, w, wg)


def _gelu(x):
    return jax.nn.gelu(x, approximate=True)


def _gmlp_kernel(gu_ref, gv_ref, gain_ref, ws_ref, bs_ref, o_ref):
    units = [(c, g) for c in range(gu_ref.shape[0] // CHUNK) for g in range(GM_GROUPS)]
    rows = lambda c: slice(c * CHUNK, (c + 1) * CHUNK)
    cols = lambda g: slice(g * HEAD_DIM, (g + 1) * HEAD_DIM)
    vns = []
    for c, g in units:
        vg = _gelu(gv_ref[rows(c), cols(g)].astype(F32))
        vn = vg * lax.rsqrt(jnp.mean(vg * vg, axis=-1, keepdims=True) + EPS) * gain_ref[:, cols(g)]
        vns.append(vn.astype(BF16))
    ss = [jnp.dot(ws_ref[g], vn, preferred_element_type=F32) for (c, g), vn in zip(units, vns)]
    for (c, g), s in zip(units, ss):
        u = _gelu(gu_ref[rows(c), cols(g)].astype(F32))
        o_ref[rows(c), cols(g)] = (u * (s + bs_ref[:, cols(g)])).astype(BF16)


def _gmlp_call(p, v_gain, ws, bs_full, tr):
    rows = p.shape[0]
    gu_blk = _P_OFF["gu"] // GM_WIDTH
    gv_blk = _P_OFF["gv"] // GM_WIDTH
    return pl.pallas_call(
        _gmlp_kernel,
        out_shape=jax.ShapeDtypeStruct((rows, GM_WIDTH), BF16),
        grid=(rows // tr,),
        in_specs=[pl.BlockSpec((tr, GM_WIDTH), lambda i: (i, gu_blk)),
                  pl.BlockSpec((tr, GM_WIDTH), lambda i: (i, gv_blk)),
                  pl.BlockSpec((1, GM_WIDTH), lambda i: (0, 0)),
                  pl.BlockSpec((GM_GROUPS, CHUNK, CHUNK), lambda i: (0, 0, 0)),
                  pl.BlockSpec((CHUNK, GM_WIDTH), lambda i: (0, 0))],
        out_specs=pl.BlockSpec((tr, GM_WIDTH), lambda i: (i, 0)),
        compiler_params=_cparams(("arbitrary",)),
        name="gmlp",
    )(p, p, v_gain, ws, bs_full)


def _rope(x_bf, cos, sin, prot):
    rot = jnp.dot(x_bf, prot, preferred_element_type=F32)
    return (x_bf.astype(F32) * cos + rot * sin).astype(BF16)


def _attn_kernel(nt_seq, n_lat_tiles, sink_ref, q_ref, kp_ref, km_ref, kn_ref, vp_ref, vm_ref, vn_ref,
                 kx_ref, vx_ref, cp_ref, sp_ref, cm_ref, sm_ref, cn_ref, sn_ref, prot_ref, o_ref):
    t = pl.program_id(0)
    is_lat = t < n_lat_tiles
    pos = t % nt_seq
    has_cur = is_lat.astype(jnp.int32)
    has_prev = jnp.logical_and(is_lat, pos != 0).astype(jnp.int32)
    has_next = jnp.logical_and(is_lat, pos != nt_seq - 1).astype(jnp.int32)
    prot = prot_ref[...]
    nq = ATT_TILE // CHUNK
    q_scale = HEAD_DIM ** -0.5 * LOG2E

    qi = lax.broadcasted_iota(jnp.int32, (ATT_GROUP * CHUNK, 3 * CHUNK), 0) & (CHUNK - 1)
    kj = lax.broadcasted_iota(jnp.int32, (ATT_GROUP * CHUNK, 3 * CHUNK), 1)
    band = jnp.where(jnp.abs(kj - CHUNK - qi) <= CHUNK, 1, 0)
    hrow = lax.broadcasted_iota(jnp.int32, (ATT_GROUP * CHUNK, 1), 0)

    cos_k = jnp.concatenate([cp_ref[...], cm_ref[...], cn_ref[...]], axis=0)
    sin_k = jnp.concatenate([sp_ref[...], sm_ref[...], sn_ref[...]], axis=0)
    cos_q, sin_q = cm_ref[...] * q_scale, sm_ref[...] * q_scale
    head = lambda h: slice(h * HEAD_DIM, (h + 1) * HEAD_DIM)

    masks = []
    for a in range(nq):
        flags = [has_prev if a + j == 0 else (has_next if a + j == nq + 1 else has_cur) for j in range(3)]
        valid = band * jnp.where(kj < CHUNK, flags[0], jnp.where(kj < 2 * CHUNK, flags[1], flags[2])) > 0
        masks.append(jnp.where(valid, 0.0, NEG_BIG))

    k_rot = [_rope(jnp.concatenate([kp_ref[:, head(g)], km_ref[:, head(g)], kn_ref[:, head(g)]], axis=0),
                   cos_k, sin_k, prot) for g in range(ATT_KV_HEADS)]
    v_loc = [jnp.concatenate([vp_ref[:, head(g)], vm_ref[:, head(g)], vn_ref[:, head(g)]], axis=0)
             for g in range(ATT_KV_HEADS)]
    sks = [jnp.where(hrow < CHUNK, sink_ref[g * ATT_GROUP],
                     jnp.where(hrow < 2 * CHUNK, sink_ref[g * ATT_GROUP + 1], sink_ref[g * ATT_GROUP + 2])) * LOG2E
           for g in range(ATT_KV_HEADS)]
    units = [(g, a) for g in range(ATT_KV_HEADS) for a in range(nq)]
    rows = lambda a: slice(a * CHUNK, (a + 1) * CHUNK)
    q3s = [jnp.concatenate([_rope(q_ref[rows(a), head(g * ATT_GROUP + h)], cos_q[rows(a)], sin_q[rows(a)], prot)
                            for h in range(ATT_GROUP)], axis=0) for g, a in units]
    ss = [lax.dot_general(q3, jnp.concatenate([k_rot[g][a * CHUNK:(a + 3) * CHUNK], kx_ref[:, head(g)]], axis=0),
                          (((1,), (1,)), ((), ())), preferred_element_type=F32)
          for (g, a), q3 in zip(units, q3s)]
    ps, dens = [], []
    for (g, a), s in zip(units, ss):
        s_loc = s[:, :3 * CHUNK] + masks[a]
        s_ctx = s[:, 3 * CHUNK:]
        m = jnp.maximum(jnp.maximum(jnp.max(s_loc, axis=1, keepdims=True),
                                    jnp.max(s_ctx, axis=1, keepdims=True)), sks[g])
        p_loc = jnp.exp2(s_loc - m)
        p_ctx = jnp.exp2(s_ctx - m)
        dens.append(jnp.sum(p_loc, axis=1, keepdims=True) + jnp.sum(p_ctx, axis=1, keepdims=True)
                    + jnp.exp2(sks[g] - m))
        ps.append(jnp.concatenate([p_loc, p_ctx], axis=1).astype(BF16))
    for (g, a), p_all, den in zip(units, ps, dens):
        v_all = jnp.concatenate([v_loc[g][a * CHUNK:(a + 3) * CHUNK], vx_ref[:, head(g)]], axis=0)
        o = jnp.dot(p_all, v_all, preferred_element_type=F32) / den
        for h in range(ATT_GROUP):
            o_ref[rows(a), head(g * ATT_GROUP + h)] = o[h * CHUNK:(h + 1) * CHUNK, :].astype(BF16)


def _attn_call(p, sink, cos_tab, sin_tab, prot, geom):
    rows = p.shape[0]
    seq, lc, r_lat = geom["seq"], geom["lc"], geom["r_lat"]
    assert seq % ATT_TILE == 0 and lc % ATT_TILE == 0
    nt_seq, n_lat_tiles, nq = seq // ATT_TILE, r_lat // ATT_TILE, ATT_TILE // CHUNK
    ak_blk = _P_OFF["ak"] // ATT_KV_WIDTH
    av_blk = _P_OFF["av"] // ATT_KV_WIDTH
    ctx_base = r_lat // lc

    def is_lat(t):
        return t < n_lat_tiles

    def prev_ok(t):
        return jnp.logical_and(is_lat(t), t % nt_seq != 0)

    def next_ok(t):
        return jnp.logical_and(is_lat(t), t % nt_seq != nt_seq - 1)

    def prev_i(t):
        return jnp.where(prev_ok(t), nq * t - 1, nq * t)

    def next_i(t):
        return jnp.where(next_ok(t), nq * t + nq, nq * t + nq - 1)

    def tab_main(t):
        return jnp.where(is_lat(t), t % nt_seq, nt_seq)

    def tab_prev(t):
        return jnp.where(prev_ok(t), nq * (t % nt_seq) - 1, nq * nt_seq)

    def tab_next(t):
        return jnp.where(next_ok(t), nq * (t % nt_seq) + nq, nq * nt_seq)

    def ctx_i(t):
        return ctx_base + jnp.where(is_lat(t), t // nt_seq, (t - n_lat_tiles) // (lc // ATT_TILE))

    halo = lambda f, blk: pl.BlockSpec((CHUNK, ATT_KV_WIDTH), lambda t: (f(t), blk))
    main = lambda blk: pl.BlockSpec((ATT_TILE, ATT_KV_WIDTH), lambda t: (t, blk))
    tb = lambda rws, f: pl.BlockSpec((rws, HEAD_DIM), lambda t: (f(t), 0))
    return pl.pallas_call(
        functools.partial(_attn_kernel, nt_seq, n_lat_tiles),
        out_shape=jax.ShapeDtypeStruct((rows, ATT_WIDTH), BF16),
        grid=(rows // ATT_TILE,),
        in_specs=[pl.BlockSpec(memory_space=pltpu.SMEM),
                  pl.BlockSpec((ATT_TILE, ATT_WIDTH), lambda t: (t, 0)),
                  halo(prev_i, ak_blk), main(ak_blk), halo(next_i, ak_blk),
                  halo(prev_i, av_blk), main(av_blk), halo(next_i, av_blk),
                  pl.BlockSpec((lc, ATT_KV_WIDTH), lambda t: (ctx_i(t), ak_blk)),
                  pl.BlockSpec((lc, ATT_KV_WIDTH), lambda t: (ctx_i(t), av_blk)),
                  tb(CHUNK, tab_prev), tb(CHUNK, tab_prev), tb(ATT_TILE, tab_main), tb(ATT_TILE, tab_main),
                  tb(CHUNK, tab_next), tb(CHUNK, tab_next),
                  pl.BlockSpec((HEAD_DIM, HEAD_DIM), lambda t: (0, 0))],
        out_specs=pl.BlockSpec((ATT_TILE, ATT_WIDTH), lambda t: (t, 0)),
        compiler_params=_cparams(("arbitrary",)),
        name="window_attn",
    )(sink, p, p, p, p, p, p, p, p, p, cos_tab, sin_tab, cos_tab, sin_tab, cos_tab, sin_tab, prot)


def _conv_kernel(start_chunks, end_chunks, q_ref, qp_ref, qn_ref, k_ref, kp_ref, kn_ref, w_ref, b_ref, sh_ref,
                 oq_ref, okt_ref):
    nc = q_ref.shape[0] // CHUNK
    i = pl.program_id(0)
    for part, (x_ref, xp_ref, xn_ref) in enumerate(((q_ref, qp_ref, qn_ref), (k_ref, kp_ref, kn_ref))):
        cs = slice(part * ML_WIDTH, (part + 1) * ML_WIDTH)
        w0, w1, w2, bias = w_ref[0:1, cs], w_ref[1:2, cs], w_ref[2:3, cs], b_ref[0:1, cs]
        for c in range(nc):
            cg = i * nc + c
            at_start = _any([cg == s for s in start_chunks])
            at_end = _any([cg == e for e in end_chunks])
            cur = x_ref[c * CHUNK:(c + 1) * CHUNK, :]
            prv = xp_ref[...] if c == 0 else x_ref[(c - 1) * CHUNK:c * CHUNK, :]
            nxt = xn_ref[...] if c == nc - 1 else x_ref[(c + 1) * CHUNK:(c + 2) * CHUNK, :]
            s_prev = sh_ref[jnp.where(at_start, 1, 0)]
            s_next = sh_ref[jnp.where(at_end, 3, 2)]
            x_prev = jnp.dot(s_prev, jnp.concatenate([prv, cur], axis=0), preferred_element_type=F32)
            x_next = jnp.dot(s_next, jnp.concatenate([cur, nxt], axis=0), preferred_element_type=F32)
            y = bias + x_prev * w0 + cur.astype(F32) * w1 + x_next * w2
            y = y * jax.nn.sigmoid(y)
            if part == 0:
                oq_ref[c * CHUNK:(c + 1) * CHUNK, :] = y.astype(BF16)
            else:
                for h in range(ML_HEADS):
                    r0 = (c * ML_HEADS + h) * HEAD_DIM
                    okt_ref[r0:r0 + HEAD_DIM, :] = y[:, h * HEAD_DIM:(h + 1) * HEAD_DIM].T.astype(BF16)


def _shift_matrices():
    t = np.arange(CHUNK)
    prev = np.zeros((CHUNK, 2 * CHUNK), np.float32)
    prev[t, t + CHUNK - 1] = 1.0
    nxt = np.zeros((CHUNK, 2 * CHUNK), np.float32)
    nxt[t, t + 1] = 1.0
    prev_cut, nxt_cut = prev.copy(), nxt.copy()
    prev_cut[0] = 0.0
    nxt_cut[CHUNK - 1] = 0.0
    return jnp.asarray(np.stack([prev, prev_cut, nxt, nxt_cut]), BF16)


def _conv_call(p, w8, b8, geom):
    rows = p.shape[0]
    tr = CONV_TILE
    nc = tr // CHUNK
    mq_blk = _P_OFF["mq"] // ML_WIDTH
    mk_blk = _P_OFF["mk"] // ML_WIDTH
    start_chunks = tuple(s // CHUNK for s in geom["starts"])
    end_chunks = tuple(e // CHUNK for e in geom["ends"])

    def trio(blk):
        return [pl.BlockSpec((tr, ML_WIDTH), lambda i: (i, blk)),
                pl.BlockSpec((CHUNK, ML_WIDTH), lambda i: (jnp.maximum(i * nc - 1, 0), blk)),
                pl.BlockSpec((CHUNK, ML_WIDTH), lambda i: (jnp.minimum((i + 1) * nc, rows // CHUNK - 1), blk))]

    return pl.pallas_call(
        functools.partial(_conv_kernel, start_chunks, end_chunks),
        out_shape=(jax.ShapeDtypeStruct((rows, ML_WIDTH), BF16),
                   jax.ShapeDtypeStruct((rows * ML_HEADS, CHUNK), BF16)),
        grid=(rows // tr,),
        in_specs=trio(mq_blk) + trio(mk_blk) + [
            pl.BlockSpec((8, 2 * ML_WIDTH), lambda i: (0, 0)),
            pl.BlockSpec((8, 2 * ML_WIDTH), lambda i: (0, 0)),
            pl.BlockSpec((4, CHUNK, 2 * CHUNK), lambda i: (0, 0, 0))],
        out_specs=(pl.BlockSpec((tr, ML_WIDTH), lambda i: (i, 0)),
                   pl.BlockSpec((tr * ML_HEADS, CHUNK), lambda i: (i, 0))),
        compiler_params=_cparams(("arbitrary",)),
        name="short_conv",
    )(p, p, p, p, p, p, w8, b8, _shift_matrices())


def _split3(x):
    hi = x.astype(BF16)
    r1 = x - hi.astype(F32)
    mid = r1.astype(BF16)
    lo = (r1 - mid.astype(F32)).astype(BF16)
    return hi, mid, lo


def _cummax_lanes(xs, reverses):
    n = xs[0].shape[1]
    lane = lax.broadcasted_iota(jnp.int32, xs[0].shape, 1)
    sh = 1
    while sh < n:
        xs = [jnp.maximum(x, jnp.where(lane < n - sh, pltpu.roll(x, n - sh, 1), NEG_BIG)) if rev else
              jnp.maximum(x, jnp.where(lane >= sh, pltpu.roll(x, sh, 1), NEG_BIG)) for x, rev in zip(xs, reverses)]
        sh *= 2
    return xs


def _gate_prep_kernel(g_ref, bias_ref, o_ref):
    gl = GATE_LANES
    row = lax.broadcasted_iota(jnp.int32, (CHUNK, CHUNK), 0)
    col = lax.broadcasted_iota(jnp.int32, (CHUNK, CHUNK), 1)
    chunks = range(g_ref.shape[0] // CHUNK)
    pres = [g_ref[c * CHUNK:(c + 1) * CHUNK, :].T[0:2 * gl, :] + bias_ref[...] for c in chunks]
    lis = [pre[0:gl] for pre in pres]
    lf3s = []
    for pre in pres:
        pf = pre[gl:2 * gl]
        lf3s.append(jnp.concatenate(_split3(jnp.minimum(pf, 0.0) - jnp.log1p(jnp.exp(-jnp.abs(pf)))), axis=0))
    units = [(c, d) for c in chunks for d in range(2)]
    csum = [((row <= col) if d == 0 else (row >= col)).astype(BF16) for d in range(2)]
    cs3s = [jnp.dot(lf3s[c], csum[d], preferred_element_type=F32) for c, d in units]
    bs = [cs3[0:gl] + cs3[gl:2 * gl] + cs3[2 * gl:3 * gl] for cs3 in cs3s]
    us = [lis[c] - b for (c, d), b in zip(units, bs)]
    umaxs = _cummax_lanes(us, [d == 1 for c, d in units])
    for (c, d), b, u, umax in zip(units, bs, us, umaxs):
        g_rep = jnp.broadcast_to(b[:, CHUNK - 1:CHUNK] if d == 0 else b[:, 0:1], b.shape)
        a = g_rep - b + lis[c]
        mloc = jnp.broadcast_to(jnp.max(a, axis=1, keepdims=True), b.shape)
        r0 = (c * 2 + d) * GP_ROWS
        o_ref[r0:r0 + GP_ROWS, :] = jnp.concatenate([b, u, umax, jnp.exp(a - mloc), g_rep, mloc], axis=0)


def _gate_prep_call(gates, bias, tr):
    rows = gates.shape[0]
    nc = tr // CHUNK
    return pl.pallas_call(
        _gate_prep_kernel,
        out_shape=jax.ShapeDtypeStruct((rows // CHUNK * 2 * GP_ROWS, CHUNK), F32),
        grid=(rows // tr,),
        in_specs=[pl.BlockSpec((tr, GATE_WIDTH), lambda i: (i, 0)),
                  pl.BlockSpec((2 * GATE_LANES, CHUNK), lambda i: (0, 0))],
        out_specs=pl.BlockSpec((nc * 2 * GP_ROWS, CHUNK), lambda i: (i, 0)),
        compiler_params=_cparams(("arbitrary",)),
        name="gate_prep",
    )(gates, bias)


def _mlstm_kernel(qf_ref, kf_ref, vf_ref, gf_ref, qb_ref, kb_ref, vb_ref, gb_ref, hf_ref, hb_ref, ct_scr, m_scr):
    @pl.when(pl.program_id(1) == 0)
    def _():
        ct_scr[...] = jnp.zeros_like(ct_scr)
        m_scr[...] = jnp.zeros_like(m_scr)

    row = lax.broadcasted_iota(jnp.int32, (CHUNK, CHUNK), 0)
    col = lax.broadcasted_iota(jnp.int32, (CHUNK, CHUNK), 1)
    ones_bf = jnp.ones((CHUNK, HEAD_DIM), BF16)
    gl = GATE_LANES
    dirs = ((qf_ref, kf_ref, vf_ref, gf_ref, hf_ref), (qb_ref, kb_ref, vb_ref, gb_ref, hb_ref))
    prep = []
    for d, (_, _, _, g_ref, _) in enumerate(dirs):
        b, u, umax, w, g_rep, mloc = (g_ref[k * gl:(k + 1) * gl, :] for k in range(6))
        m0 = m_scr[d]
        mx = jnp.maximum(m0, umax)
        em = jnp.exp(LN_INV_SCALE - b - mx)
        m_new = jnp.maximum(g_rep + m0, mloc)
        a_old = jnp.exp(g_rep + m0 - m_new)
        a_new = jnp.exp(mloc - m_new)
        m_scr[d] = m_new
        cols = jnp.concatenate([-mx, em, jnp.zeros((CHUNK - 2 * gl, CHUNK), F32)], axis=0).T
        prep.append((u, w, m0, a_old, a_new, cols))
    units = [(d, h) for d in range(2) for h in range(ML_HEADS)]
    head = lambda h: slice(h * HEAD_DIM, (h + 1) * HEAD_DIM)
    qks = [jnp.dot(dirs[d][0][:, head(h)], dirs[d][1][head(h), :], preferred_element_type=F32) for d, h in units]
    s_exts, em_bs = [], []
    for (d, h), qk in zip(units, qks):
        order = (col <= row) if d == 0 else (col >= row)
        u, w, m0, a_old, a_new, cols = prep[d]
        j = d * ML_HEADS + h
        c_b = jnp.broadcast_to(cols[:, j:j + 1], (CHUNK, CHUNK))
        em_bs.append(jnp.broadcast_to(cols[:, gl + j:gl + j + 1], (CHUNK, CHUNK)))
        d_in = jnp.where(order, c_b + u[j:j + 1, :], NEG_BIG)
        d_st = c_b + m0[j:j + 1, :]
        qh = dirs[d][0][:, head(h)]
        s_exts.append(jnp.concatenate([qk * jnp.exp(d_in), qh.astype(F32) * jnp.exp(d_st)], axis=1).astype(BF16))
    for (d, h), s_ext, em_b in zip(units, s_exts, em_bs):
        j = d * ML_HEADS + h
        v_aug = jnp.concatenate([dirs[d][2][:, head(h)], ones_bf], axis=1)
        rhs = jnp.concatenate([v_aug, ct_scr[j].astype(BF16)], axis=0)
        r = jnp.dot(s_ext, rhs, preferred_element_type=F32)
        hh = r[:, :HEAD_DIM] / jnp.maximum(jnp.abs(r[:, HEAD_DIM:]), em_b)
        dirs[d][4][:, head(h)] = hh.astype(BF16)
    for d, h in units:
        u, w, m0, a_old, a_new, cols = prep[d]
        j = d * ML_HEADS + h
        v_aug = jnp.concatenate([dirs[d][2][:, head(h)], ones_bf], axis=1)
        ktw = (dirs[d][1][head(h), :].astype(F32) * w[j:j + 1, :]).astype(BF16)
        dct = jnp.dot(ktw, v_aug, preferred_element_type=F32)
        ao = jnp.concatenate([a_old[j:j + 1, :], a_old[j:j + 1, :]], axis=1)
        an = jnp.concatenate([a_new[j:j + 1, :], a_new[j:j + 1, :]], axis=1)
        ct_scr[j] = ao * ct_scr[j] + an * dct


def _mlstm_call(q, kt, p, gp, geom):
    rows = p.shape[0]
    nb, n_lat, lc, batch = geom["nb"], geom["n_lat"], geom["lc"], geom["batch"]
    ncs = lc // CHUNK
    mv_blk = _P_OFF["mv"] // ML_WIDTH

    def fwd(b, s):
        return jnp.where(s < ncs, n_lat + b * ncs + s, b * nb + (s - ncs))

    def bwd(b, s):
        return jnp.where(s < ncs, n_lat + b * ncs + (ncs - 1 - s), b * nb + (nb - 1 - (s - ncs)))

    def specs(f, d):
        return [pl.BlockSpec((CHUNK, ML_WIDTH), lambda b, s: (f(b, s), 0)),
                pl.BlockSpec((ML_WIDTH, CHUNK), lambda b, s: (f(b, s), 0)),
                pl.BlockSpec((CHUNK, ML_WIDTH), lambda b, s: (f(b, s), mv_blk)),
                pl.BlockSpec((GP_ROWS, CHUNK), lambda b, s: (2 * f(b, s) + d, 0))]

    out = jax.ShapeDtypeStruct((rows, ML_WIDTH), BF16)
    return pl.pallas_call(
        _mlstm_kernel,
        out_shape=(out, out),
        grid=(batch, ncs + nb),
        in_specs=specs(fwd, 0) + specs(bwd, 1),
        out_specs=(pl.BlockSpec((CHUNK, ML_WIDTH), lambda b, s: (fwd(b, s), 0)),
                   pl.BlockSpec((CHUNK, ML_WIDTH), lambda b, s: (bwd(b, s), 0))),
        scratch_shapes=[pltpu.VMEM((2 * ML_HEADS, HEAD_DIM, 2 * HEAD_DIM), F32),
                        pltpu.VMEM((2, GATE_LANES, CHUNK), F32)],
        compiler_params=_cparams(("arbitrary", "arbitrary")),
        name="mlstm",
    )(q, kt, p, gp, q, kt, p, gp)


def _mix_update(ya_ref, yb_ref, hf_ref, hb_ref, mo_ref, og_ref, gt_ref, w_ref):
    hs = hf_ref[...].astype(F32) + hb_ref[...].astype(F32)
    parts = [ya_ref[...], yb_ref[...]]
    for h in range(ML_HEADS):
        cs = slice(h * HEAD_DIM, (h + 1) * HEAD_DIM)
        xh = hs[:, cs]
        y = xh * lax.rsqrt(jnp.mean(xh * xh, axis=-1, keepdims=True) + EPS) * og_ref[:, cs]
        parts.append((y * jax.nn.sigmoid(mo_ref[:, cs].astype(F32))).astype(BF16))
    lhs = jnp.concatenate(parts, axis=1)
    return gt_ref[0] * jnp.dot(lhs, w_ref[...], preferred_element_type=F32)


def _mixout_kernel(x_ref, ya_ref, yb_ref, hf_ref, hb_ref, mo_ref, og_ref, gt_ref, w_ref, o_ref):
    o_ref[...] = x_ref[...] + _mix_update(ya_ref, yb_ref, hf_ref, hb_ref, mo_ref, og_ref, gt_ref, w_ref)


def _mixout_split_kernel(n_lat_tiles, x_ref, c_ref, ya_ref, yb_ref, hf_ref, hb_ref, mo_ref, og_ref, gt_ref, w_ref,
                         o_ref):
    upd = _mix_update(ya_ref, yb_ref, hf_ref, hb_ref, mo_ref, og_ref, gt_ref, w_ref)
    is_ctx = jnp.full((x_ref.shape[0], 1), pl.program_id(0), jnp.int32) >= n_lat_tiles
    o_ref[...] = jnp.where(is_ctx, c_ref[...], x_ref[...]) + upd


def _mixout_call(srcs, ya, yb, hf, hb, p, out_gain, mod3, w_out, geom, layer_mod_base, n_tiles):
    d = srcs[0].shape[1]
    rows, tm, seg = geom["rows"], geom["tm_mix"], geom["seg_mix"]
    mo_blk = _P_OFF["mo"] // ML_WIDTH
    row = lambda w, blk=0: pl.BlockSpec((tm, w), lambda i: (i, blk))
    if len(srcs) == 1:
        body, src_specs, alias = _mixout_kernel, [row(d)], {0: 0}
    else:
        n_lat_tiles = geom["r_lat"] // tm
        body = functools.partial(_mixout_split_kernel, n_lat_tiles)
        src_specs = [pl.BlockSpec((tm, d), lambda i: (jnp.minimum(i, n_lat_tiles - 1), 0)),
                     pl.BlockSpec((tm, d), lambda i: (jnp.maximum(i - n_lat_tiles, 0), 0))]
        alias = {}
    return pl.pallas_call(
        body,
        out_shape=jax.ShapeDtypeStruct((rows, d), F32),
        grid=(n_tiles,),
        in_specs=src_specs + [row(GM_WIDTH), row(ATT_WIDTH), row(ML_WIDTH), row(ML_WIDTH), row(ML_WIDTH, mo_blk),
                              pl.BlockSpec((1, ML_WIDTH), lambda i: (0, 0)),
                              pl.BlockSpec((1, 1, d), lambda i: (layer_mod_base + seg(i) * 6 + 2, 0, 0)),
                              pl.BlockSpec((MIX_WIDTH, d), lambda i: (0, 0))],
        out_specs=row(d),
        input_output_aliases=alias,
        compiler_params=_cparams(("arbitrary",)),
        name="mix_out",
    )(*srcs, ya, yb, hf, hb, p, out_gain, mod3, w_out)


def _ffn_kernel(final, x_ref, g_ref, sh_ref, sc_ref, gt_ref, w1_ref, w2_ref, gf_ref, o_ref, h_scr):
    j = pl.program_id(1)

    @pl.when(j == 0)
    def _():
        h_scr[...] = _norm_mod(x_ref[...], g_ref[...], sc_ref[0], sh_ref[0])
        o_ref[...] = jnp.zeros_like(o_ref)

    a = jnp.maximum(jnp.dot(h_scr[...], w1_ref[...], preferred_element_type=F32), 0.0)
    o_ref[...] += jnp.dot((a * a).astype(BF16), w2_ref[...], preferred_element_type=F32)

    @pl.when(j == pl.num_programs(1) - 1)
    def _():
        xo = x_ref[...] + gt_ref[0] * o_ref[...]
        if final:
            xo = xo * lax.rsqrt(jnp.mean(xo * xo, axis=-1, keepdims=True) + EPS) * gf_ref[...]
        o_ref[...] = xo


def _ffn_call(xs, gain, mod3, w1, w2, g_final, geom, layer_mod_base, n_tiles, final):
    rows, d = xs.shape
    hidden = w1.shape[1]
    tm, seg = geom["tm_mix"], geom["seg_mix"]
    th = min(1024, hidden)
    out_rows = n_tiles * tm if final else rows
    modspec = lambda k: pl.BlockSpec((1, 1, d), lambda i, j: (layer_mod_base + seg(i) * 6 + k, 0, 0))
    return pl.pallas_call(
        functools.partial(_ffn_kernel, final),
        out_shape=jax.ShapeDtypeStruct((out_rows, d), F32),
        grid=(n_tiles, hidden // th),
        in_specs=[pl.BlockSpec((tm, d), lambda i, j: (i, 0)),
                  pl.BlockSpec((1, d), lambda i, j: (0, 0)),
                  modspec(3), modspec(4), modspec(5),
                  pl.BlockSpec((d, th), lambda i, j: (0, j)),
                  pl.BlockSpec((th, d), lambda i, j: (j, 0)),
                  pl.BlockSpec((1, d), lambda i, j: (0, 0))],
        out_specs=pl.BlockSpec((tm, d), lambda i, j: (i, 0)),
        scratch_shapes=[pltpu.VMEM((tm, d), BF16)],
        compiler_params=_cparams(("arbitrary", "arbitrary")),
        name="ffn_final" if final else "ffn",
    )(xs, gain, mod3, mod3, mod3, w1, w2, g_final)


def _rope_tables(seq):
    n_rows = seq // GRID_W
    half = HEAD_DIM // 2
    freqs = (1.0 / (ROPE_BASE ** (np.arange(0, half, 2, dtype=np.float32) / np.float32(half)))).astype(np.float32)
    ang_r = (np.arange(n_rows, dtype=np.float32)[:, None] * freqs).astype(np.float32).astype(np.float64)
    ang_c = (np.arange(GRID_W, dtype=np.float32)[:, None] * freqs).astype(np.float32).astype(np.float64)

    def full(fn, ident):
        r = jnp.repeat(jnp.asarray(fn(ang_r), F32), GRID_W, axis=0)
        c = jnp.tile(jnp.asarray(fn(ang_c), F32), (n_rows, 1))
        tab = jnp.concatenate([r, r, c, c], axis=-1)
        return jnp.concatenate([tab, jnp.full((ATT_TILE, HEAD_DIM), ident, F32)], axis=0)

    i = np.arange(HEAD_DIM)
    quarter = HEAD_DIM // 4
    prot = np.zeros((HEAD_DIM, HEAD_DIM), np.float32)
    first = (i % (2 * quarter)) < quarter
    prot[(i + quarter)[first], i[first]] = -1.0
    prot[(i - quarter)[~first], i[~first]] = 1.0
    return full(np.cos, 1.0), full(np.sin, 0.0), jnp.asarray(prot, BF16)


def _geometry(batch, seq, lc, d):
    tm = 1024 if seq % 1024 == 0 else 512
    tm_mix = 512
    assert seq % tm == 0 and seq % CHUNK == 0 and lc % CHUNK == 0 and (batch * seq) % lc == 0
    r_lat, r_ctx = batch * seq, batch * lc
    rows = r_lat + r_ctx
    assert rows % tm_mix == 0 and r_ctx % tm_mix == 0 and rows % CONV_TILE == 0
    starts = tuple(b * seq for b in range(batch)) + tuple(r_lat + b * lc for b in range(batch))
    ends = tuple((b + 1) * seq - 1 for b in range(batch)) + tuple(r_lat + (b + 1) * lc - 1 for b in range(batch))
    return dict(tm=tm, tm_mix=tm_mix, rows=rows, r_lat=r_lat, r_ctx=r_ctx, batch=batch, seq=seq, lc=lc,
                nb=seq // CHUNK, n_lat=r_lat // CHUNK,
                seg=lambda i: jnp.minimum(i // (seq // tm), batch),
                seg_mix=lambda i: jnp.minimum(i // (seq // tm_mix), batch),
                starts=starts, ends=ends)


def _permute_w_in(w):
    cols = [w[:, _SRC[n]:_SRC[n] + wd] for n, wd in _P_ORDER]
    return jnp.concatenate(cols, axis=1).astype(BF16)


def _gate_w(w):
    d = w.shape[0]
    mg = w[:, _SRC["mg"]:_SRC["mg"] + 4 * ML_HEADS].reshape(d, 2, 2, ML_HEADS)
    groups = []
    for t in range(2):
        part = mg[:, :, t, :].reshape(d, 2 * ML_HEADS)
        groups.append(jnp.pad(part, ((0, 0), (0, GATE_LANES - 2 * ML_HEADS))))
    groups.append(jnp.zeros((d, GATE_WIDTH - 2 * GATE_LANES), w.dtype))
    return jnp.concatenate(groups, axis=1).astype(BF16)


def _gate_bias(b_i, b_f):
    pad = jnp.zeros((GATE_LANES - 2 * ML_HEADS,), F32)
    col = jnp.concatenate([b_i.reshape(-1), pad, b_f.reshape(-1), pad])
    return jnp.broadcast_to(col[:, None], (2 * GATE_LANES, CHUNK))


def kernel(x, c, ctx, c_ctx, w_ada, b_ada, g_mix, g_ffn, w_in, gm_v_gain, gm_w_s, gm_b_s, attn_sink,
           ml_conv_w, ml_conv_b, ml_b_i, ml_b_f, ml_out_gain, w_out, w_ff1, w_ff2, g_final):
    batch, seq, d = x.shape
    lc = ctx.shape[1]
    depth = w_ada.shape[0]
    geom = _geometry(batch, seq, lc, d)
    rows = geom["rows"]

    cc = jnp.concatenate([c, c_ctx[None, :], jnp.zeros((8 - batch - 1, d), F32)], axis=0)
    mod = _ada_call(cc, w_ada, b_ada)
    mod3 = mod[:, :batch + 1, :].reshape(depth * (batch + 1) * 6, 1, d)

    cos_tab, sin_tab, prot = _rope_tables(seq)

    srcs = (x.reshape(-1, d), ctx.reshape(-1, d))
    for layer in range(depth):
        last = layer == depth - 1
        base = layer * (batch + 1) * 6
        w_p = _permute_w_in(w_in[layer])
        w_g = _gate_w(w_in[layer])
        p, gates = _inproj_call(srcs, g_mix[layer][None, :], mod3, w_p, w_g, geom, base)

        bs_full = jnp.repeat(gm_b_s[layer].T, HEAD_DIM, axis=1)
        ya = _gmlp_call(p, gm_v_gain[layer][None, :], gm_w_s[layer].astype(BF16), bs_full, 512)

        yb = _attn_call(p, attn_sink[layer], cos_tab, sin_tab, prot, geom)

        w8 = jnp.concatenate([ml_conv_w[layer], jnp.zeros((5, 2 * ML_WIDTH), F32)], axis=0)
        b8 = jnp.concatenate([ml_conv_b[layer][None, :], jnp.zeros((7, 2 * ML_WIDTH), F32)], axis=0)
        q_conv, kt_conv = _conv_call(p, w8, b8, geom)
        n_chunks = rows // CHUNK
        prep_chunks = max(c for c in range(1, 17) if n_chunks % c == 0)
        gp = _gate_prep_call(gates, _gate_bias(ml_b_i[layer], ml_b_f[layer]), prep_chunks * CHUNK)
        hf, hb = _mlstm_call(q_conv, kt_conv, p, gp, geom)

        n_mix = (geom["r_lat"] if last else rows) // geom["tm_mix"]
        xs = _mixout_call(srcs, ya, yb, hf, hb, p, ml_out_gain[layer][None, :], mod3, w_out[layer].astype(BF16),
                          geom, base, n_mix)
        xs = _ffn_call(xs, g_ffn[layer][None, :], mod3, w_ff1[layer].astype(BF16), w_ff2[layer].astype(BF16),
                       g_final[None, :], geom, base, n_mix, last)
        srcs = (xs,)
    return xs.reshape(batch, seq, d)
```

```python
import functools
import math

import jax
import jax.numpy as jnp
import numpy as np
from jax import lax
from jax.experimental import pallas as pl
from jax.experimental.pallas import tpu as pltpu

F32 = jnp.float32
BF16 = jnp.bfloat16

HEAD_DIM = 128
CHUNK = 128
GM_GROUPS = 4
ATT_HEADS = 6
ATT_KV_HEADS = 2
ATT_GROUP = ATT_HEADS // ATT_KV_HEADS
ML_HEADS = 6
GM_WIDTH = GM_GROUPS * HEAD_DIM
ATT_WIDTH = ATT_HEADS * HEAD_DIM
ATT_KV_WIDTH = ATT_KV_HEADS * HEAD_DIM
ML_WIDTH = ML_HEADS * HEAD_DIM
MIX_WIDTH = GM_WIDTH + ATT_WIDTH + ML_WIDTH
GRID_W = 64
ROPE_BASE = 10000.0
EPS = 1e-6
NEG_BIG = -1e30
LOG2E = math.log2(math.e)
LN_INV_SCALE = 0.5 * math.log(HEAD_DIM)

ATT_TILE = 2 * CHUNK
CONV_TILE = 4 * CHUNK
GATE_LANES = 16
GP_ROWS = 6 * GATE_LANES

_SRC = dict(gu=0, gv=512, aq=1024, ak=1792, av=2048, mq=2304, mk=3072, mv=3840, mo=4608, mg=5376)
_P_ORDER = (("aq", 768), ("mq", 768), ("gu", 512), ("gv", 512), ("ak", 256), ("av", 256),
            ("mk", 768), ("mv", 768), ("mo", 768))
P_WIDTH = sum(w for _, w in _P_ORDER)
_P_OFF = {}
_o = 0
for _n, _w in _P_ORDER:
    _P_OFF[_n] = _o
    _o += _w
GATE_WIDTH = HEAD_DIM

VMEM_LIMIT = 56 * 1024 * 1024


def _cparams(sem):
    return pltpu.CompilerParams(dimension_semantics=sem, vmem_limit_bytes=VMEM_LIMIT)


def _any(preds):
    return functools.reduce(jnp.logical_or, preds)


def _ada_kernel(c_ref, w_ref, b_ref, o_ref):
    c = c_ref[...]
    s = (c * jax.nn.sigmoid(c)).astype(BF16)
    o_ref[...] = jnp.dot(s, w_ref[...].astype(BF16), preferred_element_type=F32) + b_ref[...]


def _ada_call(cc, w_ada, b_ada):
    depth, d, n = w_ada.shape
    tn = math.gcd(n, 1024)
    return pl.pallas_call(
        _ada_kernel,
        out_shape=jax.ShapeDtypeStruct((depth, cc.shape[0], n), F32),
        grid=(depth, n // tn),
        in_specs=[pl.BlockSpec((cc.shape[0], d), lambda l, j: (0, 0)),
                  pl.BlockSpec((None, d, tn), lambda l, j: (l, 0, j)),
                  pl.BlockSpec((None, 1, tn), lambda l, j: (l, 0, j))],
        out_specs=pl.BlockSpec((None, cc.shape[0], tn), lambda l, j: (l, 0, j)),
        compiler_params=_cparams(("arbitrary", "arbitrary")),
        name="ada_mod",
    )(cc, w_ada, b_ada.reshape(depth, 1, n))


def _norm_mod(x, gain, scale, shift):
    y = x * lax.rsqrt(jnp.mean(x * x, axis=-1, keepdims=True) + EPS) * gain
    return (y * (1.0 + scale) + shift).astype(BF16)


def _norm_mod_blocked(src_ref, dst_ref, gain, scale, shift, rows_per=256):
    n, d = src_ref.shape
    g2 = gain * (1.0 + scale)
    for r0 in range(0, n, rows_per):
        rs = slice(r0, r0 + rows_per)
        ss = jnp.zeros((rows_per, HEAD_DIM), F32)
        for c0 in range(0, d, HEAD_DIM):
            xb = src_ref[rs, c0:c0 + HEAD_DIM]
            ss = ss + xb * xb
        inv = lax.rsqrt(jnp.sum(ss, axis=-1, keepdims=True) * (1.0 / d) + EPS)
        for c0 in range(0, d, HEAD_DIM):
            cs = slice(c0, c0 + HEAD_DIM)
            dst_ref[rs, cs] = (src_ref[rs, cs] * inv * g2[:, cs] + shift[:, cs]).astype(BF16)


def _inproj_fill(src_ref, g_ref, sh_ref, sc_ref, wg_ref, gate_ref, h_scr):
    n = src_ref.shape[0]
    _norm_mod_blocked(src_ref, h_scr, g_ref[...], sc_ref[0], sh_ref[0])
    gate_ref[0:n, :] = jnp.dot(h_scr[0:n, :], wg_ref[...], preferred_element_type=F32)


def _inproj_kernel(x_ref, g_ref, sh_ref, sc_ref, w_ref, wg_ref, p_ref, gate_ref, h_scr):
    @pl.when(pl.program_id(1) == 0)
    def _():
        _inproj_fill(x_ref, g_ref, sh_ref, sc_ref, wg_ref, gate_ref, h_scr)

    p_ref[...] = jnp.dot(h_scr[...], w_ref[...], preferred_element_type=F32).astype(BF16)


def _inproj_split_kernel(n_lat_tiles, x_ref, c_ref, g_ref, sh_ref, sc_ref, w_ref, wg_ref, p_ref, gate_ref, h_scr):
    i, j = pl.program_id(0), pl.program_id(1)

    @pl.when(jnp.logical_and(j == 0, i < n_lat_tiles))
    def _():
        _inproj_fill(x_ref, g_ref, sh_ref, sc_ref, wg_ref, gate_ref, h_scr)

    @pl.when(jnp.logical_and(j == 0, i >= n_lat_tiles))
    def _():
        _inproj_fill(c_ref, g_ref, sh_ref, sc_ref, wg_ref, gate_ref, h_scr)

    p_ref[...] = jnp.dot(h_scr[...], w_ref[...], preferred_element_type=F32).astype(BF16)


def _inproj_call(srcs, gain, mod3, w, wg, geom, layer_mod_base):
    d = srcs[0].shape[1]
    rows, tm, seg = geom["rows"], geom["tm"], geom["seg"]
    tn = P_WIDTH // 3
    modspec = lambda k: pl.BlockSpec((1, 1, d), lambda i, j: (layer_mod_base + seg(i) * 6 + k, 0, 0))
    if len(srcs) == 1:
        body = _inproj_kernel
        src_specs = [pl.BlockSpec((tm, d), lambda i, j: (i, 0))]
    else:
        n_lat_tiles = geom["r_lat"] // tm
        assert geom["r_ctx"] <= tm
        body = functools.partial(_inproj_split_kernel, n_lat_tiles)
        src_specs = [pl.BlockSpec((tm, d), lambda i, j: (jnp.minimum(i, n_lat_tiles - 1), 0)),
                     pl.BlockSpec((geom["r_ctx"], d), lambda i, j: (0, 0))]
    return pl.pallas_call(
        body,
        out_shape=(jax.ShapeDtypeStruct((rows, P_WIDTH), BF16),
                   jax.ShapeDtypeStruct((rows, GATE_WIDTH), F32)),
        grid=(pl.cdiv(rows, tm), P_WIDTH // tn),
        in_specs=src_specs + [pl.BlockSpec((1, d), lambda i, j: (0, 0)), modspec(0), modspec(1),
                              pl.BlockSpec((d, tn), lambda i, j: (0, j)),
                              pl.BlockSpec((d, GATE_WIDTH), lambda i, j: (0, 0))],
        out_specs=(pl.BlockSpec((tm, tn), lambda i, j: (i, j)),
                   pl.BlockSpec((tm, GATE_WIDTH), lambda i, j: (i, 0))),
        scratch_shapes=[pltpu.VMEM((tm, d), BF16)],
        compiler_params=_cparams(("arbitrary", "arbitrary")),
        name="in_proj",
    )(*srcs, gain, mod3, mod3, w, wg)


def _gelu(x):
    return jax.nn.gelu(x, approximate=True)


def _gmlp_kernel(gu_ref, gv_ref, gain_ref, ws_ref, bs_ref, o_ref):
    units = [(c, g) for c in range(gu_ref.shape[0] // CHUNK) for g in range(GM_GROUPS)]
    rows = lambda c: slice(c * CHUNK, (c + 1) * CHUNK)
    cols = lambda g: slice(g * HEAD_DIM, (g + 1) * HEAD_DIM)
    vns = []
    for c, g in units:
        vg = _gelu(gv_ref[rows(c), cols(g)].astype(F32))
        vn = vg * lax.rsqrt(jnp.mean(vg * vg, axis=-1, keepdims=True) + EPS) * gain_ref[:, cols(g)]
        vns.append(vn.astype(BF16))
    ss = [jnp.dot(ws_ref[g], vn, preferred_element_type=F32) for (c, g), vn in zip(units, vns)]
    for (c, g), s in zip(units, ss):
        u = _gelu(gu_ref[rows(c), cols(g)].astype(F32))
        o_ref[rows(c), cols(g)] = (u * (s + bs_ref[:, cols(g)])).astype(BF16)


def _gmlp_call(p, v_gain, ws, bs_full, tr):
    rows = p.shape[0]
    gu_blk = _P_OFF["gu"] // GM_WIDTH
    gv_blk = _P_OFF["gv"] // GM_WIDTH
    return pl.pallas_call(
        _gmlp_kernel,
        out_shape=jax.ShapeDtypeStruct((rows, GM_WIDTH), BF16),
        grid=(rows // tr,),
        in_specs=[pl.BlockSpec((tr, GM_WIDTH), lambda i: (i, gu_blk)),
                  pl.BlockSpec((tr, GM_WIDTH), lambda i: (i, gv_blk)),
                  pl.BlockSpec((1, GM_WIDTH), lambda i: (0, 0)),
                  pl.BlockSpec((GM_GROUPS, CHUNK, CHUNK), lambda i: (0, 0, 0)),
                  pl.BlockSpec((CHUNK, GM_WIDTH), lambda i: (0, 0))],
        out_specs=pl.BlockSpec((tr, GM_WIDTH), lambda i: (i, 0)),
        compiler_params=_cparams(("arbitrary",)),
        name="gmlp",
    )(p, p, v_gain, ws, bs_full)


def _rope(x_bf, cos, sin, prot):
    rot = jnp.dot(x_bf, prot, preferred_element_type=F32)
    return (x_bf.astype(F32) * cos + rot * sin).astype(BF16)


def _attn_kernel(nt_seq, n_lat_tiles, sink_ref, q_ref, kp_ref, km_ref, kn_ref, vp_ref, vm_ref, vn_ref,
                 kx_ref, vx_ref, cp_ref, sp_ref, cm_ref, sm_ref, cn_ref, sn_ref, prot_ref, o_ref):
    t = pl.program_id(0)
    is_lat = t < n_lat_tiles
    pos = t % nt_seq
    has_cur = is_lat.astype(jnp.int32)
    has_prev = jnp.logical_and(is_lat, pos != 0).astype(jnp.int32)
    has_next = jnp.logical_and(is_lat, pos != nt_seq - 1).astype(jnp.int32)
    prot = prot_ref[...]
    nq = ATT_TILE // CHUNK
    q_scale = HEAD_DIM ** -0.5 * LOG2E

    qi = lax.broadcasted_iota(jnp.int32, (ATT_GROUP * CHUNK, 3 * CHUNK), 0) & (CHUNK - 1)
    kj = lax.broadcasted_iota(jnp.int32, (ATT_GROUP * CHUNK, 3 * CHUNK), 1)
    band = jnp.where(jnp.abs(kj - CHUNK - qi) <= CHUNK, 1, 0)
    hrow = lax.broadcasted_iota(jnp.int32, (ATT_GROUP * CHUNK, 1), 0)

    cos_k = jnp.concatenate([cp_ref[...], cm_ref[...], cn_ref[...]], axis=0)
    sin_k = jnp.concatenate([sp_ref[...], sm_ref[...], sn_ref[...]], axis=0)
    cos_q, sin_q = cm_ref[...] * q_scale, sm_ref[...] * q_scale
    head = lambda h: slice(h * HEAD_DIM, (h + 1) * HEAD_DIM)

    masks = []
    for a in range(nq):
        flags = [has_prev if a + j == 0 else (has_next if a + j == nq + 1 else has_cur) for j in range(3)]
        valid = band * jnp.where(kj < CHUNK, flags[0], jnp.where(kj < 2 * CHUNK, flags[1], flags[2])) > 0
        masks.append(jnp.where(valid, 0.0, NEG_BIG))

    k_rot = [_rope(jnp.concatenate([kp_ref[:, head(g)], km_ref[:, head(g)], kn_ref[:, head(g)]], axis=0),
                   cos_k, sin_k, prot) for g in range(ATT_KV_HEADS)]
    v_loc = [jnp.concatenate([vp_ref[:, head(g)], vm_ref[:, head(g)], vn_ref[:, head(g)]], axis=0)
             for g in range(ATT_KV_HEADS)]
    sks = [jnp.where(hrow < CHUNK, sink_ref[g * ATT_GROUP],
                     jnp.where(hrow < 2 * CHUNK, sink_ref[g * ATT_GROUP + 1], sink_ref[g * ATT_GROUP + 2])) * LOG2E
           for g in range(ATT_KV_HEADS)]
    units = [(g, a) for g in range(ATT_KV_HEADS) for a in range(nq)]
    rows = lambda a: slice(a * CHUNK, (a + 1) * CHUNK)
    q3s = [jnp.concatenate([_rope(q_ref[rows(a), head(g * ATT_GROUP + h)], cos_q[rows(a)], sin_q[rows(a)], prot)
                            for h in range(ATT_GROUP)], axis=0) for g, a in units]
    ss = [lax.dot_general(q3, jnp.concatenate([k_rot[g][a * CHUNK:(a + 3) * CHUNK], kx_ref[:, head(g)]], axis=0),
                          (((1,), (1,)), ((), ())), preferred_element_type=F32)
          for (g, a), q3 in zip(units, q3s)]
    ps, dens = [], []
    for (g, a), s in zip(units, ss):
        s_loc = s[:, :3 * CHUNK] + masks[a]
        s_ctx = s[:, 3 * CHUNK:]
        m = jnp.maximum(jnp.maximum(jnp.max(s_loc, axis=1, keepdims=True),
                                    jnp.max(s_ctx, axis=1, keepdims=True)), sks[g])
        p_loc = jnp.exp2(s_loc - m)
        p_ctx = jnp.exp2(s_ctx - m)
        dens.append(jnp.sum(p_loc, axis=1, keepdims=True) + jnp.sum(p_ctx, axis=1, keepdims=True)
                    + jnp.exp2(sks[g] - m))
        ps.append(jnp.concatenate([p_loc, p_ctx], axis=1).astype(BF16))
    for (g, a), p_all, den in zip(units, ps, dens):
        v_all = jnp.concatenate([v_loc[g][a * CHUNK:(a + 3) * CHUNK], vx_ref[:, head(g)]], axis=0)
        o = jnp.dot(p_all, v_all, preferred_element_type=F32) / den
        for h in range(ATT_GROUP):
            o_ref[rows(a), head(g * ATT_GROUP + h)] = o[h * CHUNK:(h + 1) * CHUNK, :].astype(BF16)


def _attn_call(p, sink, cos_tab, sin_tab, prot, geom):
    rows = p.shape[0]
    seq, lc, r_lat = geom["seq"], geom["lc"], geom["r_lat"]
    assert seq % ATT_TILE == 0 and lc % ATT_TILE == 0
    nt_seq, n_lat_tiles, nq = seq // ATT_TILE, r_lat // ATT_TILE, ATT_TILE // CHUNK
    ak_blk = _P_OFF["ak"] // ATT_KV_WIDTH
    av_blk = _P_OFF["av"] // ATT_KV_WIDTH
    ctx_base = r_lat // lc

    def is_lat(t):
        return t < n_lat_tiles

    def prev_ok(t):
        return jnp.logical_and(is_lat(t), t % nt_seq != 0)

    def next_ok(t):
        return jnp.logical_and(is_lat(t), t % nt_seq != nt_seq - 1)

    def prev_i(t):
        return jnp.where(prev_ok(t), nq * t - 1, nq * t)

    def next_i(t):
        return jnp.where(next_ok(t), nq * t + nq, nq * t + nq - 1)

    def tab_main(t):
        return jnp.where(is_lat(t), t % nt_seq, nt_seq)

    def tab_prev(t):
        return jnp.where(prev_ok(t), nq * (t % nt_seq) - 1, nq * nt_seq)

    def tab_next(t):
        return jnp.where(next_ok(t), nq * (t % nt_seq) + nq, nq * nt_seq)

    def ctx_i(t):
        return ctx_base + jnp.where(is_lat(t), t // nt_seq, (t - n_lat_tiles) // (lc // ATT_TILE))

    halo = lambda f, blk: pl.BlockSpec((CHUNK, ATT_KV_WIDTH), lambda t: (f(t), blk))
    main = lambda blk: pl.BlockSpec((ATT_TILE, ATT_KV_WIDTH), lambda t: (t, blk))
    tb = lambda rws, f: pl.BlockSpec((rws, HEAD_DIM), lambda t: (f(t), 0))
    return pl.pallas_call(
        functools.partial(_attn_kernel, nt_seq, n_lat_tiles),
        out_shape=jax.ShapeDtypeStruct((rows, ATT_WIDTH), BF16),
        grid=(rows // ATT_TILE,),
        in_specs=[pl.BlockSpec(memory_space=pltpu.SMEM),
                  pl.BlockSpec((ATT_TILE, ATT_WIDTH), lambda t: (t, 0)),
                  halo(prev_i, ak_blk), main(ak_blk), halo(next_i, ak_blk),
                  halo(prev_i, av_blk), main(av_blk), halo(next_i, av_blk),
                  pl.BlockSpec((lc, ATT_KV_WIDTH), lambda t: (ctx_i(t), ak_blk)),
                  pl.BlockSpec((lc, ATT_KV_WIDTH), lambda t: (ctx_i(t), av_blk)),
                  tb(CHUNK, tab_prev), tb(CHUNK, tab_prev), tb(ATT_TILE, tab_main), tb(ATT_TILE, tab_main),
                  tb(CHUNK, tab_next), tb(CHUNK, tab_next),
                  pl.BlockSpec((HEAD_DIM, HEAD_DIM), lambda t: (0, 0))],
        out_specs=pl.BlockSpec((ATT_TILE, ATT_WIDTH), lambda t: (t, 0)),
        compiler_params=_cparams(("arbitrary",)),
        name="window_attn",
    )(sink, p, p, p, p, p, p, p, p, p, cos_tab, sin_tab, cos_tab, sin_tab, cos_tab, sin_tab, prot)


def _conv_kernel(start_chunks, end_chunks, q_ref, qp_ref, qn_ref, k_ref, kp_ref, kn_ref, w_ref, b_ref, sh_ref,
                 oq_ref, okt_ref):
    nc = q_ref.shape[0] // CHUNK
    i = pl.program_id(0)
    srcs = ((q_ref, qp_ref, qn_ref), (k_ref, kp_ref, kn_ref))
    units = [(part, c) for part in range(2) for c in range(nc)]
    shifted = []
    for part, c in units:
        x_ref, xp_ref, xn_ref = srcs[part]
        cg = i * nc + c
        at_start = _any([cg == s for s in start_chunks])
        at_end = _any([cg == e for e in end_chunks])
        cur = x_ref[c * CHUNK:(c + 1) * CHUNK, :]
        prv = xp_ref[...] if c == 0 else x_ref[(c - 1) * CHUNK:c * CHUNK, :]
        nxt = xn_ref[...] if c == nc - 1 else x_ref[(c + 1) * CHUNK:(c + 2) * CHUNK, :]
        s_prev = sh_ref[jnp.where(at_start, 1, 0)]
        s_next = sh_ref[jnp.where(at_end, 3, 2)]
        shifted.append((jnp.dot(s_prev, jnp.concatenate([prv, cur], axis=0), preferred_element_type=F32),
                        jnp.dot(s_next, jnp.concatenate([cur, nxt], axis=0), preferred_element_type=F32)))
    for (part, c), (x_prev, x_next) in zip(units, shifted):
        cs = slice(part * ML_WIDTH, (part + 1) * ML_WIDTH)
        cur = srcs[part][0][c * CHUNK:(c + 1) * CHUNK, :].astype(F32)
        y = b_ref[0:1, cs] + x_prev * w_ref[0:1, cs] + cur * w_ref[1:2, cs] + x_next * w_ref[2:3, cs]
        y = y * jax.nn.sigmoid(y)
        if part == 0:
            oq_ref[c * CHUNK:(c + 1) * CHUNK, :] = y.astype(BF16)
        else:
            for h in range(ML_HEADS):
                r0 = (c * ML_HEADS + h) * HEAD_DIM
                okt_ref[r0:r0 + HEAD_DIM, :] = y[:, h * HEAD_DIM:(h + 1) * HEAD_DIM].T.astype(BF16)


def _shift_matrices():
    t = np.arange(CHUNK)
    prev = np.zeros((CHUNK, 2 * CHUNK), np.float32)
    prev[t, t + CHUNK - 1] = 1.0
    nxt = np.zeros((CHUNK, 2 * CHUNK), np.float32)
    nxt[t, t + 1] = 1.0
    prev_cut, nxt_cut = prev.copy(), nxt.copy()
    prev_cut[0] = 0.0
    nxt_cut[CHUNK - 1] = 0.0
    return jnp.asarray(np.stack([prev, prev_cut, nxt, nxt_cut]), BF16)


def _conv_call(p, w8, b8, geom):
    rows = p.shape[0]
    tr = CONV_TILE
    nc = tr // CHUNK
    mq_blk = _P_OFF["mq"] // ML_WIDTH
    mk_blk = _P_OFF["mk"] // ML_WIDTH
    start_chunks = tuple(s // CHUNK for s in geom["starts"])
    end_chunks = tuple(e // CHUNK for e in geom["ends"])

    def trio(blk):
        return [pl.BlockSpec((tr, ML_WIDTH), lambda i: (i, blk)),
                pl.BlockSpec((CHUNK, ML_WIDTH), lambda i: (jnp.maximum(i * nc - 1, 0), blk)),
                pl.BlockSpec((CHUNK, ML_WIDTH), lambda i: (jnp.minimum((i + 1) * nc, rows // CHUNK - 1), blk))]

    return pl.pallas_call(
        functools.partial(_conv_kernel, start_chunks, end_chunks),
        out_shape=(jax.ShapeDtypeStruct((rows, ML_WIDTH), BF16),
                   jax.ShapeDtypeStruct((rows * ML_HEADS, CHUNK), BF16)),
        grid=(rows // tr,),
        in_specs=trio(mq_blk) + trio(mk_blk) + [
            pl.BlockSpec((8, 2 * ML_WIDTH), lambda i: (0, 0)),
            pl.BlockSpec((8, 2 * ML_WIDTH), lambda i: (0, 0)),
            pl.BlockSpec((4, CHUNK, 2 * CHUNK), lambda i: (0, 0, 0))],
        out_specs=(pl.BlockSpec((tr, ML_WIDTH), lambda i: (i, 0)),
                   pl.BlockSpec((tr * ML_HEADS, CHUNK), lambda i: (i, 0))),
        compiler_params=_cparams(("arbitrary",)),
        name="short_conv",
    )(p, p, p, p, p, p, w8, b8, _shift_matrices())


def _split3(x):
    hi = x.astype(BF16)
    r1 = x - hi.astype(F32)
    mid = r1.astype(BF16)
    lo = (r1 - mid.astype(F32)).astype(BF16)
    return hi, mid, lo


def _cummax_lanes(xs, reverses):
    n = xs[0].shape[1]
    lane = lax.broadcasted_iota(jnp.int32, xs[0].shape, 1)
    sh = 1
    while sh < n:
        xs = [jnp.maximum(x, jnp.where(lane < n - sh, pltpu.roll(x, n - sh, 1), NEG_BIG)) if rev else
              jnp.maximum(x, jnp.where(lane >= sh, pltpu.roll(x, sh, 1), NEG_BIG)) for x, rev in zip(xs, reverses)]
        sh *= 2
    return xs


def _gate_prep_kernel(g_ref, bias_ref, o_ref):
    gl = GATE_LANES
    row = lax.broadcasted_iota(jnp.int32, (CHUNK, CHUNK), 0)
    col = lax.broadcasted_iota(jnp.int32, (CHUNK, CHUNK), 1)
    chunks = range(g_ref.shape[0] // CHUNK)
    pres = [g_ref[c * CHUNK:(c + 1) * CHUNK, :].T[0:2 * gl, :] + bias_ref[...] for c in chunks]
    lis = [pre[0:gl] for pre in pres]
    lf3s = []
    for pre in pres:
        pf = pre[gl:2 * gl]
        lf3s.append(jnp.concatenate(_split3(jnp.minimum(pf, 0.0) - jnp.log1p(jnp.exp(-jnp.abs(pf)))), axis=0))
    units = [(c, d) for c in chunks for d in range(2)]
    csum = [((row <= col) if d == 0 else (row >= col)).astype(BF16) for d in range(2)]
    cs3s = [jnp.dot(lf3s[c], csum[d], preferred_element_type=F32) for c, d in units]
    bs = [cs3[0:gl] + cs3[gl:2 * gl] + cs3[2 * gl:3 * gl] for cs3 in cs3s]
    us = [lis[c] - b for (c, d), b in zip(units, bs)]
    umaxs = _cummax_lanes(us, [d == 1 for c, d in units])
    for (c, d), b, u, umax in zip(units, bs, us, umaxs):
        g_rep = jnp.broadcast_to(b[:, CHUNK - 1:CHUNK] if d == 0 else b[:, 0:1], b.shape)
        a = g_rep - b + lis[c]
        mloc = jnp.broadcast_to(jnp.max(a, axis=1, keepdims=True), b.shape)
        r0 = (c * 2 + d) * GP_ROWS
        o_ref[r0:r0 + GP_ROWS, :] = jnp.concatenate([b, u, umax, jnp.exp(a - mloc), g_rep, mloc], axis=0)


def _gate_prep_call(gates, bias, tr):
    rows = gates.shape[0]
    nc = tr // CHUNK
    return pl.pallas_call(
        _gate_prep_kernel,
        out_shape=jax.ShapeDtypeStruct((rows // CHUNK * 2 * GP_ROWS, CHUNK), F32),
        grid=(rows // tr,),
        in_specs=[pl.BlockSpec((tr, GATE_WIDTH), lambda i: (i, 0)),
                  pl.BlockSpec((2 * GATE_LANES, CHUNK), lambda i: (0, 0))],
        out_specs=pl.BlockSpec((nc * 2 * GP_ROWS, CHUNK), lambda i: (i, 0)),
        compiler_params=_cparams(("arbitrary",)),
        name="gate_prep",
    )(gates, bias)


def _mlstm_kernel(qf_ref, kf_ref, vf_ref, gf_ref, qb_ref, kb_ref, vb_ref, gb_ref, hf_ref, hb_ref, ct_scr, m_scr):
    @pl.when(pl.program_id(1) == 0)
    def _():
        ct_scr[...] = jnp.zeros_like(ct_scr)
        m_scr[...] = jnp.zeros_like(m_scr)

    row = lax.broadcasted_iota(jnp.int32, (CHUNK, CHUNK), 0)
    col = lax.broadcasted_iota(jnp.int32, (CHUNK, CHUNK), 1)
    ones_bf = jnp.ones((CHUNK, HEAD_DIM), BF16)
    gl = GATE_LANES
    dirs = ((qf_ref, kf_ref, vf_ref, gf_ref, hf_ref), (qb_ref, kb_ref, vb_ref, gb_ref, hb_ref))
    prep = []
    for d, (_, _, _, g_ref, _) in enumerate(dirs):
        b, u, umax, w, g_rep, mloc = (g_ref[k * gl:(k + 1) * gl, :] for k in range(6))
        m0 = m_scr[d]
        mx = jnp.maximum(m0, umax)
        em = jnp.exp(LN_INV_SCALE - b - mx)
        m_new = jnp.maximum(g_rep + m0, mloc)
        a_old = jnp.exp(g_rep + m0 - m_new)
        w_new = w * jnp.exp(mloc - m_new)
        m_scr[d] = m_new
        cols = jnp.concatenate([mx * -LOG2E, em, jnp.zeros((CHUNK - 2 * gl, CHUNK), F32)], axis=0).T
        prep.append((u * LOG2E, w_new, m0 * LOG2E, a_old, cols))
    units = [(d, h) for d in range(2) for h in range(ML_HEADS)]
    head = lambda h: slice(h * HEAD_DIM, (h + 1) * HEAD_DIM)
    qks = [jnp.dot(dirs[d][0][:, head(h)], dirs[d][1][head(h), :], preferred_element_type=F32) for d, h in units]
    s_exts, em_bs = [], []
    for (d, h), qk in zip(units, qks):
        order = (col <= row) if d == 0 else (col >= row)
        u2, _, m02, _, cols = prep[d]
        j = d * ML_HEADS + h
        c_b = jnp.broadcast_to(cols[:, j:j + 1], (CHUNK, CHUNK))
        em_bs.append(jnp.broadcast_to(cols[:, gl + j:gl + j + 1], (CHUNK, CHUNK)))
        d_in = jnp.where(order, c_b + u2[j:j + 1, :], NEG_BIG)
        d_st = c_b + m02[j:j + 1, :]
        qh = dirs[d][0][:, head(h)]
        s_exts.append(jnp.concatenate([qk * jnp.exp2(d_in), qh.astype(F32) * jnp.exp2(d_st)], axis=1).astype(BF16))
    for (d, h), s_ext, em_b in zip(units, s_exts, em_bs):
        j = d * ML_HEADS + h
        v_aug = jnp.concatenate([dirs[d][2][:, head(h)], ones_bf], axis=1)
        rhs = jnp.concatenate([v_aug, ct_scr[j].astype(BF16)], axis=0)
        r = jnp.dot(s_ext, rhs, preferred_element_type=F32)
        hh = r[:, :HEAD_DIM] / jnp.maximum(jnp.abs(r[:, HEAD_DIM:]), em_b)
        dirs[d][4][:, head(h)] = hh.astype(BF16)
    for d, h in units:
        _, w_new, _, a_old, _ = prep[d]
        j = d * ML_HEADS + h
        v_aug = jnp.concatenate([dirs[d][2][:, head(h)], ones_bf], axis=1)
        ktw = (dirs[d][1][head(h), :].astype(F32) * w_new[j:j + 1, :]).astype(BF16)
        dct = jnp.dot(ktw, v_aug, preferred_element_type=F32)
        ao = jnp.concatenate([a_old[j:j + 1, :], a_old[j:j + 1, :]], axis=1)
        ct_scr[j] = ao * ct_scr[j] + dct


def _mlstm_call(q, kt, p, gp, geom):
    rows = p.shape[0]
    nb, n_lat, lc, batch = geom["nb"], geom["n_lat"], geom["lc"], geom["batch"]
    ncs = lc // CHUNK
    mv_blk = _P_OFF["mv"] // ML_WIDTH

    def fwd(b, s):
        return jnp.where(s < ncs, n_lat + b * ncs + s, b * nb + (s - ncs))

    def bwd(b, s):
        return jnp.where(s < ncs, n_lat + b * ncs + (ncs - 1 - s), b * nb + (nb - 1 - (s - ncs)))

    def specs(f, d):
        return [pl.BlockSpec((CHUNK, ML_WIDTH), lambda b, s: (f(b, s), 0)),
                pl.BlockSpec((ML_WIDTH, CHUNK), lambda b, s: (f(b, s), 0)),
                pl.BlockSpec((CHUNK, ML_WIDTH), lambda b, s: (f(b, s), mv_blk)),
                pl.BlockSpec((GP_ROWS, CHUNK), lambda b, s: (2 * f(b, s) + d, 0))]

    out = jax.ShapeDtypeStruct((rows, ML_WIDTH), BF16)
    return pl.pallas_call(
        _mlstm_kernel,
        out_shape=(out, out),
        grid=(batch, ncs + nb),
        in_specs=specs(fwd, 0) + specs(bwd, 1),
        out_specs=(pl.BlockSpec((CHUNK, ML_WIDTH), lambda b, s: (fwd(b, s), 0)),
                   pl.BlockSpec((CHUNK, ML_WIDTH), lambda b, s: (bwd(b, s), 0))),
        scratch_shapes=[pltpu.VMEM((2 * ML_HEADS, HEAD_DIM, 2 * HEAD_DIM), F32),
                        pltpu.VMEM((2, GATE_LANES, CHUNK), F32)],
        compiler_params=_cparams(("arbitrary", "arbitrary")),
        name="mlstm",
    )(q, kt, p, gp, q, kt, p, gp)


def _mix_update(ya_ref, yb_ref, hf_ref, hb_ref, mo_ref, og_ref, gt_ref, w_ref):
    hs = hf_ref[...].astype(F32) + hb_ref[...].astype(F32)
    parts = [ya_ref[...], yb_ref[...]]
    for h in range(ML_HEADS):
        cs = slice(h * HEAD_DIM, (h + 1) * HEAD_DIM)
        xh = hs[:, cs]
        y = xh * lax.rsqrt(jnp.mean(xh * xh, axis=-1, keepdims=True) + EPS) * og_ref[:, cs]
        parts.append((y * jax.nn.sigmoid(mo_ref[:, cs].astype(F32))).astype(BF16))
    lhs = jnp.concatenate(parts, axis=1)
    return gt_ref[0] * jnp.dot(lhs, w_ref[...], preferred_element_type=F32)


def _mixout_kernel(x_ref, ya_ref, yb_ref, hf_ref, hb_ref, mo_ref, og_ref, gt_ref, w_ref, o_ref):
    o_ref[...] = x_ref[...] + _mix_update(ya_ref, yb_ref, hf_ref, hb_ref, mo_ref, og_ref, gt_ref, w_ref)


def _mixout_split_kernel(n_lat_tiles, x_ref, c_ref, ya_ref, yb_ref, hf_ref, hb_ref, mo_ref, og_ref, gt_ref, w_ref,
                         o_ref):
    upd = _mix_update(ya_ref, yb_ref, hf_ref, hb_ref, mo_ref, og_ref, gt_ref, w_ref)
    is_ctx = jnp.full((x_ref.shape[0], 1), pl.program_id(0), jnp.int32) >= n_lat_tiles
    o_ref[...] = jnp.where(is_ctx, c_ref[...], x_ref[...]) + upd


def _mixout_call(srcs, ya, yb, hf, hb, p, out_gain, mod3, w_out, geom, layer_mod_base, n_tiles):
    d = srcs[0].shape[1]
    rows, tm, seg = geom["rows"], geom["tm_mix"], geom["seg_mix"]
    mo_blk = _P_OFF["mo"] // ML_WIDTH
    row = lambda w, blk=0: pl.BlockSpec((tm, w), lambda i: (i, blk))
    if len(srcs) == 1:
        body, src_specs, alias = _mixout_kernel, [row(d)], {0: 0}
    else:
        n_lat_tiles = geom["r_lat"] // tm
        body = functools.partial(_mixout_split_kernel, n_lat_tiles)
        src_specs = [pl.BlockSpec((tm, d), lambda i: (jnp.minimum(i, n_lat_tiles - 1), 0)),
                     pl.BlockSpec((tm, d), lambda i: (jnp.maximum(i - n_lat_tiles, 0), 0))]
        alias = {}
    return pl.pallas_call(
        body,
        out_shape=jax.ShapeDtypeStruct((rows, d), F32),
        grid=(n_tiles,),
        in_specs=src_specs + [row(GM_WIDTH), row(ATT_WIDTH), row(ML_WIDTH), row(ML_WIDTH), row(ML_WIDTH, mo_blk),
                              pl.BlockSpec((1, ML_WIDTH), lambda i: (0, 0)),
                              pl.BlockSpec((1, 1, d), lambda i: (layer_mod_base + seg(i) * 6 + 2, 0, 0)),
                              pl.BlockSpec((MIX_WIDTH, d), lambda i: (0, 0))],
        out_specs=row(d),
        input_output_aliases=alias,
        compiler_params=_cparams(("arbitrary",)),
        name="mix_out",
    )(*srcs, ya, yb, hf, hb, p, out_gain, mod3, w_out)


def _ffn_kernel(final, x_ref, g_ref, sh_ref, sc_ref, gt_ref, w1_ref, w2_ref, gf_ref, o_ref, h_scr):
    j = pl.program_id(1)

    @pl.when(j == 0)
    def _():
        _norm_mod_blocked(x_ref, h_scr, g_ref[...], sc_ref[0], sh_ref[0])
        o_ref[...] = jnp.zeros_like(o_ref)

    a = jnp.maximum(jnp.dot(h_scr[...], w1_ref[...], preferred_element_type=F32), 0.0)
    o_ref[...] += jnp.dot((a * a).astype(BF16), w2_ref[...], preferred_element_type=F32)

    @pl.when(j == pl.num_programs(1) - 1)
    def _():
        xo = x_ref[...] + gt_ref[0] * o_ref[...]
        if final:
            xo = xo * lax.rsqrt(jnp.mean(xo * xo, axis=-1, keepdims=True) + EPS) * gf_ref[...]
        o_ref[...] = xo


def _ffn_call(xs, gain, mod3, w1, w2, g_final, geom, layer_mod_base, n_tiles, final):
    rows, d = xs.shape
    hidden = w1.shape[1]
    tm, seg = geom["tm_mix"], geom["seg_mix"]
    th = min(1024, hidden)
    out_rows = n_tiles * tm if final else rows
    modspec = lambda k: pl.BlockSpec((1, 1, d), lambda i, j: (layer_mod_base + seg(i) * 6 + k, 0, 0))
    return pl.pallas_call(
        functools.partial(_ffn_kernel, final),
        out_shape=jax.ShapeDtypeStruct((out_rows, d), F32),
        grid=(n_tiles, hidden // th),
        in_specs=[pl.BlockSpec((tm, d), lambda i, j: (i, 0)),
                  pl.BlockSpec((1, d), lambda i, j: (0, 0)),
                  modspec(3), modspec(4), modspec(5),
                  pl.BlockSpec((d, th), lambda i, j: (0, j)),
                  pl.BlockSpec((th, d), lambda i, j: (j, 0)),
                  pl.BlockSpec((1, d), lambda i, j: (0, 0))],
        out_specs=pl.BlockSpec((tm, d), lambda i, j: (i, 0)),
        scratch_shapes=[pltpu.VMEM((tm, d), BF16)],
        compiler_params=_cparams(("arbitrary", "arbitrary")),
        name="ffn_final" if final else "ffn",
    )(xs, gain, mod3, mod3, mod3, w1, w2, g_final)


def _rope_tables(seq):
    n_rows = seq // GRID_W
    half = HEAD_DIM // 2
    freqs = (1.0 / (ROPE_BASE ** (np.arange(0, half, 2, dtype=np.float32) / np.float32(half)))).astype(np.float32)
    ang_r = (np.arange(n_rows, dtype=np.float32)[:, None] * freqs).astype(np.float32).astype(np.float64)
    ang_c = (np.arange(GRID_W, dtype=np.float32)[:, None] * freqs).astype(np.float32).astype(np.float64)

    def full(fn, ident):
        r = jnp.repeat(jnp.asarray(fn(ang_r), F32), GRID_W, axis=0)
        c = jnp.tile(jnp.asarray(fn(ang_c), F32), (n_rows, 1))
        tab = jnp.concatenate([r, r, c, c], axis=-1)
        return jnp.concatenate([tab, jnp.full((ATT_TILE, HEAD_DIM), ident, F32)], axis=0)

    i = np.arange(HEAD_DIM)
    quarter = HEAD_DIM // 4
    prot = np.zeros((HEAD_DIM, HEAD_DIM), np.float32)
    first = (i % (2 * quarter)) < quarter
    prot[(i + quarter)[first], i[first]] = -1.0
    prot[(i - quarter)[~first], i[~first]] = 1.0
    return full(np.cos, 1.0), full(np.sin, 0.0), jnp.asarray(prot, BF16)


def _geometry(batch, seq, lc, d):
    tm = 1024 if seq % 1024 == 0 else 512
    tm_mix = 512
    assert seq % tm == 0 and seq % CHUNK == 0 and lc % CHUNK == 0 and (batch * seq) % lc == 0
    r_lat, r_ctx = batch * seq, batch * lc
    rows = r_lat + r_ctx
    assert rows % tm_mix == 0 and r_ctx % tm_mix == 0 and rows % CONV_TILE == 0
    starts = tuple(b * seq for b in range(batch)) + tuple(r_lat + b * lc for b in range(batch))
    ends = tuple((b + 1) * seq - 1 for b in range(batch)) + tuple(r_lat + (b + 1) * lc - 1 for b in range(batch))
    return dict(tm=tm, tm_mix=tm_mix, rows=rows, r_lat=r_lat, r_ctx=r_ctx, batch=batch, seq=seq, lc=lc,
                nb=seq // CHUNK, n_lat=r_lat // CHUNK,
                seg=lambda i: jnp.minimum(i // (seq // tm), batch),
                seg_mix=lambda i: jnp.minimum(i // (seq // tm_mix), batch),
                starts=starts, ends=ends)


def _permute_w_in(w):
    cols = [w[:, _SRC[n]:_SRC[n] + wd] for n, wd in _P_ORDER]
    return jnp.concatenate(cols, axis=1).astype(BF16)


def _gate_w(w):
    d = w.shape[0]
    mg = w[:, _SRC["mg"]:_SRC["mg"] + 4 * ML_HEADS].reshape(d, 2, 2, ML_HEADS)
    groups = []
    for t in range(2):
        part = mg[:, :, t, :].reshape(d, 2 * ML_HEADS)
        groups.append(jnp.pad(part, ((0, 0), (0, GATE_LANES - 2 * ML_HEADS))))
    groups.append(jnp.zeros((d, GATE_WIDTH - 2 * GATE_LANES), w.dtype))
    return jnp.concatenate(groups, axis=1).astype(BF16)


def _gate_bias(b_i, b_f):
    pad = jnp.zeros((GATE_LANES - 2 * ML_HEADS,), F32)
    col = jnp.concatenate([b_i.reshape(-1), pad, b_f.reshape(-1), pad])
    return jnp.broadcast_to(col[:, None], (2 * GATE_LANES, CHUNK))


def kernel(x, c, ctx, c_ctx, w_ada, b_ada, g_mix, g_ffn, w_in, gm_v_gain, gm_w_s, gm_b_s, attn_sink,
           ml_conv_w, ml_conv_b, ml_b_i, ml_b_f, ml_out_gain, w_out, w_ff1, w_ff2, g_final):
    batch, seq, d = x.shape
    lc = ctx.shape[1]
    depth = w_ada.shape[0]
    geom = _geometry(batch, seq, lc, d)
    rows = geom["rows"]

    cc = jnp.concatenate([c, c_ctx[None, :], jnp.zeros((8 - batch - 1, d), F32)], axis=0)
    mod = _ada_call(cc, w_ada, b_ada)
    mod3 = mod[:, :batch + 1, :].reshape(depth * (batch + 1) * 6, 1, d)

    cos_tab, sin_tab, prot = _rope_tables(seq)

    srcs = (x.reshape(-1, d), ctx.reshape(-1, d))
    for layer in range(depth):
        last = layer == depth - 1
        base = layer * (batch + 1) * 6
        w_p = _permute_w_in(w_in[layer])
        w_g = _gate_w(w_in[layer])
        p, gates = _inproj_call(srcs, g_mix[layer][None, :], mod3, w_p, w_g, geom, base)

        bs_full = jnp.repeat(gm_b_s[layer].T, HEAD_DIM, axis=1)
        ya = _gmlp_call(p, gm_v_gain[layer][None, :], gm_w_s[layer].astype(BF16), bs_full, 512)

        yb = _attn_call(p, attn_sink[layer], cos_tab, sin_tab, prot, geom)

        w8 = jnp.concatenate([ml_conv_w[layer], jnp.zeros((5, 2 * ML_WIDTH), F32)], axis=0)
        b8 = jnp.concatenate([ml_conv_b[layer][None, :], jnp.zeros((7, 2 * ML_WIDTH), F32)], axis=0)
        q_conv, kt_conv = _conv_call(p, w8, b8, geom)
        n_chunks = rows // CHUNK
        prep_chunks = max(c for c in range(1, 17) if n_chunks % c == 0)
        gp = _gate_prep_call(gates, _gate_bias(ml_b_i[layer], ml_b_f[layer]), prep_chunks * CHUNK)
        hf, hb = _mlstm_call(q_conv, kt_conv, p, gp, geom)

        n_mix = (geom["r_lat"] if last else rows) // geom["tm_mix"]
        xs = _mixout_call(srcs, ya, yb, hf, hb, p, ml_out_gain[layer][None, :], mod3, w_out[layer].astype(BF16),
                          geom, base, n_mix)
        xs = _ffn_call(xs, g_ffn[layer][None, :], mod3, w_ff1[layer].astype(BF16), w_ff2[layer].astype(BF16),
                       g_final[None, :], geom, base, n_mix, last)
        srcs = (xs,)
    return xs.reshape(batch, seq, d)
```

```python
import functools
import math

import jax
import jax.numpy as jnp
import numpy as np
from jax import lax
from jax.experimental import pallas as pl
from jax.experimental.pallas import tpu as pltpu

F32 = jnp.float32
BF16 = jnp.bfloat16

HEAD_DIM = 128
CHUNK = 128
GM_GROUPS = 4
ATT_HEADS = 6
ATT_KV_HEADS = 2
ATT_GROUP = ATT_HEADS // ATT_KV_HEADS
ML_HEADS = 6
GM_WIDTH = GM_GROUPS * HEAD_DIM
ATT_WIDTH = ATT_HEADS * HEAD_DIM
ATT_KV_WIDTH = ATT_KV_HEADS * HEAD_DIM
ML_WIDTH = ML_HEADS * HEAD_DIM
MIX_WIDTH = GM_WIDTH + ATT_WIDTH + ML_WIDTH
GRID_W = 64
ROPE_BASE = 10000.0
EPS = 1e-6
NEG_BIG = -1e30
LOG2E = math.log2(math.e)
LN_INV_SCALE = 0.5 * math.log(HEAD_DIM)

ATT_TILE = 2 * CHUNK
CONV_TILE = 4 * CHUNK
GATE_LANES = 16
GP_ROWS = 6 * GATE_LANES

_SRC = dict(gu=0, gv=512, aq=1024, ak=1792, av=2048, mq=2304, mk=3072, mv=3840, mo=4608, mg=5376)
_P_ORDER = (("aq", 768), ("mq", 768), ("gu", 512), ("gv", 512), ("ak", 256), ("av", 256),
            ("mk", 768), ("mv", 768), ("mo", 768))
P_WIDTH = sum(w for _, w in _P_ORDER)
_P_OFF = {}
_o = 0
for _n, _w in _P_ORDER:
    _P_OFF[_n] = _o
    _o += _w
GATE_WIDTH = HEAD_DIM

VMEM_LIMIT = 56 * 1024 * 1024


def _cparams(sem):
    return pltpu.CompilerParams(dimension_semantics=sem, vmem_limit_bytes=VMEM_LIMIT)


def _any(preds):
    return functools.reduce(jnp.logical_or, preds)


def _ada_kernel(c_ref, w_ref, b_ref, o_ref):
    c = c_ref[...]
    s = (c * jax.nn.sigmoid(c)).astype(BF16)
    o_ref[...] = jnp.dot(s, w_ref[...].astype(BF16), preferred_element_type=F32) + b_ref[...]


def _ada_call(cc, w_ada, b_ada):
    depth, d, n = w_ada.shape
    tn = math.gcd(n, 1024)
    return pl.pallas_call(
        _ada_kernel,
        out_shape=jax.ShapeDtypeStruct((depth, cc.shape[0], n), F32),
        grid=(depth, n // tn),
        in_specs=[pl.BlockSpec((cc.shape[0], d), lambda l, j: (0, 0)),
                  pl.BlockSpec((None, d, tn), lambda l, j: (l, 0, j)),
                  pl.BlockSpec((None, 1, tn), lambda l, j: (l, 0, j))],
        out_specs=pl.BlockSpec((None, cc.shape[0], tn), lambda l, j: (l, 0, j)),
        compiler_params=_cparams(("arbitrary", "arbitrary")),
        name="ada_mod",
    )(cc, w_ada, b_ada.reshape(depth, 1, n))


def _norm_mod(x, gain, scale, shift):
    y = x * lax.rsqrt(jnp.mean(x * x, axis=-1, keepdims=True) + EPS) * gain
    return (y * (1.0 + scale) + shift).astype(BF16)


def _norm_mod_blocked(src_ref, dst_ref, gain, scale, shift, rows_per=256):
    n, d = src_ref.shape
    g2 = gain * (1.0 + scale)
    for r0 in range(0, n, rows_per):
        rs = slice(r0, r0 + rows_per)
        ss = jnp.zeros((rows_per, HEAD_DIM), F32)
        for c0 in range(0, d, HEAD_DIM):
            xb = src_ref[rs, c0:c0 + HEAD_DIM]
            ss = ss + xb * xb
        inv = lax.rsqrt(jnp.sum(ss, axis=-1, keepdims=True) * (1.0 / d) + EPS)
        for c0 in range(0, d, HEAD_DIM):
            cs = slice(c0, c0 + HEAD_DIM)
            dst_ref[rs, cs] = (src_ref[rs, cs] * inv * g2[:, cs] + shift[:, cs]).astype(BF16)


def _inproj_fill(src_ref, g_ref, sh_ref, sc_ref, wg_ref, gate_ref, h_scr):
    n = src_ref.shape[0]
    _norm_mod_blocked(src_ref, h_scr, g_ref[...], sc_ref[0], sh_ref[0])
    gate_ref[0:n, :] = jnp.dot(h_scr[0:n, :], wg_ref[...], preferred_element_type=F32)


SIDE_LANES = 1024


def _side_cast(stack, layer, n_steps, step):
    total = stack.shape[1] * stack.shape[2]
    assert total % SIDE_LANES == 0
    r = total // SIDE_LANES
    nblk = 1
    while nblk * 2 <= n_steps and r % (nblk * 2) == 0 and (r // (nblk * 2)) % 16 == 0:
        nblk *= 2
    blk = lambda *ids: jnp.minimum(step(*ids), nblk - 1)
    in_spec = pl.BlockSpec((r // nblk, SIDE_LANES), lambda *ids: (layer * nblk + blk(*ids), 0))
    out_spec = pl.BlockSpec((r // nblk, SIDE_LANES), lambda *ids: (blk(*ids), 0))
    return stack.reshape(-1, SIDE_LANES), in_spec, out_spec, jax.ShapeDtypeStruct((r, SIDE_LANES), BF16)


def _inproj_kernel(x_ref, g_ref, sh_ref, sc_ref, w_ref, wg_ref, side_ref, p_ref, gate_ref, side_out_ref, h_scr):
    @pl.when(pl.program_id(1) == 0)
    def _():
        _inproj_fill(x_ref, g_ref, sh_ref, sc_ref, wg_ref, gate_ref, h_scr)

    p_ref[...] = jnp.dot(h_scr[...], w_ref[...], preferred_element_type=F32).astype(BF16)
    side_out_ref[...] = side_ref[...].astype(BF16)


def _inproj_split_kernel(n_lat_tiles, x_ref, c_ref, g_ref, sh_ref, sc_ref, w_ref, wg_ref, side_ref,
                         p_ref, gate_ref, side_out_ref, h_scr):
    i, j = pl.program_id(0), pl.program_id(1)

    @pl.when(jnp.logical_and(j == 0, i < n_lat_tiles))
    def _():
        _inproj_fill(x_ref, g_ref, sh_ref, sc_ref, wg_ref, gate_ref, h_scr)

    @pl.when(jnp.logical_and(j == 0, i >= n_lat_tiles))
    def _():
        _inproj_fill(c_ref, g_ref, sh_ref, sc_ref, wg_ref, gate_ref, h_scr)

    p_ref[...] = jnp.dot(h_scr[...], w_ref[...], preferred_element_type=F32).astype(BF16)
    side_out_ref[...] = side_ref[...].astype(BF16)


def _inproj_call(srcs, gain, mod3, w, wg, side_stack, layer, geom, layer_mod_base):
    d = srcs[0].shape[1]
    rows, tm, seg = geom["rows"], geom["tm"], geom["seg"]
    tn = P_WIDTH // 3
    side_view, side_in, side_out, side_shape = _side_cast(side_stack, layer, pl.cdiv(rows, tm), lambda i, j: i)
    modspec = lambda k: pl.BlockSpec((1, 1, d), lambda i, j: (layer_mod_base + seg(i) * 6 + k, 0, 0))
    if len(srcs) == 1:
        body = _inproj_kernel
        src_specs = [pl.BlockSpec((tm, d), lambda i, j: (i, 0))]
    else:
        n_lat_tiles = geom["r_lat"] // tm
        assert geom["r_ctx"] <= tm
        body = functools.partial(_inproj_split_kernel, n_lat_tiles)
        src_specs = [pl.BlockSpec((tm, d), lambda i, j: (jnp.minimum(i, n_lat_tiles - 1), 0)),
                     pl.BlockSpec((geom["r_ctx"], d), lambda i, j: (0, 0))]
    p, gates, side_bf = pl.pallas_call(
        body,
        out_shape=(jax.ShapeDtypeStruct((rows, P_WIDTH), BF16),
                   jax.ShapeDtypeStruct((rows, GATE_WIDTH), F32), side_shape),
        grid=(pl.cdiv(rows, tm), P_WIDTH // tn),
        in_specs=src_specs + [pl.BlockSpec((1, d), lambda i, j: (0, 0)), modspec(0), modspec(1),
                              pl.BlockSpec((d, tn), lambda i, j: (0, j)),
                              pl.BlockSpec((d, GATE_WIDTH), lambda i, j: (0, 0)), side_in],
        out_specs=(pl.BlockSpec((tm, tn), lambda i, j: (i, j)),
                   pl.BlockSpec((tm, GATE_WIDTH), lambda i, j: (i, 0)), side_out),
        scratch_shapes=[pltpu.VMEM((tm, d), BF16)],
        compiler_params=_cparams(("arbitrary", "arbitrary")),
        name="in_proj",
    )(*srcs, gain, mod3, mod3, w, wg, side_view)
    return p, gates, side_bf.reshape(side_stack.shape[1:])


def _gelu(x):
    return jax.nn.gelu(x, approximate=True)


def _gmlp_kernel(gu_ref, gv_ref, gain_ref, ws_ref, bs_ref, o_ref):
    units = [(c, g) for c in range(gu_ref.shape[0] // CHUNK) for g in range(GM_GROUPS)]
    rows = lambda c: slice(c * CHUNK, (c + 1) * CHUNK)
    cols = lambda g: slice(g * HEAD_DIM, (g + 1) * HEAD_DIM)
    vns = []
    for c, g in units:
        vg = _gelu(gv_ref[rows(c), cols(g)].astype(F32))
        vn = vg * lax.rsqrt(jnp.mean(vg * vg, axis=-1, keepdims=True) + EPS) * gain_ref[:, cols(g)]
        vns.append(vn.astype(BF16))
    ss = [jnp.dot(ws_ref[g], vn, preferred_element_type=F32) for (c, g), vn in zip(units, vns)]
    for (c, g), s in zip(units, ss):
        u = _gelu(gu_ref[rows(c), cols(g)].astype(F32))
        o_ref[rows(c), cols(g)] = (u * (s + bs_ref[:, cols(g)])).astype(BF16)


def _gmlp_call(p, v_gain, ws, bs_full, tr):
    rows = p.shape[0]
    gu_blk = _P_OFF["gu"] // GM_WIDTH
    gv_blk = _P_OFF["gv"] // GM_WIDTH
    return pl.pallas_call(
        _gmlp_kernel,
        out_shape=jax.ShapeDtypeStruct((rows, GM_WIDTH), BF16),
        grid=(rows // tr,),
        in_specs=[pl.BlockSpec((tr, GM_WIDTH), lambda i: (i, gu_blk)),
                  pl.BlockSpec((tr, GM_WIDTH), lambda i: (i, gv_blk)),
                  pl.BlockSpec((1, GM_WIDTH), lambda i: (0, 0)),
                  pl.BlockSpec((GM_GROUPS, CHUNK, CHUNK), lambda i: (0, 0, 0)),
                  pl.BlockSpec((CHUNK, GM_WIDTH), lambda i: (0, 0))],
        out_specs=pl.BlockSpec((tr, GM_WIDTH), lambda i: (i, 0)),
        compiler_params=_cparams(("arbitrary",)),
        name="gmlp",
    )(p, p, v_gain, ws, bs_full)


def _rope(x_bf, cos, sin, prot):
    rot = jnp.dot(x_bf, prot, preferred_element_type=F32)
    return (x_bf.astype(F32) * cos + rot * sin).astype(BF16)


def _attn_kernel(nt_seq, n_lat_tiles, sink_ref, q_ref, kp_ref, km_ref, kn_ref, vp_ref, vm_ref, vn_ref,
                 kx_ref, vx_ref, cp_ref, sp_ref, cm_ref, sm_ref, cn_ref, sn_ref, prot_ref, o_ref):
    t = pl.program_id(0)
    is_lat = t < n_lat_tiles
    pos = t % nt_seq
    has_cur = is_lat.astype(jnp.int32)
    has_prev = jnp.logical_and(is_lat, pos != 0).astype(jnp.int32)
    has_next = jnp.logical_and(is_lat, pos != nt_seq - 1).astype(jnp.int32)
    prot = prot_ref[...]
    nq = ATT_TILE // CHUNK
    q_scale = HEAD_DIM ** -0.5 * LOG2E

    qi = lax.broadcasted_iota(jnp.int32, (ATT_GROUP * CHUNK, 3 * CHUNK), 0) & (CHUNK - 1)
    kj = lax.broadcasted_iota(jnp.int32, (ATT_GROUP * CHUNK, 3 * CHUNK), 1)
    band = jnp.where(jnp.abs(kj - CHUNK - qi) <= CHUNK, 1, 0)
    hrow = lax.broadcasted_iota(jnp.int32, (ATT_GROUP * CHUNK, 1), 0)

    cos_k = jnp.concatenate([cp_ref[...], cm_ref[...], cn_ref[...]], axis=0)
    sin_k = jnp.concatenate([sp_ref[...], sm_ref[...], sn_ref[...]], axis=0)
    cos_q, sin_q = cm_ref[...] * q_scale, sm_ref[...] * q_scale
    head = lambda h: slice(h * HEAD_DIM, (h + 1) * HEAD_DIM)

    masks = []
    for a in range(nq):
        flags = [has_prev if a + j == 0 else (has_next if a + j == nq + 1 else has_cur) for j in range(3)]
        valid = band * jnp.where(kj < CHUNK, flags[0], jnp.where(kj < 2 * CHUNK, flags[1], flags[2])) > 0
        masks.append(jnp.where(valid, 0.0, NEG_BIG))

    k_rot = [_rope(jnp.concatenate([kp_ref[:, head(g)], km_ref[:, head(g)], kn_ref[:, head(g)]], axis=0),
                   cos_k, sin_k, prot) for g in range(ATT_KV_HEADS)]
    v_loc = [jnp.concatenate([vp_ref[:, head(g)], vm_ref[:, head(g)], vn_ref[:, head(g)]], axis=0)
             for g in range(ATT_KV_HEADS)]
    sks = [jnp.where(hrow < CHUNK, sink_ref[g * ATT_GROUP],
                     jnp.where(hrow < 2 * CHUNK, sink_ref[g * ATT_GROUP + 1], sink_ref[g * ATT_GROUP + 2])) * LOG2E
           for g in range(ATT_KV_HEADS)]
    units = [(g, a) for g in range(ATT_KV_HEADS) for a in range(nq)]
    rows = lambda a: slice(a * CHUNK, (a + 1) * CHUNK)
    q3s = [jnp.concatenate([_rope(q_ref[rows(a), head(g * ATT_GROUP + h)], cos_q[rows(a)], sin_q[rows(a)], prot)
                            for h in range(ATT_GROUP)], axis=0) for g, a in units]
    ss = [lax.dot_general(q3, jnp.concatenate([k_rot[g][a * CHUNK:(a + 3) * CHUNK], kx_ref[:, head(g)]], axis=0),
                          (((1,), (1,)), ((), ())), preferred_element_type=F32)
          for (g, a), q3 in zip(units, q3s)]
    ps, p_sinks = [], []
    for (g, a), s in zip(units, ss):
        s_loc = s[:, :3 * CHUNK] + masks[a]
        s_ctx = s[:, 3 * CHUNK:]
        m = jnp.maximum(jnp.maximum(jnp.max(s_loc, axis=1, keepdims=True),
                                    jnp.max(s_ctx, axis=1, keepdims=True)), sks[g])
        p_sinks.append(jnp.exp2(sks[g] - m))
        ps.append(jnp.concatenate([jnp.exp2(s_loc - m), jnp.exp2(s_ctx - m)], axis=1).astype(BF16))
    ones_bf = jnp.ones((3 * CHUNK + kx_ref.shape[0], HEAD_DIM), BF16)
    for (g, a), p_all, p_sink in zip(units, ps, p_sinks):
        v_all = jnp.concatenate([v_loc[g][a * CHUNK:(a + 3) * CHUNK], vx_ref[:, head(g)]], axis=0)
        r = jnp.dot(p_all, jnp.concatenate([v_all, ones_bf], axis=1), preferred_element_type=F32)
        o = r[:, :HEAD_DIM] / (r[:, HEAD_DIM:] + p_sink)
        for h in range(ATT_GROUP):
            o_ref[rows(a), head(g * ATT_GROUP + h)] = o[h * CHUNK:(h + 1) * CHUNK, :].astype(BF16)


def _attn_call(p, sink, cos_tab, sin_tab, prot, geom):
    rows = p.shape[0]
    seq, lc, r_lat = geom["seq"], geom["lc"], geom["r_lat"]
    assert seq % ATT_TILE == 0 and lc % ATT_TILE == 0
    nt_seq, n_lat_tiles, nq = seq // ATT_TILE, r_lat // ATT_TILE, ATT_TILE // CHUNK
    ak_blk = _P_OFF["ak"] // ATT_KV_WIDTH
    av_blk = _P_OFF["av"] // ATT_KV_WIDTH
    ctx_base = r_lat // lc

    def is_lat(t):
        return t < n_lat_tiles

    def prev_ok(t):
        return jnp.logical_and(is_lat(t), t % nt_seq != 0)

    def next_ok(t):
        return jnp.logical_and(is_lat(t), t % nt_seq != nt_seq - 1)

    def prev_i(t):
        return jnp.where(prev_ok(t), nq * t - 1, nq * t)

    def next_i(t):
        return jnp.where(next_ok(t), nq * t + nq, nq * t + nq - 1)

    def tab_main(t):
        return jnp.where(is_lat(t), t % nt_seq, nt_seq)

    def tab_prev(t):
        return jnp.where(prev_ok(t), nq * (t % nt_seq) - 1, nq * nt_seq)

    def tab_next(t):
        return jnp.where(next_ok(t), nq * (t % nt_seq) + nq, nq * nt_seq)

    def ctx_i(t):
        return ctx_base + jnp.where(is_lat(t), t // nt_seq, (t - n_lat_tiles) // (lc // ATT_TILE))

    halo = lambda f, blk: pl.BlockSpec((CHUNK, ATT_KV_WIDTH), lambda t: (f(t), blk))
    main = lambda blk: pl.BlockSpec((ATT_TILE, ATT_KV_WIDTH), lambda t: (t, blk))
    tb = lambda rws, f: pl.BlockSpec((rws, HEAD_DIM), lambda t: (f(t), 0))
    return pl.pallas_call(
        functools.partial(_attn_kernel, nt_seq, n_lat_tiles),
        out_shape=jax.ShapeDtypeStruct((rows, ATT_WIDTH), BF16),
        grid=(rows // ATT_TILE,),
        in_specs=[pl.BlockSpec(memory_space=pltpu.SMEM),
                  pl.BlockSpec((ATT_TILE, ATT_WIDTH), lambda t: (t, 0)),
                  halo(prev_i, ak_blk), main(ak_blk), halo(next_i, ak_blk),
                  halo(prev_i, av_blk), main(av_blk), halo(next_i, av_blk),
                  pl.BlockSpec((lc, ATT_KV_WIDTH), lambda t: (ctx_i(t), ak_blk)),
                  pl.BlockSpec((lc, ATT_KV_WIDTH), lambda t: (ctx_i(t), av_blk)),
                  tb(CHUNK, tab_prev), tb(CHUNK, tab_prev), tb(ATT_TILE, tab_main), tb(ATT_TILE, tab_main),
                  tb(CHUNK, tab_next), tb(CHUNK, tab_next),
                  pl.BlockSpec((HEAD_DIM, HEAD_DIM), lambda t: (0, 0))],
        out_specs=pl.BlockSpec((ATT_TILE, ATT_WIDTH), lambda t: (t, 0)),
        compiler_params=_cparams(("arbitrary",)),
        name="window_attn",
    )(sink, p, p, p, p, p, p, p, p, p, cos_tab, sin_tab, cos_tab, sin_tab, cos_tab, sin_tab, prot)


def _conv_kernel(start_chunks, end_chunks, q_ref, qp_ref, qn_ref, k_ref, kp_ref, kn_ref, w_ref, b_ref, sh_ref,
                 oq_ref, okt_ref):
    nc = q_ref.shape[0] // CHUNK
    i = pl.program_id(0)
    srcs = ((q_ref, qp_ref, qn_ref), (k_ref, kp_ref, kn_ref))
    units = [(part, c) for part in range(2) for c in range(nc)]
    shifted = []
    for part, c in units:
        x_ref, xp_ref, xn_ref = srcs[part]
        cg = i * nc + c
        at_start = _any([cg == s for s in start_chunks])
        at_end = _any([cg == e for e in end_chunks])
        cur = x_ref[c * CHUNK:(c + 1) * CHUNK, :]
        prv = xp_ref[...] if c == 0 else x_ref[(c - 1) * CHUNK:c * CHUNK, :]
        nxt = xn_ref[...] if c == nc - 1 else x_ref[(c + 1) * CHUNK:(c + 2) * CHUNK, :]
        s_prev = sh_ref[jnp.where(at_start, 1, 0)]
        s_next = sh_ref[jnp.where(at_end, 3, 2)]
        shifted.append((jnp.dot(s_prev, jnp.concatenate([prv, cur], axis=0), preferred_element_type=F32),
                        jnp.dot(s_next, jnp.concatenate([cur, nxt], axis=0), preferred_element_type=F32)))
    for (part, c), (x_prev, x_next) in zip(units, shifted):
        cs = slice(part * ML_WIDTH, (part + 1) * ML_WIDTH)
        cur = srcs[part][0][c * CHUNK:(c + 1) * CHUNK, :].astype(F32)
        y = b_ref[0:1, cs] + x_prev * w_ref[0:1, cs] + cur * w_ref[1:2, cs] + x_next * w_ref[2:3, cs]
        hy = 0.5 * y
        y = hy + hy * jnp.tanh(hy)
        if part == 0:
            oq_ref[c * CHUNK:(c + 1) * CHUNK, :] = y.astype(BF16)
        else:
            for h in range(ML_HEADS):
                r0 = (c * ML_HEADS + h) * HEAD_DIM
                okt_ref[r0:r0 + HEAD_DIM, :] = y[:, h * HEAD_DIM:(h + 1) * HEAD_DIM].T.astype(BF16)


def _shift_matrices():
    t = np.arange(CHUNK)
    prev = np.zeros((CHUNK, 2 * CHUNK), np.float32)
    prev[t, t + CHUNK - 1] = 1.0
    nxt = np.zeros((CHUNK, 2 * CHUNK), np.float32)
    nxt[t, t + 1] = 1.0
    prev_cut, nxt_cut = prev.copy(), nxt.copy()
    prev_cut[0] = 0.0
    nxt_cut[CHUNK - 1] = 0.0
    return jnp.asarray(np.stack([prev, prev_cut, nxt, nxt_cut]), BF16)


def _conv_call(p, w8, b8, geom):
    rows = p.shape[0]
    tr = CONV_TILE
    nc = tr // CHUNK
    mq_blk = _P_OFF["mq"] // ML_WIDTH
    mk_blk = _P_OFF["mk"] // ML_WIDTH
    start_chunks = tuple(s // CHUNK for s in geom["starts"])
    end_chunks = tuple(e // CHUNK for e in geom["ends"])

    def trio(blk):
        return [pl.BlockSpec((tr, ML_WIDTH), lambda i: (i, blk)),
                pl.BlockSpec((CHUNK, ML_WIDTH), lambda i: (jnp.maximum(i * nc - 1, 0), blk)),
                pl.BlockSpec((CHUNK, ML_WIDTH), lambda i: (jnp.minimum((i + 1) * nc, rows // CHUNK - 1), blk))]

    return pl.pallas_call(
        functools.partial(_conv_kernel, start_chunks, end_chunks),
        out_shape=(jax.ShapeDtypeStruct((rows, ML_WIDTH), BF16),
                   jax.ShapeDtypeStruct((rows * ML_HEADS, CHUNK), BF16)),
        grid=(rows // tr,),
        in_specs=trio(mq_blk) + trio(mk_blk) + [
            pl.BlockSpec((8, 2 * ML_WIDTH), lambda i: (0, 0)),
            pl.BlockSpec((8, 2 * ML_WIDTH), lambda i: (0, 0)),
            pl.BlockSpec((4, CHUNK, 2 * CHUNK), lambda i: (0, 0, 0))],
        out_specs=(pl.BlockSpec((tr, ML_WIDTH), lambda i: (i, 0)),
                   pl.BlockSpec((tr * ML_HEADS, CHUNK), lambda i: (i, 0))),
        compiler_params=_cparams(("arbitrary",)),
        name="short_conv",
    )(p, p, p, p, p, p, w8, b8, _shift_matrices())


def _split3(x):
    hi = x.astype(BF16)
    r1 = x - hi.astype(F32)
    mid = r1.astype(BF16)
    lo = (r1 - mid.astype(F32)).astype(BF16)
    return hi, mid, lo


def _cummax_lanes(xs, reverses):
    n = xs[0].shape[1]
    lane = lax.broadcasted_iota(jnp.int32, xs[0].shape, 1)
    sh = 1
    while sh < n:
        xs = [jnp.maximum(x, jnp.where(lane < n - sh, pltpu.roll(x, n - sh, 1), NEG_BIG)) if rev else
              jnp.maximum(x, jnp.where(lane >= sh, pltpu.roll(x, sh, 1), NEG_BIG)) for x, rev in zip(xs, reverses)]
        sh *= 2
    return xs


def _gate_prep_kernel(g_ref, bias_ref, o_ref):
    gl = GATE_LANES
    row = lax.broadcasted_iota(jnp.int32, (CHUNK, CHUNK), 0)
    col = lax.broadcasted_iota(jnp.int32, (CHUNK, CHUNK), 1)
    chunks = range(g_ref.shape[0] // CHUNK)
    pres = [g_ref[c * CHUNK:(c + 1) * CHUNK, :].T[0:2 * gl, :] + bias_ref[...] for c in chunks]
    lis = [pre[0:gl] for pre in pres]
    lf3s = []
    for pre in pres:
        pf = pre[gl:2 * gl]
        lf3s.append(jnp.concatenate(_split3(jnp.minimum(pf, 0.0) - jnp.log1p(jnp.exp(-jnp.abs(pf)))), axis=0))
    units = [(c, d) for c in chunks for d in range(2)]
    csum = [((row <= col) if d == 0 else (row >= col)).astype(BF16) for d in range(2)]
    cs3s = [jnp.dot(lf3s[c], csum[d], preferred_element_type=F32) for c, d in units]
    bs = [cs3[0:gl] + cs3[gl:2 * gl] + cs3[2 * gl:3 * gl] for cs3 in cs3s]
    us = [lis[c] - b for (c, d), b in zip(units, bs)]
    umaxs = _cummax_lanes(us, [d == 1 for c, d in units])
    for (c, d), b, u, umax in zip(units, bs, us, umaxs):
        g_rep = jnp.broadcast_to(b[:, CHUNK - 1:CHUNK] if d == 0 else b[:, 0:1], b.shape)
        a = g_rep - b + lis[c]
        mloc = jnp.broadcast_to(jnp.max(a, axis=1, keepdims=True), b.shape)
        r0 = (c * 2 + d) * GP_ROWS
        o_ref[r0:r0 + GP_ROWS, :] = jnp.concatenate([b, u, umax, jnp.exp(a - mloc), g_rep, mloc], axis=0)


def _gate_prep_call(gates, bias, tr):
    rows = gates.shape[0]
    nc = tr // CHUNK
    return pl.pallas_call(
        _gate_prep_kernel,
        out_shape=jax.ShapeDtypeStruct((rows // CHUNK * 2 * GP_ROWS, CHUNK), F32),
        grid=(rows // tr,),
        in_specs=[pl.BlockSpec((tr, GATE_WIDTH), lambda i: (i, 0)),
                  pl.BlockSpec((2 * GATE_LANES, CHUNK), lambda i: (0, 0))],
        out_specs=pl.BlockSpec((nc * 2 * GP_ROWS, CHUNK), lambda i: (i, 0)),
        compiler_params=_cparams(("arbitrary",)),
        name="gate_prep",
    )(gates, bias)


def _mlstm_kernel(qf_ref, kf_ref, vf_ref, gf_ref, qb_ref, kb_ref, vb_ref, gb_ref, hf_ref, hb_ref, ct_scr, m_scr):
    @pl.when(pl.program_id(1) == 0)
    def _():
        ct_scr[...] = jnp.zeros_like(ct_scr)
        m_scr[...] = jnp.zeros_like(m_scr)

    row = lax.broadcasted_iota(jnp.int32, (CHUNK, CHUNK), 0)
    col = lax.broadcasted_iota(jnp.int32, (CHUNK, CHUNK), 1)
    ones_bf = jnp.ones((CHUNK, HEAD_DIM), BF16)
    gl = GATE_LANES
    dirs = ((qf_ref, kf_ref, vf_ref, gf_ref, hf_ref), (qb_ref, kb_ref, vb_ref, gb_ref, hb_ref))
    prep = []
    for d, (_, _, _, g_ref, _) in enumerate(dirs):
        b, u, umax, w, g_rep, mloc = (g_ref[k * gl:(k + 1) * gl, :] for k in range(6))
        m0 = m_scr[d]
        mx = jnp.maximum(m0, umax)
        em = jnp.exp(LN_INV_SCALE - b - mx)
        m_new = jnp.maximum(g_rep + m0, mloc)
        a_old = jnp.exp(g_rep + m0 - m_new)
        w_new = w * jnp.exp(mloc - m_new)
        m_scr[d] = m_new
        cols = jnp.concatenate([mx * -LOG2E, em, jnp.zeros((CHUNK - 2 * gl, CHUNK), F32)], axis=0).T
        prep.append((u * LOG2E, w_new, m0 * LOG2E, a_old, cols))
    units = [(d, h) for d in range(2) for h in range(ML_HEADS)]
    head = lambda h: slice(h * HEAD_DIM, (h + 1) * HEAD_DIM)
    qks = [jnp.dot(dirs[d][0][:, head(h)], dirs[d][1][head(h), :], preferred_element_type=F32) for d, h in units]
    s_exts, em_bs = [], []
    for (d, h), qk in zip(units, qks):
        order = (col <= row) if d == 0 else (col >= row)
        u2, _, m02, _, cols = prep[d]
        j = d * ML_HEADS + h
        c_b = jnp.broadcast_to(cols[:, j:j + 1], (CHUNK, CHUNK))
        em_bs.append(jnp.broadcast_to(cols[:, gl + j:gl + j + 1], (CHUNK, CHUNK)))
        d_in = jnp.where(order, c_b + u2[j:j + 1, :], NEG_BIG)
        d_st = c_b + m02[j:j + 1, :]
        qh = dirs[d][0][:, head(h)]
        s_exts.append(jnp.concatenate([qk * jnp.exp2(d_in), qh.astype(F32) * jnp.exp2(d_st)], axis=1).astype(BF16))
    for (d, h), s_ext, em_b in zip(units, s_exts, em_bs):
        j = d * ML_HEADS + h
        v_aug = jnp.concatenate([dirs[d][2][:, head(h)], ones_bf], axis=1)
        rhs = jnp.concatenate([v_aug, ct_scr[j].astype(BF16)], axis=0)
        r = jnp.dot(s_ext, rhs, preferred_element_type=F32)
        hh = r[:, :HEAD_DIM] / jnp.maximum(jnp.abs(r[:, HEAD_DIM:]), em_b)
        dirs[d][4][:, head(h)] = hh.astype(BF16)
    for d, h in units:
        _, w_new, _, a_old, _ = prep[d]
        j = d * ML_HEADS + h
        v_aug = jnp.concatenate([dirs[d][2][:, head(h)], ones_bf], axis=1)
        ktw = (dirs[d][1][head(h), :].astype(F32) * w_new[j:j + 1, :]).astype(BF16)
        dct = jnp.dot(ktw, v_aug, preferred_element_type=F32)
        ao = jnp.concatenate([a_old[j:j + 1, :], a_old[j:j + 1, :]], axis=1)
        ct_scr[j] = ao * ct_scr[j] + dct


def _mlstm_call(q, kt, p, gp, geom):
    rows = p.shape[0]
    nb, n_lat, lc, batch = geom["nb"], geom["n_lat"], geom["lc"], geom["batch"]
    ncs = lc // CHUNK
    mv_blk = _P_OFF["mv"] // ML_WIDTH

    def fwd(b, s):
        return jnp.where(s < ncs, n_lat + b * ncs + s, b * nb + (s - ncs))

    def bwd(b, s):
        return jnp.where(s < ncs, n_lat + b * ncs + (ncs - 1 - s), b * nb + (nb - 1 - (s - ncs)))

    def specs(f, d):
        return [pl.BlockSpec((CHUNK, ML_WIDTH), lambda b, s: (f(b, s), 0)),
                pl.BlockSpec((ML_WIDTH, CHUNK), lambda b, s: (f(b, s), 0)),
                pl.BlockSpec((CHUNK, ML_WIDTH), lambda b, s: (f(b, s), mv_blk)),
                pl.BlockSpec((GP_ROWS, CHUNK), lambda b, s: (2 * f(b, s) + d, 0))]

    out = jax.ShapeDtypeStruct((rows, ML_WIDTH), BF16)
    return pl.pallas_call(
        _mlstm_kernel,
        out_shape=(out, out),
        grid=(batch, ncs + nb),
        in_specs=specs(fwd, 0) + specs(bwd, 1),
        out_specs=(pl.BlockSpec((CHUNK, ML_WIDTH), lambda b, s: (fwd(b, s), 0)),
                   pl.BlockSpec((CHUNK, ML_WIDTH), lambda b, s: (bwd(b, s), 0))),
        scratch_shapes=[pltpu.VMEM((2 * ML_HEADS, HEAD_DIM, 2 * HEAD_DIM), F32),
                        pltpu.VMEM((2, GATE_LANES, CHUNK), F32)],
        compiler_params=_cparams(("arbitrary", "arbitrary")),
        name="mlstm",
    )(q, kt, p, gp, q, kt, p, gp)


def _mix_update(ya_ref, yb_ref, hf_ref, hb_ref, mo_ref, og_ref, gt_ref, w_ref):
    hs = hf_ref[...].astype(F32) + hb_ref[...].astype(F32)
    parts = [ya_ref[...], yb_ref[...]]
    for h in range(ML_HEADS):
        cs = slice(h * HEAD_DIM, (h + 1) * HEAD_DIM)
        xh = hs[:, cs]
        y = xh * lax.rsqrt(jnp.mean(xh * xh, axis=-1, keepdims=True) + EPS) * og_ref[:, cs]
        parts.append((y * jax.nn.sigmoid(mo_ref[:, cs].astype(F32))).astype(BF16))
    lhs = jnp.concatenate(parts, axis=1)
    return gt_ref[0] * jnp.dot(lhs, w_ref[...], preferred_element_type=F32)


def _mixout_kernel(x_ref, ya_ref, yb_ref, hf_ref, hb_ref, mo_ref, og_ref, gt_ref, w_ref, s1_ref, s2_ref,
                   o_ref, s1_out_ref, s2_out_ref):
    o_ref[...] = x_ref[...] + _mix_update(ya_ref, yb_ref, hf_ref, hb_ref, mo_ref, og_ref, gt_ref, w_ref)
    s1_out_ref[...] = s1_ref[...].astype(BF16)
    s2_out_ref[...] = s2_ref[...].astype(BF16)


def _mixout_split_kernel(n_lat_tiles, x_ref, c_ref, ya_ref, yb_ref, hf_ref, hb_ref, mo_ref, og_ref, gt_ref, w_ref,
                         s1_ref, s2_ref, o_ref, s1_out_ref, s2_out_ref):
    upd = _mix_update(ya_ref, yb_ref, hf_ref, hb_ref, mo_ref, og_ref, gt_ref, w_ref)
    is_ctx = jnp.full((x_ref.shape[0], 1), pl.program_id(0), jnp.int32) >= n_lat_tiles
    o_ref[...] = jnp.where(is_ctx, c_ref[...], x_ref[...]) + upd
    s1_out_ref[...] = s1_ref[...].astype(BF16)
    s2_out_ref[...] = s2_ref[...].astype(BF16)


def _mixout_call(srcs, ya, yb, hf, hb, p, out_gain, mod3, w_out, side_stacks, layer, geom, layer_mod_base, n_tiles):
    d = srcs[0].shape[1]
    rows, tm, seg = geom["rows"], geom["tm_mix"], geom["seg_mix"]
    mo_blk = _P_OFF["mo"] // ML_WIDTH
    row = lambda w, blk=0: pl.BlockSpec((tm, w), lambda i: (i, blk))
    side = [_side_cast(stack, layer, n_tiles, lambda i: i) for stack in side_stacks]
    if len(srcs) == 1:
        body, src_specs, alias = _mixout_kernel, [row(d)], {0: 0}
    else:
        n_lat_tiles = geom["r_lat"] // tm
        body = functools.partial(_mixout_split_kernel, n_lat_tiles)
        src_specs = [pl.BlockSpec((tm, d), lambda i: (jnp.minimum(i, n_lat_tiles - 1), 0)),
                     pl.BlockSpec((tm, d), lambda i: (jnp.maximum(i - n_lat_tiles, 0), 0))]
        alias = {}
    xs, s1, s2 = pl.pallas_call(
        body,
        out_shape=(jax.ShapeDtypeStruct((rows, d), F32), side[0][3], side[1][3]),
        grid=(n_tiles,),
        in_specs=src_specs + [row(GM_WIDTH), row(ATT_WIDTH), row(ML_WIDTH), row(ML_WIDTH), row(ML_WIDTH, mo_blk),
                              pl.BlockSpec((1, ML_WIDTH), lambda i: (0, 0)),
                              pl.BlockSpec((1, 1, d), lambda i: (layer_mod_base + seg(i) * 6 + 2, 0, 0)),
                              pl.BlockSpec((MIX_WIDTH, d), lambda i: (0, 0)), side[0][1], side[1][1]],
        out_specs=(row(d), side[0][2], side[1][2]),
        input_output_aliases=alias,
        compiler_params=_cparams(("arbitrary",)),
        name="mix_out",
    )(*srcs, ya, yb, hf, hb, p, out_gain, mod3, w_out, side[0][0], side[1][0])
    return xs, s1.reshape(side_stacks[0].shape[1:]), s2.reshape(side_stacks[1].shape[1:])


def _ffn_kernel(final, x_ref, g_ref, sh_ref, sc_ref, gt_ref, w1_ref, w2_ref, gf_ref, o_ref, h_scr):
    j = pl.program_id(1)

    @pl.when(j == 0)
    def _():
        _norm_mod_blocked(x_ref, h_scr, g_ref[...], sc_ref[0], sh_ref[0])
        o_ref[...] = jnp.zeros_like(o_ref)

    a = jnp.maximum(jnp.dot(h_scr[...], w1_ref[...], preferred_element_type=F32), 0.0)
    o_ref[...] += jnp.dot((a * a).astype(BF16), w2_ref[...], preferred_element_type=F32)

    @pl.when(j == pl.num_programs(1) - 1)
    def _():
        xo = x_ref[...] + gt_ref[0] * o_ref[...]
        if final:
            xo = xo * lax.rsqrt(jnp.mean(xo * xo, axis=-1, keepdims=True) + EPS) * gf_ref[...]
        o_ref[...] = xo


def _ffn_call(xs, gain, mod3, w1, w2, g_final, geom, layer_mod_base, n_tiles, final):
    rows, d = xs.shape
    hidden = w1.shape[1]
    tm, seg = geom["tm_mix"], geom["seg_mix"]
    th = min(1024, hidden)
    out_rows = n_tiles * tm if final else rows
    modspec = lambda k: pl.BlockSpec((1, 1, d), lambda i, j: (layer_mod_base + seg(i) * 6 + k, 0, 0))
    return pl.pallas_call(
        functools.partial(_ffn_kernel, final),
        out_shape=jax.ShapeDtypeStruct((out_rows, d), F32),
        grid=(n_tiles, hidden // th),
        in_specs=[pl.BlockSpec((tm, d), lambda i, j: (i, 0)),
                  pl.BlockSpec((1, d), lambda i, j: (0, 0)),
                  modspec(3), modspec(4), modspec(5),
                  pl.BlockSpec((d, th), lambda i, j: (0, j)),
                  pl.BlockSpec((th, d), lambda i, j: (j, 0)),
                  pl.BlockSpec((1, d), lambda i, j: (0, 0))],
        out_specs=pl.BlockSpec((tm, d), lambda i, j: (i, 0)),
        scratch_shapes=[pltpu.VMEM((tm, d), BF16)],
        compiler_params=_cparams(("arbitrary", "arbitrary")),
        name="ffn_final" if final else "ffn",
    )(xs, gain, mod3, mod3, mod3, w1, w2, g_final)


def _rope_tables(seq):
    n_rows = seq // GRID_W
    half = HEAD_DIM // 2
    freqs = (1.0 / (ROPE_BASE ** (np.arange(0, half, 2, dtype=np.float32) / np.float32(half)))).astype(np.float32)
    ang_r = (np.arange(n_rows, dtype=np.float32)[:, None] * freqs).astype(np.float32).astype(np.float64)
    ang_c = (np.arange(GRID_W, dtype=np.float32)[:, None] * freqs).astype(np.float32).astype(np.float64)

    def full(fn, ident):
        r = jnp.repeat(jnp.asarray(fn(ang_r), F32), GRID_W, axis=0)
        c = jnp.tile(jnp.asarray(fn(ang_c), F32), (n_rows, 1))
        tab = jnp.concatenate([r, r, c, c], axis=-1)
        return jnp.concatenate([tab, jnp.full((ATT_TILE, HEAD_DIM), ident, F32)], axis=0)

    i = np.arange(HEAD_DIM)
    quarter = HEAD_DIM // 4
    prot = np.zeros((HEAD_DIM, HEAD_DIM), np.float32)
    first = (i % (2 * quarter)) < quarter
    prot[(i + quarter)[first], i[first]] = -1.0
    prot[(i - quarter)[~first], i[~first]] = 1.0
    return full(np.cos, 1.0), full(np.sin, 0.0), jnp.asarray(prot, BF16)


def _geometry(batch, seq, lc, d):
    tm = 1024 if seq % 1024 == 0 else 512
    tm_mix = 512
    assert seq % tm == 0 and seq % CHUNK == 0 and lc % CHUNK == 0 and (batch * seq) % lc == 0
    r_lat, r_ctx = batch * seq, batch * lc
    rows = r_lat + r_ctx
    assert rows % tm_mix == 0 and r_ctx % tm_mix == 0 and rows % CONV_TILE == 0
    starts = tuple(b * seq for b in range(batch)) + tuple(r_lat + b * lc for b in range(batch))
    ends = tuple((b + 1) * seq - 1 for b in range(batch)) + tuple(r_lat + (b + 1) * lc - 1 for b in range(batch))
    return dict(tm=tm, tm_mix=tm_mix, rows=rows, r_lat=r_lat, r_ctx=r_ctx, batch=batch, seq=seq, lc=lc,
                nb=seq // CHUNK, n_lat=r_lat // CHUNK,
                seg=lambda i: jnp.minimum(i // (seq // tm), batch),
                seg_mix=lambda i: jnp.minimum(i // (seq // tm_mix), batch),
                starts=starts, ends=ends)


def _permute_w_in(w):
    cols = [w[:, _SRC[n]:_SRC[n] + wd] for n, wd in _P_ORDER]
    return jnp.concatenate(cols, axis=1).astype(BF16)


def _gate_w(w):
    d = w.shape[0]
    mg = w[:, _SRC["mg"]:_SRC["mg"] + 4 * ML_HEADS].reshape(d, 2, 2, ML_HEADS)
    groups = []
    for t in range(2):
        part = mg[:, :, t, :].reshape(d, 2 * ML_HEADS)
        groups.append(jnp.pad(part, ((0, 0), (0, GATE_LANES - 2 * ML_HEADS))))
    groups.append(jnp.zeros((d, GATE_WIDTH - 2 * GATE_LANES), w.dtype))
    return jnp.concatenate(groups, axis=1).astype(BF16)


def _gate_bias(b_i, b_f):
    pad = jnp.zeros((GATE_LANES - 2 * ML_HEADS,), F32)
    col = jnp.concatenate([b_i.reshape(-1), pad, b_f.reshape(-1), pad])
    return jnp.broadcast_to(col[:, None], (2 * GATE_LANES, CHUNK))


def kernel(x, c, ctx, c_ctx, w_ada, b_ada, g_mix, g_ffn, w_in, gm_v_gain, gm_w_s, gm_b_s, attn_sink,
           ml_conv_w, ml_conv_b, ml_b_i, ml_b_f, ml_out_gain, w_out, w_ff1, w_ff2, g_final):
    batch, seq, d = x.shape
    lc = ctx.shape[1]
    depth = w_ada.shape[0]
    geom = _geometry(batch, seq, lc, d)
    rows = geom["rows"]

    cc = jnp.concatenate([c, c_ctx[None, :], jnp.zeros((8 - batch - 1, d), F32)], axis=0)
    mod = _ada_call(cc, w_ada, b_ada)
    mod3 = mod[:, :batch + 1, :].reshape(depth * (batch + 1) * 6, 1, d)

    cos_tab, sin_tab, prot = _rope_tables(seq)

    srcs = (x.reshape(-1, d), ctx.reshape(-1, d))
    for layer in range(depth):
        last = layer == depth - 1
        base = layer * (batch + 1) * 6
        w_p = _permute_w_in(w_in[layer])
        w_g = _gate_w(w_in[layer])
        p, gates, w_out_bf = _inproj_call(srcs, g_mix[layer][None, :], mod3, w_p, w_g, w_out, layer, geom, base)

        bs_full = jnp.repeat(gm_b_s[layer].T, HEAD_DIM, axis=1)
        ya = _gmlp_call(p, gm_v_gain[layer][None, :], gm_w_s[layer].astype(BF16), bs_full, 512)

        yb = _attn_call(p, attn_sink[layer], cos_tab, sin_tab, prot, geom)

        w8 = jnp.concatenate([ml_conv_w[layer], jnp.zeros((5, 2 * ML_WIDTH), F32)], axis=0)
        b8 = jnp.concatenate([ml_conv_b[layer][None, :], jnp.zeros((7, 2 * ML_WIDTH), F32)], axis=0)
        q_conv, kt_conv = _conv_call(p, w8, b8, geom)
        n_chunks = rows // CHUNK
        prep_chunks = max(c for c in range(1, 17) if n_chunks % c == 0)
        gp = _gate_prep_call(gates, _gate_bias(ml_b_i[layer], ml_b_f[layer]), prep_chunks * CHUNK)
        hf, hb = _mlstm_call(q_conv, kt_conv, p, gp, geom)

        n_mix = (geom["r_lat"] if last else rows) // geom["tm_mix"]
        xs, w_ff1_bf, w_ff2_bf = _mixout_call(srcs, ya, yb, hf, hb, p, ml_out_gain[layer][None, :], mod3, w_out_bf,
                                              (w_ff1, w_ff2), layer, geom, base, n_mix)
        xs = _ffn_call(xs, g_ffn[layer][None, :], mod3, w_ff1_bf, w_ff2_bf, g_final[None, :], geom, base, n_mix, last)
        srcs = (xs,)
    return xs.reshape(batch, seq, d)
```

```python
import functools
import math

import jax
import jax.numpy as jnp
import numpy as np
from jax import lax
from jax.experimental import pallas as pl
from jax.experimental.pallas import tpu as pltpu

F32 = jnp.float32
BF16 = jnp.bfloat16

HEAD_DIM = 128
CHUNK = 128
GM_GROUPS = 4
ATT_HEADS = 6
ATT_KV_HEADS = 2
ATT_GROUP = ATT_HEADS // ATT_KV_HEADS
ML_HEADS = 6
GM_WIDTH = GM_GROUPS * HEAD_DIM
ATT_WIDTH = ATT_HEADS * HEAD_DIM
ATT_KV_WIDTH = ATT_KV_HEADS * HEAD_DIM
ML_WIDTH = ML_HEADS * HEAD_DIM
MIX_WIDTH = GM_WIDTH + ATT_WIDTH + ML_WIDTH
GRID_W = 64
ROPE_BASE = 10000.0
EPS = 1e-6
NEG_BIG = -1e30
LOG2E = math.log2(math.e)
LN_INV_SCALE = 0.5 * math.log(HEAD_DIM)

ATT_TILE = 2 * CHUNK
CONV_TILE = 4 * CHUNK
GATE_LANES = 16
GP_ROWS = 6 * GATE_LANES

_SRC = dict(gu=0, gv=512, aq=1024, ak=1792, av=2048, mq=2304, mk=3072, mv=3840, mo=4608, mg=5376)
_P_ORDER = (("aq", 768), ("mq", 768), ("gu", 512), ("gv", 512), ("ak", 256), ("av", 256),
            ("mk", 768), ("mv", 768), ("mo", 768))
P_WIDTH = sum(w for _, w in _P_ORDER)
_P_OFF = {}
_o = 0
for _n, _w in _P_ORDER:
    _P_OFF[_n] = _o
    _o += _w
GATE_WIDTH = HEAD_DIM

VMEM_LIMIT = 56 * 1024 * 1024


def _cparams(sem):
    return pltpu.CompilerParams(dimension_semantics=sem, vmem_limit_bytes=VMEM_LIMIT)


def _any(preds):
    return functools.reduce(jnp.logical_or, preds)


def _ada_kernel(c_ref, w_ref, b_ref, o_ref):
    c = c_ref[...]
    s = (c * jax.nn.sigmoid(c)).astype(BF16)
    o_ref[...] = jnp.dot(s, w_ref[...].astype(BF16), preferred_element_type=F32) + b_ref[...]


def _ada_call(cc, w_ada, b_ada):
    depth, d, n = w_ada.shape
    tn = math.gcd(n, 1024)
    return pl.pallas_call(
        _ada_kernel,
        out_shape=jax.ShapeDtypeStruct((depth, cc.shape[0], n), F32),
        grid=(depth, n // tn),
        in_specs=[pl.BlockSpec((cc.shape[0], d), lambda l, j: (0, 0)),
                  pl.BlockSpec((None, d, tn), lambda l, j: (l, 0, j)),
                  pl.BlockSpec((None, 1, tn), lambda l, j: (l, 0, j))],
        out_specs=pl.BlockSpec((None, cc.shape[0], tn), lambda l, j: (l, 0, j)),
        compiler_params=_cparams(("arbitrary", "arbitrary")),
        name="ada_mod",
    )(cc, w_ada, b_ada.reshape(depth, 1, n))


def _norm_mod(x, gain, scale, shift):
    y = x * lax.rsqrt(jnp.mean(x * x, axis=-1, keepdims=True) + EPS) * gain
    return (y * (1.0 + scale) + shift).astype(BF16)


def _norm_mod_blocked(src_ref, dst_ref, gain, scale, shift, rows_per=256):
    n, d = src_ref.shape
    g2 = gain * (1.0 + scale)
    for r0 in range(0, n, rows_per):
        rs = slice(r0, r0 + rows_per)
        ss = jnp.zeros((rows_per, HEAD_DIM), F32)
        for c0 in range(0, d, HEAD_DIM):
            xb = src_ref[rs, c0:c0 + HEAD_DIM]
            ss = ss + xb * xb
        inv = lax.rsqrt(jnp.sum(ss, axis=-1, keepdims=True) * (1.0 / d) + EPS)
        for c0 in range(0, d, HEAD_DIM):
            cs = slice(c0, c0 + HEAD_DIM)
            dst_ref[rs, cs] = (src_ref[rs, cs] * inv * g2[:, cs] + shift[:, cs]).astype(BF16)


def _inproj_fill(src_ref, g_ref, sh_ref, sc_ref, wg_ref, gate_ref, h_scr):
    n = src_ref.shape[0]
    _norm_mod_blocked(src_ref, h_scr, g_ref[...], sc_ref[0], sh_ref[0])
    gate_ref[0:n, :] = jnp.dot(h_scr[0:n, :], wg_ref[...], preferred_element_type=F32)


def _side_cast(stack, layer, n_steps, step):
    _, r, c = stack.shape
    nblk = 1
    while nblk * 2 <= n_steps and r % (nblk * 2) == 0 and (r // (nblk * 2)) % 16 == 0:
        nblk *= 2
    blk = lambda *ids: jnp.minimum(step(*ids), nblk - 1)
    in_spec = pl.BlockSpec((None, r // nblk, c), lambda *ids: (layer, blk(*ids), 0))
    out_spec = pl.BlockSpec((r // nblk, c), lambda *ids: (blk(*ids), 0))
    return in_spec, out_spec, jax.ShapeDtypeStruct((r, c), BF16)


def _inproj_kernel(x_ref, g_ref, sh_ref, sc_ref, w_ref, wg_ref, side_ref, p_ref, gate_ref, side_out_ref, h_scr):
    @pl.when(pl.program_id(1) == 0)
    def _():
        _inproj_fill(x_ref, g_ref, sh_ref, sc_ref, wg_ref, gate_ref, h_scr)

    p_ref[...] = jnp.dot(h_scr[...], w_ref[...], preferred_element_type=F32).astype(BF16)
    side_out_ref[...] = side_ref[...].astype(BF16)


def _inproj_split_kernel(n_lat_tiles, x_ref, c_ref, g_ref, sh_ref, sc_ref, w_ref, wg_ref, side_ref,
                         p_ref, gate_ref, side_out_ref, h_scr):
    i, j = pl.program_id(0), pl.program_id(1)

    @pl.when(jnp.logical_and(j == 0, i < n_lat_tiles))
    def _():
        _inproj_fill(x_ref, g_ref, sh_ref, sc_ref, wg_ref, gate_ref, h_scr)

    @pl.when(jnp.logical_and(j == 0, i >= n_lat_tiles))
    def _():
        _inproj_fill(c_ref, g_ref, sh_ref, sc_ref, wg_ref, gate_ref, h_scr)

    p_ref[...] = jnp.dot(h_scr[...], w_ref[...], preferred_element_type=F32).astype(BF16)
    side_out_ref[...] = side_ref[...].astype(BF16)


def _inproj_call(srcs, gain, mod3, w, wg, side_stack, layer, geom, layer_mod_base):
    d = srcs[0].shape[1]
    rows, tm, seg = geom["rows"], geom["tm"], geom["seg"]
    tn = P_WIDTH // 3
    side_in, side_out, side_shape = _side_cast(side_stack, layer, pl.cdiv(rows, tm), lambda i, j: i)
    modspec = lambda k: pl.BlockSpec((1, 1, d), lambda i, j: (layer_mod_base + seg(i) * 6 + k, 0, 0))
    if len(srcs) == 1:
        body = _inproj_kernel
        src_specs = [pl.BlockSpec((tm, d), lambda i, j: (i, 0))]
    else:
        n_lat_tiles = geom["r_lat"] // tm
        assert geom["r_ctx"] <= tm
        body = functools.partial(_inproj_split_kernel, n_lat_tiles)
        src_specs = [pl.BlockSpec((tm, d), lambda i, j: (jnp.minimum(i, n_lat_tiles - 1), 0)),
                     pl.BlockSpec((geom["r_ctx"], d), lambda i, j: (0, 0))]
    return pl.pallas_call(
        body,
        out_shape=(jax.ShapeDtypeStruct((rows, P_WIDTH), BF16),
                   jax.ShapeDtypeStruct((rows, GATE_WIDTH), F32), side_shape),
        grid=(pl.cdiv(rows, tm), P_WIDTH // tn),
        in_specs=src_specs + [pl.BlockSpec((1, d), lambda i, j: (0, 0)), modspec(0), modspec(1),
                              pl.BlockSpec((d, tn), lambda i, j: (0, j)),
                              pl.BlockSpec((d, GATE_WIDTH), lambda i, j: (0, 0)), side_in],
        out_specs=(pl.BlockSpec((tm, tn), lambda i, j: (i, j)),
                   pl.BlockSpec((tm, GATE_WIDTH), lambda i, j: (i, 0)), side_out),
        scratch_shapes=[pltpu.VMEM((tm, d), BF16)],
        compiler_params=_cparams(("arbitrary", "arbitrary")),
        name="in_proj",
    )(*srcs, gain, mod3, mod3, w, wg, side_stack)


def _gelu(x):
    return jax.nn.gelu(x, approximate=True)


def _gmlp_kernel(gu_ref, gv_ref, gain_ref, ws_ref, bs_ref, o_ref):
    units = [(c, g) for c in range(gu_ref.shape[0] // CHUNK) for g in range(GM_GROUPS)]
    rows = lambda c: slice(c * CHUNK, (c + 1) * CHUNK)
    cols = lambda g: slice(g * HEAD_DIM, (g + 1) * HEAD_DIM)
    vns = []
    for c, g in units:
        vg = _gelu(gv_ref[rows(c), cols(g)].astype(F32))
        vn = vg * lax.rsqrt(jnp.mean(vg * vg, axis=-1, keepdims=True) + EPS) * gain_ref[:, cols(g)]
        vns.append(vn.astype(BF16))
    ss = [jnp.dot(ws_ref[g], vn, preferred_element_type=F32) for (c, g), vn in zip(units, vns)]
    for (c, g), s in zip(units, ss):
        u = _gelu(gu_ref[rows(c), cols(g)].astype(F32))
        o_ref[rows(c), cols(g)] = (u * (s + bs_ref[:, cols(g)])).astype(BF16)


def _gmlp_call(p, v_gain, ws, bs_full, tr):
    rows = p.shape[0]
    gu_blk = _P_OFF["gu"] // GM_WIDTH
    gv_blk = _P_OFF["gv"] // GM_WIDTH
    return pl.pallas_call(
        _gmlp_kernel,
        out_shape=jax.ShapeDtypeStruct((rows, GM_WIDTH), BF16),
        grid=(rows // tr,),
        in_specs=[pl.BlockSpec((tr, GM_WIDTH), lambda i: (i, gu_blk)),
                  pl.BlockSpec((tr, GM_WIDTH), lambda i: (i, gv_blk)),
                  pl.BlockSpec((1, GM_WIDTH), lambda i: (0, 0)),
                  pl.BlockSpec((GM_GROUPS, CHUNK, CHUNK), lambda i: (0, 0, 0)),
                  pl.BlockSpec((CHUNK, GM_WIDTH), lambda i: (0, 0))],
        out_specs=pl.BlockSpec((tr, GM_WIDTH), lambda i: (i, 0)),
        compiler_params=_cparams(("arbitrary",)),
        name="gmlp",
    )(p, p, v_gain, ws, bs_full)


def _rope(x_bf, cos, sin, prot):
    rot = jnp.dot(x_bf, prot, preferred_element_type=F32)
    return (x_bf.astype(F32) * cos + rot * sin).astype(BF16)


def _attn_kernel(nt_seq, n_lat_tiles, sink_ref, q_ref, kp_ref, km_ref, kn_ref, vp_ref, vm_ref, vn_ref,
                 kx_ref, vx_ref, cp_ref, sp_ref, cm_ref, sm_ref, cn_ref, sn_ref, prot_ref, o_ref):
    t = pl.program_id(0)
    is_lat = t < n_lat_tiles
    pos = t % nt_seq
    has_cur = is_lat.astype(jnp.int32)
    has_prev = jnp.logical_and(is_lat, pos != 0).astype(jnp.int32)
    has_next = jnp.logical_and(is_lat, pos != nt_seq - 1).astype(jnp.int32)
    prot = prot_ref[...]
    nq = ATT_TILE // CHUNK
    q_scale = HEAD_DIM ** -0.5 * LOG2E

    qi = lax.broadcasted_iota(jnp.int32, (ATT_GROUP * CHUNK, 3 * CHUNK), 0) & (CHUNK - 1)
    kj = lax.broadcasted_iota(jnp.int32, (ATT_GROUP * CHUNK, 3 * CHUNK), 1)
    band = jnp.where(jnp.abs(kj - CHUNK - qi) <= CHUNK, 1, 0)
    hrow = lax.broadcasted_iota(jnp.int32, (ATT_GROUP * CHUNK, 1), 0)

    cos_k = jnp.concatenate([cp_ref[...], cm_ref[...], cn_ref[...]], axis=0)
    sin_k = jnp.concatenate([sp_ref[...], sm_ref[...], sn_ref[...]], axis=0)
    cos_q, sin_q = cm_ref[...] * q_scale, sm_ref[...] * q_scale
    head = lambda h: slice(h * HEAD_DIM, (h + 1) * HEAD_DIM)

    masks = []
    for a in range(nq):
        flags = [has_prev if a + j == 0 else (has_next if a + j == nq + 1 else has_cur) for j in range(3)]
        valid = band * jnp.where(kj < CHUNK, flags[0], jnp.where(kj < 2 * CHUNK, flags[1], flags[2])) > 0
        masks.append(jnp.where(valid, 0.0, NEG_BIG))

    k_rot = [_rope(jnp.concatenate([kp_ref[:, head(g)], km_ref[:, head(g)], kn_ref[:, head(g)]], axis=0),
                   cos_k, sin_k, prot) for g in range(ATT_KV_HEADS)]
    v_loc = [jnp.concatenate([vp_ref[:, head(g)], vm_ref[:, head(g)], vn_ref[:, head(g)]], axis=0)
             for g in range(ATT_KV_HEADS)]
    sks = [jnp.where(hrow < CHUNK, sink_ref[g * ATT_GROUP],
                     jnp.where(hrow < 2 * CHUNK, sink_ref[g * ATT_GROUP + 1], sink_ref[g * ATT_GROUP + 2])) * LOG2E
           for g in range(ATT_KV_HEADS)]
    units = [(g, a) for g in range(ATT_KV_HEADS) for a in range(nq)]
    rows = lambda a: slice(a * CHUNK, (a + 1) * CHUNK)
    q3s = [jnp.concatenate([_rope(q_ref[rows(a), head(g * ATT_GROUP + h)], cos_q[rows(a)], sin_q[rows(a)], prot)
                            for h in range(ATT_GROUP)], axis=0) for g, a in units]
    ss = [lax.dot_general(q3, jnp.concatenate([k_rot[g][a * CHUNK:(a + 3) * CHUNK], kx_ref[:, head(g)]], axis=0),
                          (((1,), (1,)), ((), ())), preferred_element_type=F32)
          for (g, a), q3 in zip(units, q3s)]
    ps, p_sinks = [], []
    for (g, a), s in zip(units, ss):
        s_loc = s[:, :3 * CHUNK] + masks[a]
        s_ctx = s[:, 3 * CHUNK:]
        m = jnp.maximum(jnp.maximum(jnp.max(s_loc, axis=1, keepdims=True),
                                    jnp.max(s_ctx, axis=1, keepdims=True)), sks[g])
        p_sinks.append(jnp.exp2(sks[g] - m))
        ps.append(jnp.concatenate([jnp.exp2(s_loc - m), jnp.exp2(s_ctx - m)], axis=1).astype(BF16))
    ones_bf = jnp.ones((3 * CHUNK + kx_ref.shape[0], HEAD_DIM), BF16)
    for (g, a), p_all, p_sink in zip(units, ps, p_sinks):
        v_all = jnp.concatenate([v_loc[g][a * CHUNK:(a + 3) * CHUNK], vx_ref[:, head(g)]], axis=0)
        r = jnp.dot(p_all, jnp.concatenate([v_all, ones_bf], axis=1), preferred_element_type=F32)
        o = r[:, :HEAD_DIM] / (r[:, HEAD_DIM:] + p_sink)
        for h in range(ATT_GROUP):
            o_ref[rows(a), head(g * ATT_GROUP + h)] = o[h * CHUNK:(h + 1) * CHUNK, :].astype(BF16)


def _attn_call(p, sink, cos_tab, sin_tab, prot, geom):
    rows = p.shape[0]
    seq, lc, r_lat = geom["seq"], geom["lc"], geom["r_lat"]
    assert seq % ATT_TILE == 0 and lc % ATT_TILE == 0
    nt_seq, n_lat_tiles, nq = seq // ATT_TILE, r_lat // ATT_TILE, ATT_TILE // CHUNK
    ak_blk = _P_OFF["ak"] // ATT_KV_WIDTH
    av_blk = _P_OFF["av"] // ATT_KV_WIDTH
    ctx_base = r_lat // lc

    def is_lat(t):
        return t < n_lat_tiles

    def prev_ok(t):
        return jnp.logical_and(is_lat(t), t % nt_seq != 0)

    def next_ok(t):
        return jnp.logical_and(is_lat(t), t % nt_seq != nt_seq - 1)

    def prev_i(t):
        return jnp.where(prev_ok(t), nq * t - 1, nq * t)

    def next_i(t):
        return jnp.where(next_ok(t), nq * t + nq, nq * t + nq - 1)

    def tab_main(t):
        return jnp.where(is_lat(t), t % nt_seq, nt_seq)

    def tab_prev(t):
        return jnp.where(prev_ok(t), nq * (t % nt_seq) - 1, nq * nt_seq)

    def tab_next(t):
        return jnp.where(next_ok(t), nq * (t % nt_seq) + nq, nq * nt_seq)

    def ctx_i(t):
        return ctx_base + jnp.where(is_lat(t), t // nt_seq, (t - n_lat_tiles) // (lc // ATT_TILE))

    halo = lambda f, blk: pl.BlockSpec((CHUNK, ATT_KV_WIDTH), lambda t: (f(t), blk))
    main = lambda blk: pl.BlockSpec((ATT_TILE, ATT_KV_WIDTH), lambda t: (t, blk))
    tb = lambda rws, f: pl.BlockSpec((rws, HEAD_DIM), lambda t: (f(t), 0))
    return pl.pallas_call(
        functools.partial(_attn_kernel, nt_seq, n_lat_tiles),
        out_shape=jax.ShapeDtypeStruct((rows, ATT_WIDTH), BF16),
        grid=(rows // ATT_TILE,),
        in_specs=[pl.BlockSpec(memory_space=pltpu.SMEM),
                  pl.BlockSpec((ATT_TILE, ATT_WIDTH), lambda t: (t, 0)),
                  halo(prev_i, ak_blk), main(ak_blk), halo(next_i, ak_blk),
                  halo(prev_i, av_blk), main(av_blk), halo(next_i, av_blk),
                  pl.BlockSpec((lc, ATT_KV_WIDTH), lambda t: (ctx_i(t), ak_blk)),
                  pl.BlockSpec((lc, ATT_KV_WIDTH), lambda t: (ctx_i(t), av_blk)),
                  tb(CHUNK, tab_prev), tb(CHUNK, tab_prev), tb(ATT_TILE, tab_main), tb(ATT_TILE, tab_main),
                  tb(CHUNK, tab_next), tb(CHUNK, tab_next),
                  pl.BlockSpec((HEAD_DIM, HEAD_DIM), lambda t: (0, 0))],
        out_specs=pl.BlockSpec((ATT_TILE, ATT_WIDTH), lambda t: (t, 0)),
        compiler_params=_cparams(("arbitrary",)),
        name="window_attn",
    )(sink, p, p, p, p, p, p, p, p, p, cos_tab, sin_tab, cos_tab, sin_tab, cos_tab, sin_tab, prot)


def _conv_kernel(start_chunks, end_chunks, q_ref, qp_ref, qn_ref, k_ref, kp_ref, kn_ref, w_ref, b_ref, sh_ref,
                 oq_ref, okt_ref):
    nc = q_ref.shape[0] // CHUNK
    i = pl.program_id(0)
    srcs = ((q_ref, qp_ref, qn_ref), (k_ref, kp_ref, kn_ref))
    units = [(part, c) for part in range(2) for c in range(nc)]
    shifted = []
    for part, c in units:
        x_ref, xp_ref, xn_ref = srcs[part]
        cg = i * nc + c
        at_start = _any([cg == s for s in start_chunks])
        at_end = _any([cg == e for e in end_chunks])
        cur = x_ref[c * CHUNK:(c + 1) * CHUNK, :]
        prv = xp_ref[...] if c == 0 else x_ref[(c - 1) * CHUNK:c * CHUNK, :]
        nxt = xn_ref[...] if c == nc - 1 else x_ref[(c + 1) * CHUNK:(c + 2) * CHUNK, :]
        s_prev = sh_ref[jnp.where(at_start, 1, 0)]
        s_next = sh_ref[jnp.where(at_end, 3, 2)]
        shifted.append((jnp.dot(s_prev, jnp.concatenate([prv, cur], axis=0), preferred_element_type=F32),
                        jnp.dot(s_next, jnp.concatenate([cur, nxt], axis=0), preferred_element_type=F32)))
    for (part, c), (x_prev, x_next) in zip(units, shifted):
        cs = slice(part * ML_WIDTH, (part + 1) * ML_WIDTH)
        cur = srcs[part][0][c * CHUNK:(c + 1) * CHUNK, :].astype(F32)
        y = b_ref[0:1, cs] + x_prev * w_ref[0:1, cs] + cur * w_ref[1:2, cs] + x_next * w_ref[2:3, cs]
        hy = 0.5 * y
        y = hy + hy * jnp.tanh(hy)
        if part == 0:
            oq_ref[c * CHUNK:(c + 1) * CHUNK, :] = y.astype(BF16)
        else:
            for h in range(ML_HEADS):
                r0 = (c * ML_HEADS + h) * HEAD_DIM
                okt_ref[r0:r0 + HEAD_DIM, :] = y[:, h * HEAD_DIM:(h + 1) * HEAD_DIM].T.astype(BF16)


def _shift_matrices():
    t = np.arange(CHUNK)
    prev = np.zeros((CHUNK, 2 * CHUNK), np.float32)
    prev[t, t + CHUNK - 1] = 1.0
    nxt = np.zeros((CHUNK, 2 * CHUNK), np.float32)
    nxt[t, t + 1] = 1.0
    prev_cut, nxt_cut = prev.copy(), nxt.copy()
    prev_cut[0] = 0.0
    nxt_cut[CHUNK - 1] = 0.0
    return jnp.asarray(np.stack([prev, prev_cut, nxt, nxt_cut]), BF16)


def _conv_call(p, w8, b8, geom):
    rows = p.shape[0]
    tr = CONV_TILE
    nc = tr // CHUNK
    mq_blk = _P_OFF["mq"] // ML_WIDTH
    mk_blk = _P_OFF["mk"] // ML_WIDTH
    start_chunks = tuple(s // CHUNK for s in geom["starts"])
    end_chunks = tuple(e // CHUNK for e in geom["ends"])

    def trio(blk):
        return [pl.BlockSpec((tr, ML_WIDTH), lambda i: (i, blk)),
                pl.BlockSpec((CHUNK, ML_WIDTH), lambda i: (jnp.maximum(i * nc - 1, 0), blk)),
                pl.BlockSpec((CHUNK, ML_WIDTH), lambda i: (jnp.minimum((i + 1) * nc, rows // CHUNK - 1), blk))]

    return pl.pallas_call(
        functools.partial(_conv_kernel, start_chunks, end_chunks),
        out_shape=(jax.ShapeDtypeStruct((rows, ML_WIDTH), BF16),
                   jax.ShapeDtypeStruct((rows * ML_HEADS, CHUNK), BF16)),
        grid=(rows // tr,),
        in_specs=trio(mq_blk) + trio(mk_blk) + [
            pl.BlockSpec((8, 2 * ML_WIDTH), lambda i: (0, 0)),
            pl.BlockSpec((8, 2 * ML_WIDTH), lambda i: (0, 0)),
            pl.BlockSpec((4, CHUNK, 2 * CHUNK), lambda i: (0, 0, 0))],
        out_specs=(pl.BlockSpec((tr, ML_WIDTH), lambda i: (i, 0)),
                   pl.BlockSpec((tr * ML_HEADS, CHUNK), lambda i: (i, 0))),
        compiler_params=_cparams(("arbitrary",)),
        name="short_conv",
    )(p, p, p, p, p, p, w8, b8, _shift_matrices())


def _split3(x):
    hi = x.astype(BF16)
    r1 = x - hi.astype(F32)
    mid = r1.astype(BF16)
    lo = (r1 - mid.astype(F32)).astype(BF16)
    return hi, mid, lo


def _cummax_lanes(xs, reverses):
    n = xs[0].shape[1]
    lane = lax.broadcasted_iota(jnp.int32, xs[0].shape, 1)
    sh = 1
    while sh < n:
        xs = [jnp.maximum(x, jnp.where(lane < n - sh, pltpu.roll(x, n - sh, 1), NEG_BIG)) if rev else
              jnp.maximum(x, jnp.where(lane >= sh, pltpu.roll(x, sh, 1), NEG_BIG)) for x, rev in zip(xs, reverses)]
        sh *= 2
    return xs


def _gate_prep_kernel(g_ref, bias_ref, o_ref):
    gl = GATE_LANES
    row = lax.broadcasted_iota(jnp.int32, (CHUNK, CHUNK), 0)
    col = lax.broadcasted_iota(jnp.int32, (CHUNK, CHUNK), 1)
    chunks = range(g_ref.shape[0] // CHUNK)
    pres = [g_ref[c * CHUNK:(c + 1) * CHUNK, :].T[0:2 * gl, :] + bias_ref[...] for c in chunks]
    lis = [pre[0:gl] for pre in pres]
    lf3s = []
    for pre in pres:
        pf = pre[gl:2 * gl]
        lf3s.append(jnp.concatenate(_split3(jnp.minimum(pf, 0.0) - jnp.log1p(jnp.exp(-jnp.abs(pf)))), axis=0))
    units = [(c, d) for c in chunks for d in range(2)]
    csum = [((row <= col) if d == 0 else (row >= col)).astype(BF16) for d in range(2)]
    cs3s = [jnp.dot(lf3s[c], csum[d], preferred_element_type=F32) for c, d in units]
    bs = [cs3[0:gl] + cs3[gl:2 * gl] + cs3[2 * gl:3 * gl] for cs3 in cs3s]
    us = [lis[c] - b for (c, d), b in zip(units, bs)]
    umaxs = _cummax_lanes(us, [d == 1 for c, d in units])
    for (c, d), b, u, umax in zip(units, bs, us, umaxs):
        g_rep = jnp.broadcast_to(b[:, CHUNK - 1:CHUNK] if d == 0 else b[:, 0:1], b.shape)
        a = g_rep - b + lis[c]
        mloc = jnp.broadcast_to(jnp.max(a, axis=1, keepdims=True), b.shape)
        r0 = (c * 2 + d) * GP_ROWS
        o_ref[r0:r0 + GP_ROWS, :] = jnp.concatenate([b, u, umax, jnp.exp(a - mloc), g_rep, mloc], axis=0)


def _gate_prep_call(gates, bias, tr):
    rows = gates.shape[0]
    nc = tr // CHUNK
    return pl.pallas_call(
        _gate_prep_kernel,
        out_shape=jax.ShapeDtypeStruct((rows // CHUNK * 2 * GP_ROWS, CHUNK), F32),
        grid=(rows // tr,),
        in_specs=[pl.BlockSpec((tr, GATE_WIDTH), lambda i: (i, 0)),
                  pl.BlockSpec((2 * GATE_LANES, CHUNK), lambda i: (0, 0))],
        out_specs=pl.BlockSpec((nc * 2 * GP_ROWS, CHUNK), lambda i: (i, 0)),
        compiler_params=_cparams(("arbitrary",)),
        name="gate_prep",
    )(gates, bias)


def _mlstm_kernel(qf_ref, kf_ref, vf_ref, gf_ref, qb_ref, kb_ref, vb_ref, gb_ref, hf_ref, hb_ref, ct_scr, m_scr):
    @pl.when(pl.program_id(1) == 0)
    def _():
        ct_scr[...] = jnp.zeros_like(ct_scr)
        m_scr[...] = jnp.zeros_like(m_scr)

    row = lax.broadcasted_iota(jnp.int32, (CHUNK, CHUNK), 0)
    col = lax.broadcasted_iota(jnp.int32, (CHUNK, CHUNK), 1)
    ones_bf = jnp.ones((CHUNK, HEAD_DIM), BF16)
    gl = GATE_LANES
    dirs = ((qf_ref, kf_ref, vf_ref, gf_ref, hf_ref), (qb_ref, kb_ref, vb_ref, gb_ref, hb_ref))
    prep = []
    for d, (_, _, _, g_ref, _) in enumerate(dirs):
        b, u, umax, w, g_rep, mloc = (g_ref[k * gl:(k + 1) * gl, :] for k in range(6))
        m0 = m_scr[d]
        mx = jnp.maximum(m0, umax)
        em = jnp.exp(LN_INV_SCALE - b - mx)
        m_new = jnp.maximum(g_rep + m0, mloc)
        a_old = jnp.exp(g_rep + m0 - m_new)
        w_new = w * jnp.exp(mloc - m_new)
        m_scr[d] = m_new
        cols = jnp.concatenate([mx * -LOG2E, em, jnp.zeros((CHUNK - 2 * gl, CHUNK), F32)], axis=0).T
        prep.append((u * LOG2E, w_new, m0 * LOG2E, a_old, cols))
    units = [(d, h) for d in range(2) for h in range(ML_HEADS)]
    head = lambda h: slice(h * HEAD_DIM, (h + 1) * HEAD_DIM)
    qks = [jnp.dot(dirs[d][0][:, head(h)], dirs[d][1][head(h), :], preferred_element_type=F32) for d, h in units]
    s_exts, em_bs = [], []
    for (d, h), qk in zip(units, qks):
        order = (col <= row) if d == 0 else (col >= row)
        u2, _, m02, _, cols = prep[d]
        j = d * ML_HEADS + h
        c_b = jnp.broadcast_to(cols[:, j:j + 1], (CHUNK, CHUNK))
        em_bs.append(jnp.broadcast_to(cols[:, gl + j:gl + j + 1], (CHUNK, CHUNK)))
        d_in = jnp.where(order, c_b + u2[j:j + 1, :], NEG_BIG)
        d_st = c_b + m02[j:j + 1, :]
        qh = dirs[d][0][:, head(h)]
        s_exts.append(jnp.concatenate([qk * jnp.exp2(d_in), qh.astype(F32) * jnp.exp2(d_st)], axis=1).astype(BF16))
    for (d, h), s_ext, em_b in zip(units, s_exts, em_bs):
        j = d * ML_HEADS + h
        v_aug = jnp.concatenate([dirs[d][2][:, head(h)], ones_bf], axis=1)
        rhs = jnp.concatenate([v_aug, ct_scr[j].astype(BF16)], axis=0)
        r = jnp.dot(s_ext, rhs, preferred_element_type=F32)
        hh = r[:, :HEAD_DIM] / jnp.maximum(jnp.abs(r[:, HEAD_DIM:]), em_b)
        dirs[d][4][:, head(h)] = hh.astype(BF16)
    for d, h in units:
        _, w_new, _, a_old, _ = prep[d]
        j = d * ML_HEADS + h
        v_aug = jnp.concatenate([dirs[d][2][:, head(h)], ones_bf], axis=1)
        ktw = (dirs[d][1][head(h), :].astype(F32) * w_new[j:j + 1, :]).astype(BF16)
        dct = jnp.dot(ktw, v_aug, preferred_element_type=F32)
        ao = jnp.concatenate([a_old[j:j + 1, :], a_old[j:j + 1, :]], axis=1)
        ct_scr[j] = ao * ct_scr[j] + dct


def _mlstm_call(q, kt, p, gp, geom):
    rows = p.shape[0]
    nb, n_lat, lc, batch = geom["nb"], geom["n_lat"], geom["lc"], geom["batch"]
    ncs = lc // CHUNK
    mv_blk = _P_OFF["mv"] // ML_WIDTH

    def fwd(b, s):
        return jnp.where(s < ncs, n_lat + b * ncs + s, b * nb + (s - ncs))

    def bwd(b, s):
        return jnp.where(s < ncs, n_lat + b * ncs + (ncs - 1 - s), b * nb + (nb - 1 - (s - ncs)))

    def specs(f, d):
        return [pl.BlockSpec((CHUNK, ML_WIDTH), lambda b, s: (f(b, s), 0)),
                pl.BlockSpec((ML_WIDTH, CHUNK), lambda b, s: (f(b, s), 0)),
                pl.BlockSpec((CHUNK, ML_WIDTH), lambda b, s: (f(b, s), mv_blk)),
                pl.BlockSpec((GP_ROWS, CHUNK), lambda b, s: (2 * f(b, s) + d, 0))]

    out = jax.ShapeDtypeStruct((rows, ML_WIDTH), BF16)
    return pl.pallas_call(
        _mlstm_kernel,
        out_shape=(out, out),
        grid=(batch, ncs + nb),
        in_specs=specs(fwd, 0) + specs(bwd, 1),
        out_specs=(pl.BlockSpec((CHUNK, ML_WIDTH), lambda b, s: (fwd(b, s), 0)),
                   pl.BlockSpec((CHUNK, ML_WIDTH), lambda b, s: (bwd(b, s), 0))),
        scratch_shapes=[pltpu.VMEM((2 * ML_HEADS, HEAD_DIM, 2 * HEAD_DIM), F32),
                        pltpu.VMEM((2, GATE_LANES, CHUNK), F32)],
        compiler_params=_cparams(("arbitrary", "arbitrary")),
        name="mlstm",
    )(q, kt, p, gp, q, kt, p, gp)


def _mix_update(ya_ref, yb_ref, hf_ref, hb_ref, mo_ref, og_ref, gt_ref, w_ref):
    hs = hf_ref[...].astype(F32) + hb_ref[...].astype(F32)
    parts = [ya_ref[...], yb_ref[...]]
    for h in range(ML_HEADS):
        cs = slice(h * HEAD_DIM, (h + 1) * HEAD_DIM)
        xh = hs[:, cs]
        y = xh * lax.rsqrt(jnp.mean(xh * xh, axis=-1, keepdims=True) + EPS) * og_ref[:, cs]
        parts.append((y * jax.nn.sigmoid(mo_ref[:, cs].astype(F32))).astype(BF16))
    lhs = jnp.concatenate(parts, axis=1)
    return gt_ref[0] * jnp.dot(lhs, w_ref[...], preferred_element_type=F32)


def _mixout_kernel(x_ref, ya_ref, yb_ref, hf_ref, hb_ref, mo_ref, og_ref, gt_ref, w_ref, s1_ref, s2_ref,
                   o_ref, s1_out_ref, s2_out_ref):
    o_ref[...] = x_ref[...] + _mix_update(ya_ref, yb_ref, hf_ref, hb_ref, mo_ref, og_ref, gt_ref, w_ref)
    s1_out_ref[...] = s1_ref[...].astype(BF16)
    s2_out_ref[...] = s2_ref[...].astype(BF16)


def _mixout_split_kernel(n_lat_tiles, x_ref, c_ref, ya_ref, yb_ref, hf_ref, hb_ref, mo_ref, og_ref, gt_ref, w_ref,
                         s1_ref, s2_ref, o_ref, s1_out_ref, s2_out_ref):
    upd = _mix_update(ya_ref, yb_ref, hf_ref, hb_ref, mo_ref, og_ref, gt_ref, w_ref)
    is_ctx = jnp.full((x_ref.shape[0], 1), pl.program_id(0), jnp.int32) >= n_lat_tiles
    o_ref[...] = jnp.where(is_ctx, c_ref[...], x_ref[...]) + upd
    s1_out_ref[...] = s1_ref[...].astype(BF16)
    s2_out_ref[...] = s2_ref[...].astype(BF16)


def _mixout_call(srcs, ya, yb, hf, hb, p, out_gain, mod3, w_out, side_stacks, layer, geom, layer_mod_base, n_tiles):
    d = srcs[0].shape[1]
    rows, tm, seg = geom["rows"], geom["tm_mix"], geom["seg_mix"]
    mo_blk = _P_OFF["mo"] // ML_WIDTH
    row = lambda w, blk=0: pl.BlockSpec((tm, w), lambda i: (i, blk))
    side = [_side_cast(stack, layer, n_tiles, lambda i: i) for stack in side_stacks]
    if len(srcs) == 1:
        body, src_specs, alias = _mixout_kernel, [row(d)], {0: 0}
    else:
        n_lat_tiles = geom["r_lat"] // tm
        body = functools.partial(_mixout_split_kernel, n_lat_tiles)
        src_specs = [pl.BlockSpec((tm, d), lambda i: (jnp.minimum(i, n_lat_tiles - 1), 0)),
                     pl.BlockSpec((tm, d), lambda i: (jnp.maximum(i - n_lat_tiles, 0), 0))]
        alias = {}
    return pl.pallas_call(
        body,
        out_shape=(jax.ShapeDtypeStruct((rows, d), F32), side[0][2], side[1][2]),
        grid=(n_tiles,),
        in_specs=src_specs + [row(GM_WIDTH), row(ATT_WIDTH), row(ML_WIDTH), row(ML_WIDTH), row(ML_WIDTH, mo_blk),
                              pl.BlockSpec((1, ML_WIDTH), lambda i: (0, 0)),
                              pl.BlockSpec((1, 1, d), lambda i: (layer_mod_base + seg(i) * 6 + 2, 0, 0)),
                              pl.BlockSpec((MIX_WIDTH, d), lambda i: (0, 0)), side[0][0], side[1][0]],
        out_specs=(row(d), side[0][1], side[1][1]),
        input_output_aliases=alias,
        compiler_params=_cparams(("arbitrary",)),
        name="mix_out",
    )(*srcs, ya, yb, hf, hb, p, out_gain, mod3, w_out, *side_stacks)


def _ffn_kernel(final, x_ref, g_ref, sh_ref, sc_ref, gt_ref, w1_ref, w2_ref, gf_ref, o_ref, h_scr):
    j = pl.program_id(1)

    @pl.when(j == 0)
    def _():
        _norm_mod_blocked(x_ref, h_scr, g_ref[...], sc_ref[0], sh_ref[0])
        o_ref[...] = jnp.zeros_like(o_ref)

    a = jnp.maximum(jnp.dot(h_scr[...], w1_ref[...], preferred_element_type=F32), 0.0)
    o_ref[...] += jnp.dot((a * a).astype(BF16), w2_ref[...], preferred_element_type=F32)

    @pl.when(j == pl.num_programs(1) - 1)
    def _():
        xo = x_ref[...] + gt_ref[0] * o_ref[...]
        if final:
            xo = xo * lax.rsqrt(jnp.mean(xo * xo, axis=-1, keepdims=True) + EPS) * gf_ref[...]
        o_ref[...] = xo


def _ffn_call(xs, gain, mod3, w1, w2, g_final, geom, layer_mod_base, n_tiles, final):
    rows, d = xs.shape
    hidden = w1.shape[1]
    tm, seg = geom["tm_mix"], geom["seg_mix"]
    th = min(1024, hidden)
    out_rows = n_tiles * tm if final else rows
    modspec = lambda k: pl.BlockSpec((1, 1, d), lambda i, j: (layer_mod_base + seg(i) * 6 + k, 0, 0))
    return pl.pallas_call(
        functools.partial(_ffn_kernel, final),
        out_shape=jax.ShapeDtypeStruct((out_rows, d), F32),
        grid=(n_tiles, hidden // th),
        in_specs=[pl.BlockSpec((tm, d), lambda i, j: (i, 0)),
                  pl.BlockSpec((1, d), lambda i, j: (0, 0)),
                  modspec(3), modspec(4), modspec(5),
                  pl.BlockSpec((d, th), lambda i, j: (0, j)),
                  pl.BlockSpec((th, d), lambda i, j: (j, 0)),
                  pl.BlockSpec((1, d), lambda i, j: (0, 0))],
        out_specs=pl.BlockSpec((tm, d), lambda i, j: (i, 0)),
        scratch_shapes=[pltpu.VMEM((tm, d), BF16)],
        compiler_params=_cparams(("arbitrary", "arbitrary")),
        name="ffn_final" if final else "ffn",
    )(xs, gain, mod3, mod3, mod3, w1, w2, g_final)


def _rope_tables(seq):
    n_rows = seq // GRID_W
    half = HEAD_DIM // 2
    freqs = (1.0 / (ROPE_BASE ** (np.arange(0, half, 2, dtype=np.float32) / np.float32(half)))).astype(np.float32)
    ang_r = (np.arange(n_rows, dtype=np.float32)[:, None] * freqs).astype(np.float32).astype(np.float64)
    ang_c = (np.arange(GRID_W, dtype=np.float32)[:, None] * freqs).astype(np.float32).astype(np.float64)

    def full(fn, ident):
        r = jnp.repeat(jnp.asarray(fn(ang_r), F32), GRID_W, axis=0)
        c = jnp.tile(jnp.asarray(fn(ang_c), F32), (n_rows, 1))
        tab = jnp.concatenate([r, r, c, c], axis=-1)
        return jnp.concatenate([tab, jnp.full((ATT_TILE, HEAD_DIM), ident, F32)], axis=0)

    i = np.arange(HEAD_DIM)
    quarter = HEAD_DIM // 4
    prot = np.zeros((HEAD_DIM, HEAD_DIM), np.float32)
    first = (i % (2 * quarter)) < quarter
    prot[(i + quarter)[first], i[first]] = -1.0
    prot[(i - quarter)[~first], i[~first]] = 1.0
    return full(np.cos, 1.0), full(np.sin, 0.0), jnp.asarray(prot, BF16)


def _geometry(batch, seq, lc, d):
    tm = 1024 if seq % 1024 == 0 else 512
    tm_mix = 512
    assert seq % tm == 0 and seq % CHUNK == 0 and lc % CHUNK == 0 and (batch * seq) % lc == 0
    r_lat, r_ctx = batch * seq, batch * lc
    rows = r_lat + r_ctx
    assert rows % tm_mix == 0 and r_ctx % tm_mix == 0 and rows % CONV_TILE == 0
    starts = tuple(b * seq for b in range(batch)) + tuple(r_lat + b * lc for b in range(batch))
    ends = tuple((b + 1) * seq - 1 for b in range(batch)) + tuple(r_lat + (b + 1) * lc - 1 for b in range(batch))
    return dict(tm=tm, tm_mix=tm_mix, rows=rows, r_lat=r_lat, r_ctx=r_ctx, batch=batch, seq=seq, lc=lc,
                nb=seq // CHUNK, n_lat=r_lat // CHUNK,
                seg=lambda i: jnp.minimum(i // (seq // tm), batch),
                seg_mix=lambda i: jnp.minimum(i // (seq // tm_mix), batch),
                starts=starts, ends=ends)


def _permute_w_in(w):
    cols = [w[:, _SRC[n]:_SRC[n] + wd] for n, wd in _P_ORDER]
    return jnp.concatenate(cols, axis=1).astype(BF16)


def _gate_w(w):
    d = w.shape[0]
    mg = w[:, _SRC["mg"]:_SRC["mg"] + 4 * ML_HEADS].reshape(d, 2, 2, ML_HEADS)
    groups = []
    for t in range(2):
        part = mg[:, :, t, :].reshape(d, 2 * ML_HEADS)
        groups.append(jnp.pad(part, ((0, 0), (0, GATE_LANES - 2 * ML_HEADS))))
    groups.append(jnp.zeros((d, GATE_WIDTH - 2 * GATE_LANES), w.dtype))
    return jnp.concatenate(groups, axis=1).astype(BF16)


def _gate_bias(b_i, b_f):
    pad = jnp.zeros((GATE_LANES - 2 * ML_HEADS,), F32)
    col = jnp.concatenate([b_i.reshape(-1), pad, b_f.reshape(-1), pad])
    return jnp.broadcast_to(col[:, None], (2 * GATE_LANES, CHUNK))


def kernel(x, c, ctx, c_ctx, w_ada, b_ada, g_mix, g_ffn, w_in, gm_v_gain, gm_w_s, gm_b_s, attn_sink,
           ml_conv_w, ml_conv_b, ml_b_i, ml_b_f, ml_out_gain, w_out, w_ff1, w_ff2, g_final):
    batch, seq, d = x.shape
    lc = ctx.shape[1]
    depth = w_ada.shape[0]
    geom = _geometry(batch, seq, lc, d)
    rows = geom["rows"]

    cc = jnp.concatenate([c, c_ctx[None, :], jnp.zeros((8 - batch - 1, d), F32)], axis=0)
    mod = _ada_call(cc, w_ada, b_ada)
    mod3 = mod[:, :batch + 1, :].reshape(depth * (batch + 1) * 6, 1, d)

    cos_tab, sin_tab, prot = _rope_tables(seq)

    srcs = (x.reshape(-1, d), ctx.reshape(-1, d))
    for layer in range(depth):
        last = layer == depth - 1
        base = layer * (batch + 1) * 6
        w_p = _permute_w_in(w_in[layer])
        w_g = _gate_w(w_in[layer])
        p, gates, w_out_bf = _inproj_call(srcs, g_mix[layer][None, :], mod3, w_p, w_g, w_out, layer, geom, base)

        bs_full = jnp.repeat(gm_b_s[layer].T, HEAD_DIM, axis=1)
        ya = _gmlp_call(p, gm_v_gain[layer][None, :], gm_w_s[layer].astype(BF16), bs_full, 512)

        yb = _attn_call(p, attn_sink[layer], cos_tab, sin_tab, prot, geom)

        w8 = jnp.concatenate([ml_conv_w[layer], jnp.zeros((5, 2 * ML_WIDTH), F32)], axis=0)
        b8 = jnp.concatenate([ml_conv_b[layer][None, :], jnp.zeros((7, 2 * ML_WIDTH), F32)], axis=0)
        q_conv, kt_conv = _conv_call(p, w8, b8, geom)
        n_chunks = rows // CHUNK
        prep_chunks = max(c for c in range(1, 17) if n_chunks % c == 0)
        gp = _gate_prep_call(gates, _gate_bias(ml_b_i[layer], ml_b_f[layer]), prep_chunks * CHUNK)
        hf, hb = _mlstm_call(q_conv, kt_conv, p, gp, geom)

        n_mix = (geom["r_lat"] if last else rows) // geom["tm_mix"]
        xs, w_ff1_bf, w_ff2_bf = _mixout_call(srcs, ya, yb, hf, hb, p, ml_out_gain[layer][None, :], mod3, w_out_bf,
                                              (w_ff1, w_ff2), layer, geom, base, n_mix)
        xs = _ffn_call(xs, g_ffn[layer][None, :], mod3, w_ff1_bf, w_ff2_bf, g_final[None, :], geom, base, n_mix, last)
        srcs = (xs,)
    return xs.reshape(batch, seq, d)
```

```python
import functools
import math

import jax
import jax.numpy as jnp
import numpy as np
from jax import lax
from jax.experimental import pallas as pl
from jax.experimental.pallas import tpu as pltpu

F32 = jnp.float32
BF16 = jnp.bfloat16

HEAD_DIM = 128
CHUNK = 128
GM_GROUPS = 4
ATT_HEADS = 6
ATT_KV_HEADS = 2
ATT_GROUP = ATT_HEADS // ATT_KV_HEADS
ML_HEADS = 6
GM_WIDTH = GM_GROUPS * HEAD_DIM
ATT_WIDTH = ATT_HEADS * HEAD_DIM
ATT_KV_WIDTH = ATT_KV_HEADS * HEAD_DIM
ML_WIDTH = ML_HEADS * HEAD_DIM
MIX_WIDTH = GM_WIDTH + ATT_WIDTH + ML_WIDTH
GRID_W = 64
ROPE_BASE = 10000.0
EPS = 1e-6
NEG_BIG = -1e30
LOG2E = math.log2(math.e)
LN_INV_SCALE = 0.5 * math.log(HEAD_DIM)

ATT_TILE = 2 * CHUNK
GATE_LANES = 16
GP_ROWS = 6 * GATE_LANES

_SRC = dict(gu=0, gv=512, aq=1024, ak=1792, av=2048, mq=2304, mk=3072, mv=3840, mo=4608, mg=5376)
_P_ORDER = (("aq", 768), ("mq", 768), ("gu", 512), ("gv", 512), ("ak", 256), ("av", 256),
            ("mk", 768), ("mv", 768), ("mo", 768))
P_WIDTH = sum(w for _, w in _P_ORDER)
_P_OFF = {}
_o = 0
for _n, _w in _P_ORDER:
    _P_OFF[_n] = _o
    _o += _w

VMEM_LIMIT = 56 * 1024 * 1024


def _cparams(sem):
    return pltpu.CompilerParams(dimension_semantics=sem, vmem_limit_bytes=VMEM_LIMIT)


def _any(preds):
    return functools.reduce(jnp.logical_or, preds)


def _ada_kernel(c_ref, w_ref, b_ref, o_ref):
    c = c_ref[...]
    s = (c * jax.nn.sigmoid(c)).astype(BF16)
    o_ref[...] = jnp.dot(s, w_ref[...].astype(BF16), preferred_element_type=F32) + b_ref[...]


def _ada_call(cc, w_ada, b_ada):
    depth, d, n = w_ada.shape
    tn = math.gcd(n, 2048)
    return pl.pallas_call(
        _ada_kernel,
        out_shape=jax.ShapeDtypeStruct((depth, cc.shape[0], n), F32),
        grid=(depth, n // tn),
        in_specs=[pl.BlockSpec((cc.shape[0], d), lambda l, j: (0, 0)),
                  pl.BlockSpec((None, d, tn), lambda l, j: (l, 0, j)),
                  pl.BlockSpec((None, 1, tn), lambda l, j: (l, 0, j))],
        out_specs=pl.BlockSpec((None, cc.shape[0], tn), lambda l, j: (l, 0, j)),
        compiler_params=_cparams(("arbitrary", "arbitrary")),
        name="ada_mod",
    )(cc, w_ada, b_ada.reshape(depth, 1, n))


def _norm_mod(x, gain, scale, shift):
    y = x * lax.rsqrt(jnp.mean(x * x, axis=-1, keepdims=True) + EPS) * gain
    return (y * (1.0 + scale) + shift).astype(BF16)


def _norm_mod_blocked(src_ref, dst_ref, gain, scale, shift, rows_per=256):
    n, d = src_ref.shape
    g2 = gain * (1.0 + scale)
    for r0 in range(0, n, rows_per):
        rs = slice(r0, r0 + rows_per)
        ss = jnp.zeros((rows_per, HEAD_DIM), F32)
        for c0 in range(0, d, HEAD_DIM):
            xb = src_ref[rs, c0:c0 + HEAD_DIM]
            ss = ss + xb * xb
        inv = lax.rsqrt(jnp.sum(ss, axis=-1, keepdims=True) * (1.0 / d) + EPS)
        for c0 in range(0, d, HEAD_DIM):
            cs = slice(c0, c0 + HEAD_DIM)
            dst_ref[rs, cs] = (src_ref[rs, cs] * inv * g2[:, cs] + shift[:, cs]).astype(BF16)


def _inproj_fill(src_ref, g_ref, sh_ref, sc_ref, wg_ref, gate_ref, h_scr):
    n = src_ref.shape[0]
    _norm_mod_blocked(src_ref, h_scr, g_ref[...], sc_ref[0], sh_ref[0])
    gate_ref[:, 0:n] = lax.dot_general(wg_ref[...], h_scr[0:n, :], (((1,), (1,)), ((), ())),
                                       preferred_element_type=F32)


def _side_cast(stack, layer, n_steps, step):
    _, r, c = stack.shape
    nblk = 1
    while nblk * 2 <= n_steps and r % (nblk * 2) == 0 and (r // (nblk * 2)) % 16 == 0:
        nblk *= 2
    blk = lambda *ids: jnp.minimum(step(*ids), nblk - 1)
    in_spec = pl.BlockSpec((None, r // nblk, c), lambda *ids: (layer, blk(*ids), 0))
    out_spec = pl.BlockSpec((r // nblk, c), lambda *ids: (blk(*ids), 0))
    return in_spec, out_spec, jax.ShapeDtypeStruct((r, c), BF16)


def _inproj_kernel(x_ref, g_ref, sh_ref, sc_ref, w_ref, wg_ref, side_ref, p_ref, gate_ref, side_out_ref, h_scr):
    @pl.when(pl.program_id(1) == 0)
    def _():
        _inproj_fill(x_ref, g_ref, sh_ref, sc_ref, wg_ref, gate_ref, h_scr)

    p_ref[...] = jnp.dot(h_scr[...], w_ref[...], preferred_element_type=F32).astype(BF16)
    side_out_ref[...] = side_ref[...].astype(BF16)


def _inproj_split_kernel(n_lat_tiles, x_ref, c_ref, g_ref, sh_ref, sc_ref, w_ref, wg_ref, side_ref,
                         p_ref, gate_ref, side_out_ref, h_scr):
    i, j = pl.program_id(0), pl.program_id(1)

    @pl.when(jnp.logical_and(j == 0, i < n_lat_tiles))
    def _():
        _inproj_fill(x_ref, g_ref, sh_ref, sc_ref, wg_ref, gate_ref, h_scr)

    @pl.when(jnp.logical_and(j == 0, i >= n_lat_tiles))
    def _():
        _inproj_fill(c_ref, g_ref, sh_ref, sc_ref, wg_ref, gate_ref, h_scr)

    p_ref[...] = jnp.dot(h_scr[...], w_ref[...], preferred_element_type=F32).astype(BF16)
    side_out_ref[...] = side_ref[...].astype(BF16)


def _inproj_call(srcs, gain, mod3, w, wg, side_stack, layer, geom, layer_mod_base):
    d = srcs[0].shape[1]
    rows, tm, seg = geom["rows"], geom["tm"], geom["seg"]
    tn = P_WIDTH // 3
    side_in, side_out, side_shape = _side_cast(side_stack, layer, pl.cdiv(rows, tm), lambda i, j: i)
    modspec = lambda k: pl.BlockSpec((1, 1, d), lambda i, j: (layer_mod_base + seg(i) * 6 + k, 0, 0))
    if len(srcs) == 1:
        body = _inproj_kernel
        src_specs = [pl.BlockSpec((tm, d), lambda i, j: (i, 0))]
    else:
        n_lat_tiles = geom["r_lat"] // tm
        assert geom["r_ctx"] <= tm
        body = functools.partial(_inproj_split_kernel, n_lat_tiles)
        src_specs = [pl.BlockSpec((tm, d), lambda i, j: (jnp.minimum(i, n_lat_tiles - 1), 0)),
                     pl.BlockSpec((geom["r_ctx"], d), lambda i, j: (0, 0))]
    return pl.pallas_call(
        body,
        out_shape=(jax.ShapeDtypeStruct((rows, P_WIDTH), BF16),
                   jax.ShapeDtypeStruct((2 * GATE_LANES, rows), F32), side_shape),
        grid=(pl.cdiv(rows, tm), P_WIDTH // tn),
        in_specs=src_specs + [pl.BlockSpec((1, d), lambda i, j: (0, 0)), modspec(0), modspec(1),
                              pl.BlockSpec((d, tn), lambda i, j: (0, j)),
                              pl.BlockSpec((2 * GATE_LANES, d), lambda i, j: (0, 0)), side_in],
        out_specs=(pl.BlockSpec((tm, tn), lambda i, j: (i, j)),
                   pl.BlockSpec((2 * GATE_LANES, tm), lambda i, j: (0, i)), side_out),
        scratch_shapes=[pltpu.VMEM((tm, d), BF16)],
        compiler_params=_cparams(("arbitrary", "arbitrary")),
        name="in_proj",
    )(*srcs, gain, mod3, mod3, w, wg, side_stack)


def _gelu(x):
    return jax.nn.gelu(x, approximate=True)


def _gmlp_kernel(gu_ref, gv_ref, gain_ref, ws_ref, bs_ref, o_ref):
    units = [(c, g) for c in range(gu_ref.shape[0] // CHUNK) for g in range(GM_GROUPS)]
    rows = lambda c: slice(c * CHUNK, (c + 1) * CHUNK)
    cols = lambda g: slice(g * HEAD_DIM, (g + 1) * HEAD_DIM)
    vns = []
    for c, g in units:
        vg = _gelu(gv_ref[rows(c), cols(g)].astype(F32))
        vn = vg * lax.rsqrt(jnp.mean(vg * vg, axis=-1, keepdims=True) + EPS) * gain_ref[:, cols(g)]
        vns.append(vn.astype(BF16))
    ss = [jnp.dot(ws_ref[g], vn, preferred_element_type=F32) for (c, g), vn in zip(units, vns)]
    for (c, g), s in zip(units, ss):
        u = _gelu(gu_ref[rows(c), cols(g)].astype(F32))
        o_ref[rows(c), cols(g)] = (u * (s + bs_ref[:, cols(g)])).astype(BF16)


def _gmlp_call(p, v_gain, ws, bs_full, tr):
    rows = p.shape[0]
    gu_blk = _P_OFF["gu"] // GM_WIDTH
    gv_blk = _P_OFF["gv"] // GM_WIDTH
    return pl.pallas_call(
        _gmlp_kernel,
        out_shape=jax.ShapeDtypeStruct((rows, GM_WIDTH), BF16),
        grid=(rows // tr,),
        in_specs=[pl.BlockSpec((tr, GM_WIDTH), lambda i: (i, gu_blk)),
                  pl.BlockSpec((tr, GM_WIDTH), lambda i: (i, gv_blk)),
                  pl.BlockSpec((1, GM_WIDTH), lambda i: (0, 0)),
                  pl.BlockSpec((GM_GROUPS, CHUNK, CHUNK), lambda i: (0, 0, 0)),
                  pl.BlockSpec((CHUNK, GM_WIDTH), lambda i: (0, 0))],
        out_specs=pl.BlockSpec((tr, GM_WIDTH), lambda i: (i, 0)),
        compiler_params=_cparams(("arbitrary",)),
        name="gmlp",
    )(p, p, v_gain, ws, bs_full)


def _rope(x_bf, cos, sin, prot):
    rot = jnp.dot(x_bf, prot, preferred_element_type=F32)
    return (x_bf.astype(F32) * cos + rot * sin).astype(BF16)


def _attn_kernel(nt_seq, n_lat_tiles, sink_ref, q_ref, kp_ref, km_ref, kn_ref, vp_ref, vm_ref, vn_ref,
                 kx_ref, vx_ref, cp_ref, sp_ref, cm_ref, sm_ref, cn_ref, sn_ref, prot_ref, o_ref):
    t = pl.program_id(0)
    is_lat = t < n_lat_tiles
    pos = t % nt_seq
    has_cur = is_lat.astype(jnp.int32)
    has_prev = jnp.logical_and(is_lat, pos != 0).astype(jnp.int32)
    has_next = jnp.logical_and(is_lat, pos != nt_seq - 1).astype(jnp.int32)
    prot = prot_ref[...]
    nq = ATT_TILE // CHUNK
    q_scale = HEAD_DIM ** -0.5 * LOG2E

    qi = lax.broadcasted_iota(jnp.int32, (ATT_GROUP * CHUNK, 3 * CHUNK), 0) & (CHUNK - 1)
    kj = lax.broadcasted_iota(jnp.int32, (ATT_GROUP * CHUNK, 3 * CHUNK), 1)
    band = jnp.where(jnp.abs(kj - CHUNK - qi) <= CHUNK, 1, 0)
    hrow = lax.broadcasted_iota(jnp.int32, (ATT_GROUP * CHUNK, 1), 0)

    cos_k = jnp.concatenate([cp_ref[...], cm_ref[...], cn_ref[...]], axis=0)
    sin_k = jnp.concatenate([sp_ref[...], sm_ref[...], sn_ref[...]], axis=0)
    cos_q, sin_q = cm_ref[...] * q_scale, sm_ref[...] * q_scale
    head = lambda h: slice(h * HEAD_DIM, (h + 1) * HEAD_DIM)

    masks = []
    for a in range(nq):
        flags = [has_prev if a + j == 0 else (has_next if a + j == nq + 1 else has_cur) for j in range(3)]
        valid = band * jnp.where(kj < CHUNK, flags[0], jnp.where(kj < 2 * CHUNK, flags[1], flags[2])) > 0
        masks.append(jnp.where(valid, 0.0, NEG_BIG))

    k_rot = [_rope(jnp.concatenate([kp_ref[:, head(g)], km_ref[:, head(g)], kn_ref[:, head(g)]], axis=0),
                   cos_k, sin_k, prot) for g in range(ATT_KV_HEADS)]
    v_loc = [jnp.concatenate([vp_ref[:, head(g)], vm_ref[:, head(g)], vn_ref[:, head(g)]], axis=0)
             for g in range(ATT_KV_HEADS)]
    sks = [jnp.where(hrow < CHUNK, sink_ref[g * ATT_GROUP],
                     jnp.where(hrow < 2 * CHUNK, sink_ref[g * ATT_GROUP + 1], sink_ref[g * ATT_GROUP + 2])) * LOG2E
           for g in range(ATT_KV_HEADS)]
    units = [(g, a) for g in range(ATT_KV_HEADS) for a in range(nq)]
    rows = lambda a: slice(a * CHUNK, (a + 1) * CHUNK)
    q3s = [jnp.concatenate([_rope(q_ref[rows(a), head(g * ATT_GROUP + h)], cos_q[rows(a)], sin_q[rows(a)], prot)
                            for h in range(ATT_GROUP)], axis=0) for g, a in units]
    ss = [lax.dot_general(q3, jnp.concatenate([k_rot[g][a * CHUNK:(a + 3) * CHUNK], kx_ref[:, head(g)]], axis=0),
                          (((1,), (1,)), ((), ())), preferred_element_type=F32)
          for (g, a), q3 in zip(units, q3s)]
    ps, p_sinks = [], []
    for (g, a), s in zip(units, ss):
        s_loc = s[:, :3 * CHUNK] + masks[a]
        s_ctx = s[:, 3 * CHUNK:]
        m = jnp.maximum(jnp.maximum(jnp.max(s_loc, axis=1, keepdims=True),
                                    jnp.max(s_ctx, axis=1, keepdims=True)), sks[g])
        p_sinks.append(jnp.exp2(sks[g] - m))
        ps.append(jnp.concatenate([jnp.exp2(s_loc - m), jnp.exp2(s_ctx - m)], axis=1).astype(BF16))
    ones_bf = jnp.ones((3 * CHUNK + kx_ref.shape[0], HEAD_DIM), BF16)
    for (g, a), p_all, p_sink in zip(units, ps, p_sinks):
        v_all = jnp.concatenate([v_loc[g][a * CHUNK:(a + 3) * CHUNK], vx_ref[:, head(g)]], axis=0)
        r = jnp.dot(p_all, jnp.concatenate([v_all, ones_bf], axis=1), preferred_element_type=F32)
        o = r[:, :HEAD_DIM] / (r[:, HEAD_DIM:] + p_sink)
        for h in range(ATT_GROUP):
            o_ref[rows(a), head(g * ATT_GROUP + h)] = o[h * CHUNK:(h + 1) * CHUNK, :].astype(BF16)


def _attn_call(p, sink, cos_tab, sin_tab, prot, geom):
    rows = p.shape[0]
    seq, lc, r_lat = geom["seq"], geom["lc"], geom["r_lat"]
    assert seq % ATT_TILE == 0 and lc % ATT_TILE == 0
    nt_seq, n_lat_tiles, nq = seq // ATT_TILE, r_lat // ATT_TILE, ATT_TILE // CHUNK
    ak_blk = _P_OFF["ak"] // ATT_KV_WIDTH
    av_blk = _P_OFF["av"] // ATT_KV_WIDTH
    ctx_base = r_lat // lc

    def is_lat(t):
        return t < n_lat_tiles

    def prev_ok(t):
        return jnp.logical_and(is_lat(t), t % nt_seq != 0)

    def next_ok(t):
        return jnp.logical_and(is_lat(t), t % nt_seq != nt_seq - 1)

    def prev_i(t):
        return jnp.where(prev_ok(t), nq * t - 1, nq * t)

    def next_i(t):
        return jnp.where(next_ok(t), nq * t + nq, nq * t + nq - 1)

    def tab_main(t):
        return jnp.where(is_lat(t), t % nt_seq, nt_seq)

    def tab_prev(t):
        return jnp.where(prev_ok(t), nq * (t % nt_seq) - 1, nq * nt_seq)

    def tab_next(t):
        return jnp.where(next_ok(t), nq * (t % nt_seq) + nq, nq * nt_seq)

    def ctx_i(t):
        return ctx_base + jnp.where(is_lat(t), t // nt_seq, (t - n_lat_tiles) // (lc // ATT_TILE))

    halo = lambda f, blk: pl.BlockSpec((CHUNK, ATT_KV_WIDTH), lambda t: (f(t), blk))
    main = lambda blk: pl.BlockSpec((ATT_TILE, ATT_KV_WIDTH), lambda t: (t, blk))
    tb = lambda rws, f: pl.BlockSpec((rws, HEAD_DIM), lambda t: (f(t), 0))
    return pl.pallas_call(
        functools.partial(_attn_kernel, nt_seq, n_lat_tiles),
        out_shape=jax.ShapeDtypeStruct((rows, ATT_WIDTH), BF16),
        grid=(rows // ATT_TILE,),
        in_specs=[pl.BlockSpec(memory_space=pltpu.SMEM),
                  pl.BlockSpec((ATT_TILE, ATT_WIDTH), lambda t: (t, 0)),
                  halo(prev_i, ak_blk), main(ak_blk), halo(next_i, ak_blk),
                  halo(prev_i, av_blk), main(av_blk), halo(next_i, av_blk),
                  pl.BlockSpec((lc, ATT_KV_WIDTH), lambda t: (ctx_i(t), ak_blk)),
                  pl.BlockSpec((lc, ATT_KV_WIDTH), lambda t: (ctx_i(t), av_blk)),
                  tb(CHUNK, tab_prev), tb(CHUNK, tab_prev), tb(ATT_TILE, tab_main), tb(ATT_TILE, tab_main),
                  tb(CHUNK, tab_next), tb(CHUNK, tab_next),
                  pl.BlockSpec((HEAD_DIM, HEAD_DIM), lambda t: (0, 0))],
        out_specs=pl.BlockSpec((ATT_TILE, ATT_WIDTH), lambda t: (t, 0)),
        compiler_params=_cparams(("arbitrary",)),
        name="window_attn",
    )(sink, p, p, p, p, p, p, p, p, p, cos_tab, sin_tab, cos_tab, sin_tab, cos_tab, sin_tab, prot)


def _conv_kernel(start_chunks, end_chunks, q_ref, qp_ref, qn_ref, k_ref, kp_ref, kn_ref, w_ref, b_ref, sh_ref,
                 oq_ref, okt_ref):
    nc = q_ref.shape[0] // CHUNK
    i = pl.program_id(0)
    srcs = ((q_ref, qp_ref, qn_ref), (k_ref, kp_ref, kn_ref))
    units = [(part, c) for part in range(2) for c in range(nc)]
    shifted = []
    for part, c in units:
        x_ref, xp_ref, xn_ref = srcs[part]
        cg = i * nc + c
        at_start = _any([cg == s for s in start_chunks])
        at_end = _any([cg == e for e in end_chunks])
        cur = x_ref[c * CHUNK:(c + 1) * CHUNK, :]
        prv = xp_ref[...] if c == 0 else x_ref[(c - 1) * CHUNK:c * CHUNK, :]
        nxt = xn_ref[...] if c == nc - 1 else x_ref[(c + 1) * CHUNK:(c + 2) * CHUNK, :]
        s_prev = sh_ref[jnp.where(at_start, 1, 0)]
        s_next = sh_ref[jnp.where(at_end, 3, 2)]
        shifted.append((jnp.dot(s_prev, jnp.concatenate([prv, cur], axis=0), preferred_element_type=F32),
                        jnp.dot(s_next, jnp.concatenate([cur, nxt], axis=0), preferred_element_type=F32)))
    for (part, c), (x_prev, x_next) in zip(units, shifted):
        cs = slice(part * ML_WIDTH, (part + 1) * ML_WIDTH)
        cur = srcs[part][0][c * CHUNK:(c + 1) * CHUNK, :].astype(F32)
        y = b_ref[0:1, cs] + x_prev * w_ref[0:1, cs] + cur * w_ref[1:2, cs] + x_next * w_ref[2:3, cs]
        hy = 0.5 * y
        y = hy + hy * jnp.tanh(hy)
        if part == 0:
            oq_ref[c * CHUNK:(c + 1) * CHUNK, :] = y.astype(BF16)
        else:
            for h in range(ML_HEADS):
                r0 = (c * ML_HEADS + h) * HEAD_DIM
                okt_ref[r0:r0 + HEAD_DIM, :] = y[:, h * HEAD_DIM:(h + 1) * HEAD_DIM].T.astype(BF16)


def _shift_matrices():
    t = np.arange(CHUNK)
    prev = np.zeros((CHUNK, 2 * CHUNK), np.float32)
    prev[t, t + CHUNK - 1] = 1.0
    nxt = np.zeros((CHUNK, 2 * CHUNK), np.float32)
    nxt[t, t + 1] = 1.0
    prev_cut, nxt_cut = prev.copy(), nxt.copy()
    prev_cut[0] = 0.0
    nxt_cut[CHUNK - 1] = 0.0
    return jnp.asarray(np.stack([prev, prev_cut, nxt, nxt_cut]), BF16)


def _conv_call(p, w8, b8, geom):
    rows = p.shape[0]
    tr = geom["conv_tile"]
    nc = tr // CHUNK
    mq_blk = _P_OFF["mq"] // ML_WIDTH
    mk_blk = _P_OFF["mk"] // ML_WIDTH
    start_chunks = tuple(s // CHUNK for s in geom["starts"])
    end_chunks = tuple(e // CHUNK for e in geom["ends"])

    def trio(blk):
        return [pl.BlockSpec((tr, ML_WIDTH), lambda i: (i, blk)),
                pl.BlockSpec((CHUNK, ML_WIDTH), lambda i: (jnp.maximum(i * nc - 1, 0), blk)),
                pl.BlockSpec((CHUNK, ML_WIDTH), lambda i: (jnp.minimum((i + 1) * nc, rows // CHUNK - 1), blk))]

    return pl.pallas_call(
        functools.partial(_conv_kernel, start_chunks, end_chunks),
        out_shape=(jax.ShapeDtypeStruct((rows, ML_WIDTH), BF16),
                   jax.ShapeDtypeStruct((rows * ML_HEADS, CHUNK), BF16)),
        grid=(rows // tr,),
        in_specs=trio(mq_blk) + trio(mk_blk) + [
            pl.BlockSpec((8, 2 * ML_WIDTH), lambda i: (0, 0)),
            pl.BlockSpec((8, 2 * ML_WIDTH), lambda i: (0, 0)),
            pl.BlockSpec((4, CHUNK, 2 * CHUNK), lambda i: (0, 0, 0))],
        out_specs=(pl.BlockSpec((tr, ML_WIDTH), lambda i: (i, 0)),
                   pl.BlockSpec((tr * ML_HEADS, CHUNK), lambda i: (i, 0))),
        compiler_params=_cparams(("arbitrary",)),
        name="short_conv",
    )(p, p, p, p, p, p, w8, b8, _shift_matrices())


def _split3(x):
    hi = x.astype(BF16)
    r1 = x - hi.astype(F32)
    mid = r1.astype(BF16)
    lo = (r1 - mid.astype(F32)).astype(BF16)
    return hi, mid, lo


def _cummax_lanes(xs, reverses):
    n = xs[0].shape[1]
    lane = lax.broadcasted_iota(jnp.int32, xs[0].shape, 1)
    sh = 1
    while sh < n:
        xs = [jnp.maximum(x, jnp.where(lane < n - sh, pltpu.roll(x, n - sh, 1), NEG_BIG)) if rev else
              jnp.maximum(x, jnp.where(lane >= sh, pltpu.roll(x, sh, 1), NEG_BIG)) for x, rev in zip(xs, reverses)]
        sh *= 2
    return xs


def _gate_prep_kernel(g_ref, bias_ref, o_ref):
    gl = GATE_LANES
    row = lax.broadcasted_iota(jnp.int32, (CHUNK, CHUNK), 0)
    col = lax.broadcasted_iota(jnp.int32, (CHUNK, CHUNK), 1)
    chunks = range(g_ref.shape[1] // CHUNK)
    pres = [g_ref[:, c * CHUNK:(c + 1) * CHUNK] + bias_ref[...] for c in chunks]
    lis = [pre[0:gl] for pre in pres]
    lf3s = []
    for pre in pres:
        pf = pre[gl:2 * gl]
        lf3s.append(jnp.concatenate(_split3(jnp.minimum(pf, 0.0) - jnp.log1p(jnp.exp(-jnp.abs(pf)))), axis=0))
    units = [(c, d) for c in chunks for d in range(2)]
    csum = [((row <= col) if d == 0 else (row >= col)).astype(BF16) for d in range(2)]
    cs3s = [jnp.dot(lf3s[c], csum[d], preferred_element_type=F32) for c, d in units]
    bs = [cs3[0:gl] + cs3[gl:2 * gl] + cs3[2 * gl:3 * gl] for cs3 in cs3s]
    us = [lis[c] - b for (c, d), b in zip(units, bs)]
    umaxs = _cummax_lanes(us, [d == 1 for c, d in units])
    for (c, d), b, u, umax in zip(units, bs, us, umaxs):
        g_rep = jnp.broadcast_to(b[:, CHUNK - 1:CHUNK] if d == 0 else b[:, 0:1], b.shape)
        a = g_rep - b + lis[c]
        mloc = jnp.broadcast_to(jnp.max(a, axis=1, keepdims=True), b.shape)
        r0 = (c * 2 + d) * GP_ROWS
        o_ref[r0:r0 + GP_ROWS, :] = jnp.concatenate([b, u, umax, jnp.exp(a - mloc), g_rep, mloc], axis=0)


def _gate_prep_call(gates, bias, tr):
    rows = gates.shape[1]
    nc = tr // CHUNK
    return pl.pallas_call(
        _gate_prep_kernel,
        out_shape=jax.ShapeDtypeStruct((rows // CHUNK * 2 * GP_ROWS, CHUNK), F32),
        grid=(rows // tr,),
        in_specs=[pl.BlockSpec((2 * GATE_LANES, tr), lambda i: (0, i)),
                  pl.BlockSpec((2 * GATE_LANES, CHUNK), lambda i: (0, 0))],
        out_specs=pl.BlockSpec((nc * 2 * GP_ROWS, CHUNK), lambda i: (i, 0)),
        compiler_params=_cparams(("arbitrary",)),
        name="gate_prep",
    )(gates, bias)


def _mlstm_kernel(qf_ref, kf_ref, vf_ref, gf_ref, qb_ref, kb_ref, vb_ref, gb_ref, hf_ref, hb_ref, ct_scr, m_scr):
    @pl.when(pl.program_id(1) == 0)
    def _():
        ct_scr[...] = jnp.zeros_like(ct_scr)
        m_scr[...] = jnp.zeros_like(m_scr)

    row = lax.broadcasted_iota(jnp.int32, (CHUNK, CHUNK), 0)
    col = lax.broadcasted_iota(jnp.int32, (CHUNK, CHUNK), 1)
    ones_bf = jnp.ones((CHUNK, HEAD_DIM), BF16)
    gl = GATE_LANES
    dirs = ((qf_ref, kf_ref, vf_ref, gf_ref, hf_ref), (qb_ref, kb_ref, vb_ref, gb_ref, hb_ref))
    prep = []
    for d, (_, _, _, g_ref, _) in enumerate(dirs):
        b, u, umax, w, g_rep, mloc = (g_ref[k * gl:(k + 1) * gl, :] for k in range(6))
        m0 = m_scr[d]
        mx = jnp.maximum(m0, umax)
        em = jnp.exp(LN_INV_SCALE - b - mx)
        m_new = jnp.maximum(g_rep + m0, mloc)
        a_old = jnp.exp(g_rep + m0 - m_new)
        w_new = w * jnp.exp(mloc - m_new)
        m_scr[d] = m_new
        cols = jnp.concatenate([mx * -LOG2E, em, jnp.zeros((CHUNK - 2 * gl, CHUNK), F32)], axis=0).T
        prep.append((u * LOG2E, w_new, m0 * LOG2E, a_old, cols))
    units = [(d, h) for d in range(2) for h in range(ML_HEADS)]
    head = lambda h: slice(h * HEAD_DIM, (h + 1) * HEAD_DIM)
    qks = [jnp.dot(dirs[d][0][:, head(h)], dirs[d][1][head(h), :], preferred_element_type=F32) for d, h in units]
    s_exts, em_bs = [], []
    for (d, h), qk in zip(units, qks):
        order = (col <= row) if d == 0 else (col >= row)
        u2, _, m02, _, cols = prep[d]
        j = d * ML_HEADS + h
        c_b = jnp.broadcast_to(cols[:, j:j + 1], (CHUNK, CHUNK))
        em_bs.append(jnp.broadcast_to(cols[:, gl + j:gl + j + 1], (CHUNK, CHUNK)))
        d_in = jnp.where(order, c_b + u2[j:j + 1, :], NEG_BIG)
        d_st = c_b + m02[j:j + 1, :]
        qh = dirs[d][0][:, head(h)]
        s_exts.append(jnp.concatenate([qk * jnp.exp2(d_in), qh.astype(F32) * jnp.exp2(d_st)], axis=1).astype(BF16))
    for (d, h), s_ext, em_b in zip(units, s_exts, em_bs):
        j = d * ML_HEADS + h
        v_aug = jnp.concatenate([dirs[d][2][:, head(h)], ones_bf], axis=1)
        rhs = jnp.concatenate([v_aug, ct_scr[j].astype(BF16)], axis=0)
        r = jnp.dot(s_ext, rhs, preferred_element_type=F32)
        hh = r[:, :HEAD_DIM] / jnp.maximum(jnp.abs(r[:, HEAD_DIM:]), em_b)
        dirs[d][4][:, head(h)] = hh.astype(BF16)
    for d, h in units:
        _, w_new, _, a_old, _ = prep[d]
        j = d * ML_HEADS + h
        v_aug = jnp.concatenate([dirs[d][2][:, head(h)], ones_bf], axis=1)
        ktw = (dirs[d][1][head(h), :].astype(F32) * w_new[j:j + 1, :]).astype(BF16)
        dct = jnp.dot(ktw, v_aug, preferred_element_type=F32)
        ao = jnp.concatenate([a_old[j:j + 1, :], a_old[j:j + 1, :]], axis=1)
        ct_scr[j] = ao * ct_scr[j] + dct


def _mlstm_call(q, kt, p, gp, geom):
    rows = p.shape[0]
    nb, n_lat, lc, batch = geom["nb"], geom["n_lat"], geom["lc"], geom["batch"]
    ncs = lc // CHUNK
    mv_blk = _P_OFF["mv"] // ML_WIDTH

    def fwd(b, s):
        return jnp.where(s < ncs, n_lat + b * ncs + s, b * nb + (s - ncs))

    def bwd(b, s):
        return jnp.where(s < ncs, n_lat + b * ncs + (ncs - 1 - s), b * nb + (nb - 1 - (s - ncs)))

    def specs(f, d):
        return [pl.BlockSpec((CHUNK, ML_WIDTH), lambda b, s: (f(b, s), 0)),
                pl.BlockSpec((ML_WIDTH, CHUNK), lambda b, s: (f(b, s), 0)),
                pl.BlockSpec((CHUNK, ML_WIDTH), lambda b, s: (f(b, s), mv_blk)),
                pl.BlockSpec((GP_ROWS, CHUNK), lambda b, s: (2 * f(b, s) + d, 0))]

    out = jax.ShapeDtypeStruct((rows, ML_WIDTH), BF16)
    return pl.pallas_call(
        _mlstm_kernel,
        out_shape=(out, out),
        grid=(batch, ncs + nb),
        in_specs=specs(fwd, 0) + specs(bwd, 1),
        out_specs=(pl.BlockSpec((CHUNK, ML_WIDTH), lambda b, s: (fwd(b, s), 0)),
                   pl.BlockSpec((CHUNK, ML_WIDTH), lambda b, s: (bwd(b, s), 0))),
        scratch_shapes=[pltpu.VMEM((2 * ML_HEADS, HEAD_DIM, 2 * HEAD_DIM), F32),
                        pltpu.VMEM((2, GATE_LANES, CHUNK), F32)],
        compiler_params=_cparams(("arbitrary", "arbitrary")),
        name="mlstm",
    )(q, kt, p, gp, q, kt, p, gp)


def _mix_update(ya_ref, yb_ref, hf_ref, hb_ref, mo_ref, og_ref, gt_ref, w_ref):
    hs = hf_ref[...].astype(F32) + hb_ref[...].astype(F32)
    parts = [ya_ref[...], yb_ref[...]]
    for h in range(ML_HEADS):
        cs = slice(h * HEAD_DIM, (h + 1) * HEAD_DIM)
        xh = hs[:, cs]
        y = xh * lax.rsqrt(jnp.mean(xh * xh, axis=-1, keepdims=True) + EPS) * og_ref[:, cs]
        parts.append((y * jax.nn.sigmoid(mo_ref[:, cs].astype(F32))).astype(BF16))
    lhs = jnp.concatenate(parts, axis=1)
    return gt_ref[0] * jnp.dot(lhs, w_ref[...], preferred_element_type=F32)


def _mixout_kernel(x_ref, ya_ref, yb_ref, hf_ref, hb_ref, mo_ref, og_ref, gt_ref, w_ref, s1_ref, s2_ref,
                   o_ref, s1_out_ref, s2_out_ref):
    o_ref[...] = x_ref[...] + _mix_update(ya_ref, yb_ref, hf_ref, hb_ref, mo_ref, og_ref, gt_ref, w_ref)
    s1_out_ref[...] = s1_ref[...].astype(BF16)
    s2_out_ref[...] = s2_ref[...].astype(BF16)


def _mixout_split_kernel(n_lat_tiles, x_ref, c_ref, ya_ref, yb_ref, hf_ref, hb_ref, mo_ref, og_ref, gt_ref, w_ref,
                         s1_ref, s2_ref, o_ref, s1_out_ref, s2_out_ref):
    upd = _mix_update(ya_ref, yb_ref, hf_ref, hb_ref, mo_ref, og_ref, gt_ref, w_ref)
    is_ctx = jnp.full((x_ref.shape[0], 1), pl.program_id(0), jnp.int32) >= n_lat_tiles
    o_ref[...] = jnp.where(is_ctx, c_ref[...], x_ref[...]) + upd
    s1_out_ref[...] = s1_ref[...].astype(BF16)
    s2_out_ref[...] = s2_ref[...].astype(BF16)


def _mixout_call(srcs, ya, yb, hf, hb, p, out_gain, mod3, w_out, side_stacks, layer, geom, layer_mod_base, n_tiles):
    d = srcs[0].shape[1]
    rows, tm, seg = geom["rows"], geom["tm_mix"], geom["seg_mix"]
    mo_blk = _P_OFF["mo"] // ML_WIDTH
    row = lambda w, blk=0: pl.BlockSpec((tm, w), lambda i: (i, blk))
    side = [_side_cast(stack, layer, n_tiles, lambda i: i) for stack in side_stacks]
    if len(srcs) == 1:
        body, src_specs, alias = _mixout_kernel, [row(d)], {0: 0}
    else:
        n_lat_tiles = geom["r_lat"] // tm
        body = functools.partial(_mixout_split_kernel, n_lat_tiles)
        src_specs = [pl.BlockSpec((tm, d), lambda i: (jnp.minimum(i, n_lat_tiles - 1), 0)),
                     pl.BlockSpec((tm, d), lambda i: (jnp.maximum(i - n_lat_tiles, 0), 0))]
        alias = {}
    return pl.pallas_call(
        body,
        out_shape=(jax.ShapeDtypeStruct((rows, d), F32), side[0][2], side[1][2]),
        grid=(n_tiles,),
        in_specs=src_specs + [row(GM_WIDTH), row(ATT_WIDTH), row(ML_WIDTH), row(ML_WIDTH), row(ML_WIDTH, mo_blk),
                              pl.BlockSpec((1, ML_WIDTH), lambda i: (0, 0)),
                              pl.BlockSpec((1, 1, d), lambda i: (layer_mod_base + seg(i) * 6 + 2, 0, 0)),
                              pl.BlockSpec((MIX_WIDTH, d), lambda i: (0, 0)), side[0][0], side[1][0]],
        out_specs=(row(d), side[0][1], side[1][1]),
        input_output_aliases=alias,
        compiler_params=_cparams(("arbitrary",)),
        name="mix_out",
    )(*srcs, ya, yb, hf, hb, p, out_gain, mod3, w_out, *side_stacks)


def _ffn_kernel(final, x_ref, g_ref, sh_ref, sc_ref, gt_ref, w1_ref, w2_ref, gf_ref, o_ref, h_scr):
    j = pl.program_id(1)

    @pl.when(j == 0)
    def _():
        _norm_mod_blocked(x_ref, h_scr, g_ref[...], sc_ref[0], sh_ref[0])
        o_ref[...] = jnp.zeros_like(o_ref)

    a = jnp.maximum(jnp.dot(h_scr[...], w1_ref[...], preferred_element_type=F32), 0.0)
    o_ref[...] += jnp.dot((a * a).astype(BF16), w2_ref[...], preferred_element_type=F32)

    @pl.when(j == pl.num_programs(1) - 1)
    def _():
        xo = x_ref[...] + gt_ref[0] * o_ref[...]
        if final:
            xo = xo * lax.rsqrt(jnp.mean(xo * xo, axis=-1, keepdims=True) + EPS) * gf_ref[...]
        o_ref[...] = xo


def _ffn_call(xs, gain, mod3, w1, w2, g_final, geom, layer_mod_base, n_tiles, final):
    rows, d = xs.shape
    hidden = w1.shape[1]
    tm, seg = geom["tm_mix"], geom["seg_mix"]
    th = min(1024, hidden)
    out_rows = n_tiles * tm if final else rows
    modspec = lambda k: pl.BlockSpec((1, 1, d), lambda i, j: (layer_mod_base + seg(i) * 6 + k, 0, 0))
    return pl.pallas_call(
        functools.partial(_ffn_kernel, final),
        out_shape=jax.ShapeDtypeStruct((out_rows, d), F32),
        grid=(n_tiles, hidden // th),
        in_specs=[pl.BlockSpec((tm, d), lambda i, j: (i, 0)),
                  pl.BlockSpec((1, d), lambda i, j: (0, 0)),
                  modspec(3), modspec(4), modspec(5),
                  pl.BlockSpec((d, th), lambda i, j: (0, j)),
                  pl.BlockSpec((th, d), lambda i, j: (j, 0)),
                  pl.BlockSpec((1, d), lambda i, j: (0, 0))],
        out_specs=pl.BlockSpec((tm, d), lambda i, j: (i, 0)),
        scratch_shapes=[pltpu.VMEM((tm, d), BF16)],
        compiler_params=_cparams(("arbitrary", "arbitrary")),
        name="ffn_final" if final else "ffn",
    )(xs, gain, mod3, mod3, mod3, w1, w2, g_final)


def _rope_tables(seq):
    n_rows = seq // GRID_W
    half = HEAD_DIM // 2
    freqs = (1.0 / (ROPE_BASE ** (np.arange(0, half, 2, dtype=np.float32) / np.float32(half)))).astype(np.float32)
    ang_r = (np.arange(n_rows, dtype=np.float32)[:, None] * freqs).astype(np.float32).astype(np.float64)
    ang_c = (np.arange(GRID_W, dtype=np.float32)[:, None] * freqs).astype(np.float32).astype(np.float64)

    def full(fn, ident):
        r = jnp.repeat(jnp.asarray(fn(ang_r), F32), GRID_W, axis=0)
        c = jnp.tile(jnp.asarray(fn(ang_c), F32), (n_rows, 1))
        tab = jnp.concatenate([r, r, c, c], axis=-1)
        return jnp.concatenate([tab, jnp.full((ATT_TILE, HEAD_DIM), ident, F32)], axis=0)

    i = np.arange(HEAD_DIM)
    quarter = HEAD_DIM // 4
    prot = np.zeros((HEAD_DIM, HEAD_DIM), np.float32)
    first = (i % (2 * quarter)) < quarter
    prot[(i + quarter)[first], i[first]] = -1.0
    prot[(i - quarter)[~first], i[~first]] = 1.0
    return full(np.cos, 1.0), full(np.sin, 0.0), jnp.asarray(prot, BF16)


def _geometry(batch, seq, lc, d):
    tm = 1024 if seq % 1024 == 0 else 512
    tm_mix = 512
    assert seq % tm == 0 and seq % CHUNK == 0 and lc % CHUNK == 0 and (batch * seq) % lc == 0
    r_lat, r_ctx = batch * seq, batch * lc
    rows = r_lat + r_ctx
    assert rows % tm_mix == 0 and r_ctx % tm_mix == 0
    starts = tuple(b * seq for b in range(batch)) + tuple(r_lat + b * lc for b in range(batch))
    ends = tuple((b + 1) * seq - 1 for b in range(batch)) + tuple(r_lat + (b + 1) * lc - 1 for b in range(batch))
    n_chunks = rows // CHUNK

    def chunk_tile(limit):
        return CHUNK * max(c for c in range(1, limit + 1) if n_chunks % c == 0)

    return dict(tm=tm, tm_mix=tm_mix, rows=rows, r_lat=r_lat, r_ctx=r_ctx, batch=batch, seq=seq, lc=lc,
                conv_tile=chunk_tile(10), gmlp_tile=chunk_tile(20), prep_tile=chunk_tile(16),
                nb=seq // CHUNK, n_lat=r_lat // CHUNK,
                seg=lambda i: jnp.minimum(i // (seq // tm), batch),
                seg_mix=lambda i: jnp.minimum(i // (seq // tm_mix), batch),
                starts=starts, ends=ends)


def _permute_w_in(w):
    cols = [w[:, _SRC[n]:_SRC[n] + wd] for n, wd in _P_ORDER]
    return jnp.concatenate(cols, axis=1).astype(BF16)


def _gate_w(w):
    d = w.shape[0]
    mg = w[:, _SRC["mg"]:_SRC["mg"] + 4 * ML_HEADS].reshape(d, 2, 2, ML_HEADS)
    groups = []
    for t in range(2):
        part = mg[:, :, t, :].reshape(d, 2 * ML_HEADS)
        groups.append(jnp.pad(part, ((0, 0), (0, GATE_LANES - 2 * ML_HEADS))))
    return jnp.concatenate(groups, axis=1).T.astype(BF16)


def _gate_bias(b_i, b_f):
    pad = jnp.zeros((GATE_LANES - 2 * ML_HEADS,), F32)
    col = jnp.concatenate([b_i.reshape(-1), pad, b_f.reshape(-1), pad])
    return jnp.broadcast_to(col[:, None], (2 * GATE_LANES, CHUNK))


def kernel(x, c, ctx, c_ctx, w_ada, b_ada, g_mix, g_ffn, w_in, gm_v_gain, gm_w_s, gm_b_s, attn_sink,
           ml_conv_w, ml_conv_b, ml_b_i, ml_b_f, ml_out_gain, w_out, w_ff1, w_ff2, g_final):
    batch, seq, d = x.shape
    lc = ctx.shape[1]
    depth = w_ada.shape[0]
    geom = _geometry(batch, seq, lc, d)
    rows = geom["rows"]

    cc = jnp.concatenate([c, c_ctx[None, :], jnp.zeros((8 - batch - 1, d), F32)], axis=0)
    mod = _ada_call(cc, w_ada, b_ada)
    mod3 = mod[:, :batch + 1, :].reshape(depth * (batch + 1) * 6, 1, d)

    cos_tab, sin_tab, prot = _rope_tables(seq)

    srcs = (x.reshape(-1, d), ctx.reshape(-1, d))
    for layer in range(depth):
        last = layer == depth - 1
        base = layer * (batch + 1) * 6
        w_p = _permute_w_in(w_in[layer])
        w_g = _gate_w(w_in[layer])
        p, gates, w_out_bf = _inproj_call(srcs, g_mix[layer][None, :], mod3, w_p, w_g, w_out, layer, geom, base)

        bs_full = jnp.repeat(gm_b_s[layer].T, HEAD_DIM, axis=1)
        ya = _gmlp_call(p, gm_v_gain[layer][None, :], gm_w_s[layer].astype(BF16), bs_full, geom["gmlp_tile"])

        yb = _attn_call(p, attn_sink[layer], cos_tab, sin_tab, prot, geom)

        w8 = jnp.concatenate([ml_conv_w[layer], jnp.zeros((5, 2 * ML_WIDTH), F32)], axis=0)
        b8 = jnp.concatenate([ml_conv_b[layer][None, :], jnp.zeros((7, 2 * ML_WIDTH), F32)], axis=0)
        q_conv, kt_conv = _conv_call(p, w8, b8, geom)
        gp = _gate_prep_call(gates, _gate_bias(ml_b_i[layer], ml_b_f[layer]), geom["prep_tile"])
        hf, hb = _mlstm_call(q_conv, kt_conv, p, gp, geom)

        n_mix = (geom["r_lat"] if last else rows) // geom["tm_mix"]
        xs, w_ff1_bf, w_ff2_bf = _mixout_call(srcs, ya, yb, hf, hb, p, ml_out_gain[layer][None, :], mod3, w_out_bf,
                                              (w_ff1, w_ff2), layer, geom, base, n_mix)
        xs = _ffn_call(xs, g_ffn[layer][None, :], mod3, w_ff1_bf, w_ff2_bf, g_final[None, :], geom, base, n_mix, last)
        srcs = (xs,)
    return xs.reshape(batch, seq, d)
```

```python
import functools
import math

import jax
import jax.numpy as jnp
import numpy as np
from jax import lax
from jax.experimental import pallas as pl
from jax.experimental.pallas import tpu as pltpu

F32 = jnp.float32
BF16 = jnp.bfloat16

HEAD_DIM = 128
CHUNK = 128
GM_GROUPS = 4
ATT_HEADS = 6
ATT_KV_HEADS = 2
ATT_GROUP = ATT_HEADS // ATT_KV_HEADS
ML_HEADS = 6
GM_WIDTH = GM_GROUPS * HEAD_DIM
ATT_WIDTH = ATT_HEADS * HEAD_DIM
ATT_KV_WIDTH = ATT_KV_HEADS * HEAD_DIM
ML_WIDTH = ML_HEADS * HEAD_DIM
MIX_WIDTH = GM_WIDTH + ATT_WIDTH + ML_WIDTH
GRID_W = 64
ROPE_BASE = 10000.0
EPS = 1e-6
NEG_BIG = -1e30
LOG2E = math.log2(math.e)
LN_INV_SCALE = 0.5 * math.log(HEAD_DIM)

ATT_TILE = 2 * CHUNK
GATE_LANES = 16
GP_ROWS = 6 * GATE_LANES

_SRC = dict(gu=0, gv=512, aq=1024, ak=1792, av=2048, mq=2304, mk=3072, mv=3840, mo=4608, mg=5376)
_P_ORDER = (("aq", 768), ("mq", 768), ("gu", 512), ("gv", 512), ("ak", 256), ("av", 256),
            ("mk", 768), ("mv", 768), ("mo", 768))
P_WIDTH = sum(w for _, w in _P_ORDER)
_P_OFF = {}
_o = 0
for _n, _w in _P_ORDER:
    _P_OFF[_n] = _o
    _o += _w

VMEM_LIMIT = 56 * 1024 * 1024


def _cparams(sem):
    return pltpu.CompilerParams(dimension_semantics=sem, vmem_limit_bytes=VMEM_LIMIT)


def _any(preds):
    return functools.reduce(jnp.logical_or, preds)


def _ada_kernel(c_ref, w_ref, b_ref, o_ref):
    c = c_ref[...]
    s = (c * jax.nn.sigmoid(c)).astype(BF16)
    o_ref[...] = jnp.dot(s, w_ref[...].astype(BF16), preferred_element_type=F32) + b_ref[...]


def _ada_call(cc, w_ada, b_ada):
    depth, d, n = w_ada.shape
    tn = math.gcd(n, 2048)
    return pl.pallas_call(
        _ada_kernel,
        out_shape=jax.ShapeDtypeStruct((depth, cc.shape[0], n), F32),
        grid=(depth, n // tn),
        in_specs=[pl.BlockSpec((cc.shape[0], d), lambda l, j: (0, 0)),
                  pl.BlockSpec((None, d, tn), lambda l, j: (l, 0, j)),
                  pl.BlockSpec((None, 1, tn), lambda l, j: (l, 0, j))],
        out_specs=pl.BlockSpec((None, cc.shape[0], tn), lambda l, j: (l, 0, j)),
        compiler_params=_cparams(("arbitrary", "arbitrary")),
        name="ada_mod",
    )(cc, w_ada, b_ada.reshape(depth, 1, n))


def _norm_mod_blocked(src_ref, dst_ref, gain, scale, shift, rows_per=256):
    n, d = src_ref.shape
    g2 = gain * (1.0 + scale)
    for r0 in range(0, n, rows_per):
        rs = slice(r0, r0 + rows_per)
        ss = jnp.zeros((rows_per, HEAD_DIM), F32)
        for c0 in range(0, d, HEAD_DIM):
            xb = src_ref[rs, c0:c0 + HEAD_DIM]
            ss = ss + xb * xb
        inv = lax.rsqrt(jnp.sum(ss, axis=-1, keepdims=True) * (1.0 / d) + EPS)
        for c0 in range(0, d, HEAD_DIM):
            cs = slice(c0, c0 + HEAD_DIM)
            dst_ref[rs, cs] = (src_ref[rs, cs] * inv * g2[:, cs] + shift[:, cs]).astype(BF16)


def _inproj_fill(src_ref, g_ref, sh_ref, sc_ref, wg_ref, gate_ref, h_scr):
    n = src_ref.shape[0]
    _norm_mod_blocked(src_ref, h_scr, g_ref[...], sc_ref[0], sh_ref[0])
    gate_ref[:, 0:n] = lax.dot_general(wg_ref[...], h_scr[0:n, :], (((1,), (1,)), ((), ())),
                                       preferred_element_type=F32)


def _side_cast(stack, layer, n_steps, step):
    _, r, c = stack.shape
    nblk = 1
    while nblk * 2 <= n_steps and r % (nblk * 2) == 0 and (r // (nblk * 2)) % 16 == 0:
        nblk *= 2
    blk = lambda *ids: jnp.minimum(step(*ids), nblk - 1)
    in_spec = pl.BlockSpec((None, r // nblk, c), lambda *ids: (layer, blk(*ids), 0))
    out_spec = pl.BlockSpec((r // nblk, c), lambda *ids: (blk(*ids), 0))
    return in_spec, out_spec, jax.ShapeDtypeStruct((r, c), BF16)


def _inproj_kernel(x_ref, g_ref, sh_ref, sc_ref, w_ref, wg_ref, side_ref, p_ref, gate_ref, side_out_ref, h_scr):
    @pl.when(pl.program_id(1) == 0)
    def _():
        _inproj_fill(x_ref, g_ref, sh_ref, sc_ref, wg_ref, gate_ref, h_scr)

    p_ref[...] = jnp.dot(h_scr[...], w_ref[...], preferred_element_type=F32).astype(BF16)
    side_out_ref[...] = side_ref[...].astype(BF16)


def _inproj_split_kernel(n_lat_tiles, x_ref, c_ref, g_ref, sh_ref, sc_ref, w_ref, wg_ref, side_ref,
                         p_ref, gate_ref, side_out_ref, h_scr):
    i, j = pl.program_id(0), pl.program_id(1)

    @pl.when(jnp.logical_and(j == 0, i < n_lat_tiles))
    def _():
        _inproj_fill(x_ref, g_ref, sh_ref, sc_ref, wg_ref, gate_ref, h_scr)

    @pl.when(jnp.logical_and(j == 0, i >= n_lat_tiles))
    def _():
        _inproj_fill(c_ref, g_ref, sh_ref, sc_ref, wg_ref, gate_ref, h_scr)

    p_ref[...] = jnp.dot(h_scr[...], w_ref[...], preferred_element_type=F32).astype(BF16)
    side_out_ref[...] = side_ref[...].astype(BF16)


def _inproj_call(srcs, gain, mod3, w, wg, side_stack, layer, geom, layer_mod_base):
    d = srcs[0].shape[1]
    rows, tm, seg = geom["rows"], geom["tm"], geom["seg"]
    tn = P_WIDTH // 3
    side_in, side_out, side_shape = _side_cast(side_stack, layer, pl.cdiv(rows, tm), lambda i, j: i)
    modspec = lambda k: pl.BlockSpec((1, 1, d), lambda i, j: (layer_mod_base + seg(i) * 6 + k, 0, 0))
    if len(srcs) == 1:
        body = _inproj_kernel
        src_specs = [pl.BlockSpec((tm, d), lambda i, j: (i, 0))]
    else:
        n_lat_tiles = geom["r_lat"] // tm
        assert geom["r_ctx"] <= tm
        body = functools.partial(_inproj_split_kernel, n_lat_tiles)
        src_specs = [pl.BlockSpec((tm, d), lambda i, j: (jnp.minimum(i, n_lat_tiles - 1), 0)),
                     pl.BlockSpec((geom["r_ctx"], d), lambda i, j: (0, 0))]
    return pl.pallas_call(
        body,
        out_shape=(jax.ShapeDtypeStruct((rows, P_WIDTH), BF16),
                   jax.ShapeDtypeStruct((2 * GATE_LANES, rows), F32), side_shape),
        grid=(pl.cdiv(rows, tm), P_WIDTH // tn),
        in_specs=src_specs + [pl.BlockSpec((1, d), lambda i, j: (0, 0)), modspec(0), modspec(1),
                              pl.BlockSpec((d, tn), lambda i, j: (0, j)),
                              pl.BlockSpec((2 * GATE_LANES, d), lambda i, j: (0, 0)), side_in],
        out_specs=(pl.BlockSpec((tm, tn), lambda i, j: (i, j)),
                   pl.BlockSpec((2 * GATE_LANES, tm), lambda i, j: (0, i)), side_out),
        scratch_shapes=[pltpu.VMEM((tm, d), BF16)],
        compiler_params=_cparams(("arbitrary", "arbitrary")),
        name="in_proj",
    )(*srcs, gain, mod3, mod3, w, wg, side_stack)


def _gelu(x):
    return jax.nn.gelu(x, approximate=True)


def _gmlp_kernel(gu_ref, gv_ref, gain_ref, ws_ref, bs_ref, o_ref):
    units = [(c, g) for c in range(gu_ref.shape[0] // CHUNK) for g in range(GM_GROUPS)]
    rows = lambda c: slice(c * CHUNK, (c + 1) * CHUNK)
    cols = lambda g: slice(g * HEAD_DIM, (g + 1) * HEAD_DIM)
    vns = []
    for c, g in units:
        vg = _gelu(gv_ref[rows(c), cols(g)].astype(F32))
        vn = vg * lax.rsqrt(jnp.mean(vg * vg, axis=-1, keepdims=True) + EPS) * gain_ref[:, cols(g)]
        vns.append(vn.astype(BF16))
    ss = [jnp.dot(ws_ref[g], vn, preferred_element_type=F32) for (c, g), vn in zip(units, vns)]
    for (c, g), s in zip(units, ss):
        u = _gelu(gu_ref[rows(c), cols(g)].astype(F32))
        o_ref[rows(c), cols(g)] = (u * (s + bs_ref[:, cols(g)])).astype(BF16)


def _gmlp_call(p, v_gain, ws, bs_full, tr):
    rows = p.shape[0]
    gu_blk = _P_OFF["gu"] // GM_WIDTH
    gv_blk = _P_OFF["gv"] // GM_WIDTH
    return pl.pallas_call(
        _gmlp_kernel,
        out_shape=jax.ShapeDtypeStruct((rows, GM_WIDTH), BF16),
        grid=(rows // tr,),
        in_specs=[pl.BlockSpec((tr, GM_WIDTH), lambda i: (i, gu_blk)),
                  pl.BlockSpec((tr, GM_WIDTH), lambda i: (i, gv_blk)),
                  pl.BlockSpec((1, GM_WIDTH), lambda i: (0, 0)),
                  pl.BlockSpec((GM_GROUPS, CHUNK, CHUNK), lambda i: (0, 0, 0)),
                  pl.BlockSpec((CHUNK, GM_WIDTH), lambda i: (0, 0))],
        out_specs=pl.BlockSpec((tr, GM_WIDTH), lambda i: (i, 0)),
        compiler_params=_cparams(("arbitrary",)),
        name="gmlp",
    )(p, p, v_gain, ws, bs_full)


def _rope(x_bf, cos, sin, prot):
    rot = jnp.dot(x_bf, prot, preferred_element_type=F32)
    return (x_bf.astype(F32) * cos + rot * sin).astype(BF16)


def _attn_kernel(nt_seq, n_lat_tiles, sink_ref, q_ref, kp_ref, km_ref, kn_ref, vp_ref, vm_ref, vn_ref,
                 kx_ref, vx_ref, cp_ref, sp_ref, cm_ref, sm_ref, cn_ref, sn_ref, prot_ref, o_ref):
    t = pl.program_id(0)
    is_lat = t < n_lat_tiles
    pos = t % nt_seq
    has_cur = is_lat.astype(jnp.int32)
    has_prev = jnp.logical_and(is_lat, pos != 0).astype(jnp.int32)
    has_next = jnp.logical_and(is_lat, pos != nt_seq - 1).astype(jnp.int32)
    prot = prot_ref[...]
    nq = ATT_TILE // CHUNK
    q_scale = HEAD_DIM ** -0.5 * LOG2E

    qi = lax.broadcasted_iota(jnp.int32, (ATT_GROUP * CHUNK, 3 * CHUNK), 0) & (CHUNK - 1)
    kj = lax.broadcasted_iota(jnp.int32, (ATT_GROUP * CHUNK, 3 * CHUNK), 1)
    band = jnp.where(jnp.abs(kj - CHUNK - qi) <= CHUNK, 1, 0)
    hrow = lax.broadcasted_iota(jnp.int32, (ATT_GROUP * CHUNK, 1), 0)

    cos_k = jnp.concatenate([cp_ref[...], cm_ref[...], cn_ref[...]], axis=0)
    sin_k = jnp.concatenate([sp_ref[...], sm_ref[...], sn_ref[...]], axis=0)
    cos_q, sin_q = cm_ref[...] * q_scale, sm_ref[...] * q_scale
    head = lambda h: slice(h * HEAD_DIM, (h + 1) * HEAD_DIM)

    masks = []
    for a in range(nq):
        flags = [has_prev if a + j == 0 else (has_next if a + j == nq + 1 else has_cur) for j in range(3)]
        valid = band * jnp.where(kj < CHUNK, flags[0], jnp.where(kj < 2 * CHUNK, flags[1], flags[2])) > 0
        masks.append(jnp.where(valid, 0.0, NEG_BIG))

    k_rot = [_rope(jnp.concatenate([kp_ref[:, head(g)], km_ref[:, head(g)], kn_ref[:, head(g)]], axis=0),
                   cos_k, sin_k, prot) for g in range(ATT_KV_HEADS)]
    v_loc = [jnp.concatenate([vp_ref[:, head(g)], vm_ref[:, head(g)], vn_ref[:, head(g)]], axis=0)
             for g in range(ATT_KV_HEADS)]
    sks = [jnp.where(hrow < CHUNK, sink_ref[g * ATT_GROUP],
                     jnp.where(hrow < 2 * CHUNK, sink_ref[g * ATT_GROUP + 1], sink_ref[g * ATT_GROUP + 2])) * LOG2E
           for g in range(ATT_KV_HEADS)]
    units = [(g, a) for g in range(ATT_KV_HEADS) for a in range(nq)]
    rows = lambda a: slice(a * CHUNK, (a + 1) * CHUNK)
    q3s = [jnp.concatenate([_rope(q_ref[rows(a), head(g * ATT_GROUP + h)], cos_q[rows(a)], sin_q[rows(a)], prot)
                            for h in range(ATT_GROUP)], axis=0) for g, a in units]
    ss = [lax.dot_general(q3, jnp.concatenate([k_rot[g][a * CHUNK:(a + 3) * CHUNK], kx_ref[:, head(g)]], axis=0),
                          (((1,), (1,)), ((), ())), preferred_element_type=F32)
          for (g, a), q3 in zip(units, q3s)]
    ps, p_sinks = [], []
    for (g, a), s in zip(units, ss):
        s_loc = s[:, :3 * CHUNK] + masks[a]
        s_ctx = s[:, 3 * CHUNK:]
        m = jnp.maximum(jnp.maximum(jnp.max(s_loc, axis=1, keepdims=True),
                                    jnp.max(s_ctx, axis=1, keepdims=True)), sks[g])
        p_sinks.append(jnp.exp2(sks[g] - m))
        ps.append(jnp.concatenate([jnp.exp2(s_loc - m), jnp.exp2(s_ctx - m)], axis=1).astype(BF16))
    ones_bf = jnp.ones((3 * CHUNK + kx_ref.shape[0], HEAD_DIM), BF16)
    for (g, a), p_all, p_sink in zip(units, ps, p_sinks):
        v_all = jnp.concatenate([v_loc[g][a * CHUNK:(a + 3) * CHUNK], vx_ref[:, head(g)]], axis=0)
        r = jnp.dot(p_all, jnp.concatenate([v_all, ones_bf], axis=1), preferred_element_type=F32)
        o = r[:, :HEAD_DIM] / (r[:, HEAD_DIM:] + p_sink)
        for h in range(ATT_GROUP):
            o_ref[rows(a), head(g * ATT_GROUP + h)] = o[h * CHUNK:(h + 1) * CHUNK, :].astype(BF16)


def _attn_call(p, sink, cos_tab, sin_tab, prot, geom):
    rows = p.shape[0]
    seq, lc, r_lat = geom["seq"], geom["lc"], geom["r_lat"]
    assert seq % ATT_TILE == 0 and lc % ATT_TILE == 0
    nt_seq, n_lat_tiles, nq = seq // ATT_TILE, r_lat // ATT_TILE, ATT_TILE // CHUNK
    ak_blk = _P_OFF["ak"] // ATT_KV_WIDTH
    av_blk = _P_OFF["av"] // ATT_KV_WIDTH
    ctx_base = r_lat // lc

    def is_lat(t):
        return t < n_lat_tiles

    def prev_ok(t):
        return jnp.logical_and(is_lat(t), t % nt_seq != 0)

    def next_ok(t):
        return jnp.logical_and(is_lat(t), t % nt_seq != nt_seq - 1)

    def prev_i(t):
        return jnp.where(prev_ok(t), nq * t - 1, nq * t)

    def next_i(t):
        return jnp.where(next_ok(t), nq * t + nq, nq * t + nq - 1)

    def tab_main(t):
        return jnp.where(is_lat(t), t % nt_seq, nt_seq)

    def tab_prev(t):
        return jnp.where(prev_ok(t), nq * (t % nt_seq) - 1, nq * nt_seq)

    def tab_next(t):
        return jnp.where(next_ok(t), nq * (t % nt_seq) + nq, nq * nt_seq)

    def ctx_i(t):
        return ctx_base + jnp.where(is_lat(t), t // nt_seq, (t - n_lat_tiles) // (lc // ATT_TILE))

    halo = lambda f, blk: pl.BlockSpec((CHUNK, ATT_KV_WIDTH), lambda t: (f(t), blk))
    main = lambda blk: pl.BlockSpec((ATT_TILE, ATT_KV_WIDTH), lambda t: (t, blk))
    tb = lambda rws, f: pl.BlockSpec((rws, HEAD_DIM), lambda t: (f(t), 0))
    return pl.pallas_call(
        functools.partial(_attn_kernel, nt_seq, n_lat_tiles),
        out_shape=jax.ShapeDtypeStruct((rows, ATT_WIDTH), BF16),
        grid=(rows // ATT_TILE,),
        in_specs=[pl.BlockSpec(memory_space=pltpu.SMEM),
                  pl.BlockSpec((ATT_TILE, ATT_WIDTH), lambda t: (t, 0)),
                  halo(prev_i, ak_blk), main(ak_blk), halo(next_i, ak_blk),
                  halo(prev_i, av_blk), main(av_blk), halo(next_i, av_blk),
                  pl.BlockSpec((lc, ATT_KV_WIDTH), lambda t: (ctx_i(t), ak_blk)),
                  pl.BlockSpec((lc, ATT_KV_WIDTH), lambda t: (ctx_i(t), av_blk)),
                  tb(CHUNK, tab_prev), tb(CHUNK, tab_prev), tb(ATT_TILE, tab_main), tb(ATT_TILE, tab_main),
                  tb(CHUNK, tab_next), tb(CHUNK, tab_next),
                  pl.BlockSpec((HEAD_DIM, HEAD_DIM), lambda t: (0, 0))],
        out_specs=pl.BlockSpec((ATT_TILE, ATT_WIDTH), lambda t: (t, 0)),
        compiler_params=_cparams(("arbitrary",)),
        name="window_attn",
    )(sink, p, p, p, p, p, p, p, p, p, cos_tab, sin_tab, cos_tab, sin_tab, cos_tab, sin_tab, prot)


def _conv_kernel(start_chunks, end_chunks, q_ref, qp_ref, qn_ref, k_ref, kp_ref, kn_ref, w_ref, b_ref, sh_ref,
                 oq_ref, okt_ref):
    nc = q_ref.shape[0] // CHUNK
    i = pl.program_id(0)
    srcs = ((q_ref, qp_ref, qn_ref), (k_ref, kp_ref, kn_ref))
    units = [(part, c) for part in range(2) for c in range(nc)]
    shifted = []
    for part, c in units:
        x_ref, xp_ref, xn_ref = srcs[part]
        cg = i * nc + c
        at_start = _any([cg == s for s in start_chunks])
        at_end = _any([cg == e for e in end_chunks])
        cur = x_ref[c * CHUNK:(c + 1) * CHUNK, :]
        prv = xp_ref[...] if c == 0 else x_ref[(c - 1) * CHUNK:c * CHUNK, :]
        nxt = xn_ref[...] if c == nc - 1 else x_ref[(c + 1) * CHUNK:(c + 2) * CHUNK, :]
        s_prev = sh_ref[jnp.where(at_start, 1, 0)]
        s_next = sh_ref[jnp.where(at_end, 3, 2)]
        shifted.append((jnp.dot(s_prev, jnp.concatenate([prv, cur], axis=0), preferred_element_type=F32),
                        jnp.dot(s_next, jnp.concatenate([cur, nxt], axis=0), preferred_element_type=F32)))
    for (part, c), (x_prev, x_next) in zip(units, shifted):
        cs = slice(part * ML_WIDTH, (part + 1) * ML_WIDTH)
        cur = srcs[part][0][c * CHUNK:(c + 1) * CHUNK, :].astype(F32)
        y = b_ref[0:1, cs] + x_prev * w_ref[0:1, cs] + cur * w_ref[1:2, cs] + x_next * w_ref[2:3, cs]
        hy = 0.5 * y
        y = hy + hy * jnp.tanh(hy)
        if part == 0:
            oq_ref[c * CHUNK:(c + 1) * CHUNK, :] = y.astype(BF16)
        else:
            for h in range(ML_HEADS):
                r0 = (c * ML_HEADS + h) * HEAD_DIM
                okt_ref[r0:r0 + HEAD_DIM, :] = y[:, h * HEAD_DIM:(h + 1) * HEAD_DIM].T.astype(BF16)


def _shift_matrices():
    t = np.arange(CHUNK)
    prev = np.zeros((CHUNK, 2 * CHUNK), np.float32)
    prev[t, t + CHUNK - 1] = 1.0
    nxt = np.zeros((CHUNK, 2 * CHUNK), np.float32)
    nxt[t, t + 1] = 1.0
    prev_cut, nxt_cut = prev.copy(), nxt.copy()
    prev_cut[0] = 0.0
    nxt_cut[CHUNK - 1] = 0.0
    return jnp.asarray(np.stack([prev, prev_cut, nxt, nxt_cut]), BF16)


def _conv_call(p, w8, b8, geom):
    rows = p.shape[0]
    tr = geom["conv_tile"]
    nc = tr // CHUNK
    mq_blk = _P_OFF["mq"] // ML_WIDTH
    mk_blk = _P_OFF["mk"] // ML_WIDTH
    start_chunks = tuple(s // CHUNK for s in geom["starts"])
    end_chunks = tuple(e // CHUNK for e in geom["ends"])

    def trio(blk):
        return [pl.BlockSpec((tr, ML_WIDTH), lambda i: (i, blk)),
                pl.BlockSpec((CHUNK, ML_WIDTH), lambda i: (jnp.maximum(i * nc - 1, 0), blk)),
                pl.BlockSpec((CHUNK, ML_WIDTH), lambda i: (jnp.minimum((i + 1) * nc, rows // CHUNK - 1), blk))]

    return pl.pallas_call(
        functools.partial(_conv_kernel, start_chunks, end_chunks),
        out_shape=(jax.ShapeDtypeStruct((rows, ML_WIDTH), BF16),
                   jax.ShapeDtypeStruct((rows * ML_HEADS, CHUNK), BF16)),
        grid=(rows // tr,),
        in_specs=trio(mq_blk) + trio(mk_blk) + [
            pl.BlockSpec((8, 2 * ML_WIDTH), lambda i: (0, 0)),
            pl.BlockSpec((8, 2 * ML_WIDTH), lambda i: (0, 0)),
            pl.BlockSpec((4, CHUNK, 2 * CHUNK), lambda i: (0, 0, 0))],
        out_specs=(pl.BlockSpec((tr, ML_WIDTH), lambda i: (i, 0)),
                   pl.BlockSpec((tr * ML_HEADS, CHUNK), lambda i: (i, 0))),
        compiler_params=_cparams(("arbitrary",)),
        name="short_conv",
    )(p, p, p, p, p, p, w8, b8, _shift_matrices())


def _split3(x):
    hi = x.astype(BF16)
    r1 = x - hi.astype(F32)
    mid = r1.astype(BF16)
    lo = (r1 - mid.astype(F32)).astype(BF16)
    return hi, mid, lo


def _cummax_lanes(xs, reverses):
    n = xs[0].shape[1]
    lane = lax.broadcasted_iota(jnp.int32, xs[0].shape, 1)
    sh = 1
    while sh < n:
        xs = [jnp.maximum(x, jnp.where(lane < n - sh, pltpu.roll(x, n - sh, 1), NEG_BIG)) if rev else
              jnp.maximum(x, jnp.where(lane >= sh, pltpu.roll(x, sh, 1), NEG_BIG)) for x, rev in zip(xs, reverses)]
        sh *= 2
    return xs


def _gate_prep_kernel(g_ref, bias_ref, o_ref):
    gl = GATE_LANES
    row = lax.broadcasted_iota(jnp.int32, (CHUNK, CHUNK), 0)
    col = lax.broadcasted_iota(jnp.int32, (CHUNK, CHUNK), 1)
    chunks = range(g_ref.shape[1] // CHUNK)
    pres = [g_ref[:, c * CHUNK:(c + 1) * CHUNK] + bias_ref[...] for c in chunks]
    lis = [pre[0:gl] for pre in pres]
    lf3s = []
    for pre in pres:
        pf = pre[gl:2 * gl]
        lf3s.append(jnp.concatenate(_split3(jnp.minimum(pf, 0.0) - jnp.log1p(jnp.exp(-jnp.abs(pf)))), axis=0))
    units = [(c, d) for c in chunks for d in range(2)]
    csum = [((row <= col) if d == 0 else (row >= col)).astype(BF16) for d in range(2)]
    cs3s = [jnp.dot(lf3s[c], csum[d], preferred_element_type=F32) for c, d in units]
    bs = [cs3[0:gl] + cs3[gl:2 * gl] + cs3[2 * gl:3 * gl] for cs3 in cs3s]
    us = [lis[c] - b for (c, d), b in zip(units, bs)]
    umaxs = _cummax_lanes(us, [d == 1 for c, d in units])
    for (c, d), b, u, umax in zip(units, bs, us, umaxs):
        g_rep = jnp.broadcast_to(b[:, CHUNK - 1:CHUNK] if d == 0 else b[:, 0:1], b.shape)
        a = g_rep - b + lis[c]
        mloc = jnp.broadcast_to(jnp.max(a, axis=1, keepdims=True), b.shape)
        r0 = (c * 2 + d) * GP_ROWS
        o_ref[r0:r0 + GP_ROWS, :] = jnp.concatenate([b, u, umax, jnp.exp(a - mloc), g_rep, mloc], axis=0)


def _gate_prep_call(gates, bias, tr):
    rows = gates.shape[1]
    nc = tr // CHUNK
    return pl.pallas_call(
        _gate_prep_kernel,
        out_shape=jax.ShapeDtypeStruct((rows // CHUNK * 2 * GP_ROWS, CHUNK), F32),
        grid=(rows // tr,),
        in_specs=[pl.BlockSpec((2 * GATE_LANES, tr), lambda i: (0, i)),
                  pl.BlockSpec((2 * GATE_LANES, CHUNK), lambda i: (0, 0))],
        out_specs=pl.BlockSpec((nc * 2 * GP_ROWS, CHUNK), lambda i: (i, 0)),
        compiler_params=_cparams(("arbitrary",)),
        name="gate_prep",
    )(gates, bias)


def _mlstm_kernel(qf_ref, kf_ref, vf_ref, gf_ref, qb_ref, kb_ref, vb_ref, gb_ref, hf_ref, hb_ref, ct_scr, m_scr):
    @pl.when(pl.program_id(1) == 0)
    def _():
        ct_scr[...] = jnp.zeros_like(ct_scr)
        m_scr[...] = jnp.zeros_like(m_scr)

    row = lax.broadcasted_iota(jnp.int32, (CHUNK, CHUNK), 0)
    col = lax.broadcasted_iota(jnp.int32, (CHUNK, CHUNK), 1)
    ones_bf = jnp.ones((CHUNK, HEAD_DIM), BF16)
    gl = GATE_LANES
    dirs = ((qf_ref, kf_ref, vf_ref, gf_ref, hf_ref), (qb_ref, kb_ref, vb_ref, gb_ref, hb_ref))
    prep = []
    for d, (_, _, _, g_ref, _) in enumerate(dirs):
        b, u, umax, w, g_rep, mloc = (g_ref[k * gl:(k + 1) * gl, :] for k in range(6))
        m0 = m_scr[d]
        mx = jnp.maximum(m0, umax)
        em = jnp.exp(LN_INV_SCALE - b - mx)
        m_new = jnp.maximum(g_rep + m0, mloc)
        a_old = jnp.exp(g_rep + m0 - m_new)
        w_new = w * jnp.exp(mloc - m_new)
        m_scr[d] = m_new
        cols = jnp.concatenate([mx * -LOG2E, em, jnp.zeros((CHUNK - 2 * gl, CHUNK), F32)], axis=0).T
        prep.append((u * LOG2E, w_new, m0 * LOG2E, a_old, cols))
    units = [(d, h) for d in range(2) for h in range(ML_HEADS)]
    head = lambda h: slice(h * HEAD_DIM, (h + 1) * HEAD_DIM)
    qks = [jnp.dot(dirs[d][0][:, head(h)], dirs[d][1][head(h), :], preferred_element_type=F32) for d, h in units]
    s_exts, em_bs = [], []
    for (d, h), qk in zip(units, qks):
        order = (col <= row) if d == 0 else (col >= row)
        u2, _, m02, _, cols = prep[d]
        j = d * ML_HEADS + h
        c_b = jnp.broadcast_to(cols[:, j:j + 1], (CHUNK, CHUNK))
        em_bs.append(jnp.broadcast_to(cols[:, gl + j:gl + j + 1], (CHUNK, CHUNK)))
        d_in = jnp.where(order, c_b + u2[j:j + 1, :], NEG_BIG)
        d_st = c_b + m02[j:j + 1, :]
        qh = dirs[d][0][:, head(h)]
        s_exts.append(jnp.concatenate([qk * jnp.exp2(d_in), qh.astype(F32) * jnp.exp2(d_st)], axis=1).astype(BF16))
    for (d, h), s_ext, em_b in zip(units, s_exts, em_bs):
        j = d * ML_HEADS + h
        v_aug = jnp.concatenate([dirs[d][2][:, head(h)], ones_bf], axis=1)
        rhs = jnp.concatenate([v_aug, ct_scr[j].astype(BF16)], axis=0)
        r = jnp.dot(s_ext, rhs, preferred_element_type=F32)
        hh = r[:, :HEAD_DIM] / jnp.maximum(jnp.abs(r[:, HEAD_DIM:]), em_b)
        dirs[d][4][:, head(h)] = hh.astype(BF16)
    for d, h in units:
        _, w_new, _, a_old, _ = prep[d]
        j = d * ML_HEADS + h
        v_aug = jnp.concatenate([dirs[d][2][:, head(h)], ones_bf], axis=1)
        ktw = (dirs[d][1][head(h), :].astype(F32) * w_new[j:j + 1, :]).astype(BF16)
        dct = jnp.dot(ktw, v_aug, preferred_element_type=F32)
        ao = jnp.concatenate([a_old[j:j + 1, :], a_old[j:j + 1, :]], axis=1)
        ct_scr[j] = ao * ct_scr[j] + dct


def _mlstm_call(q, kt, p, gp, geom):
    rows = p.shape[0]
    nb, n_lat, lc, batch = geom["nb"], geom["n_lat"], geom["lc"], geom["batch"]
    ncs = lc // CHUNK
    mv_blk = _P_OFF["mv"] // ML_WIDTH

    def fwd(b, s):
        return jnp.where(s < ncs, n_lat + b * ncs + s, b * nb + (s - ncs))

    def bwd(b, s):
        return jnp.where(s < ncs, n_lat + b * ncs + (ncs - 1 - s), b * nb + (nb - 1 - (s - ncs)))

    def specs(f, d):
        return [pl.BlockSpec((CHUNK, ML_WIDTH), lambda b, s: (f(b, s), 0)),
                pl.BlockSpec((ML_WIDTH, CHUNK), lambda b, s: (f(b, s), 0)),
                pl.BlockSpec((CHUNK, ML_WIDTH), lambda b, s: (f(b, s), mv_blk)),
                pl.BlockSpec((GP_ROWS, CHUNK), lambda b, s: (2 * f(b, s) + d, 0))]

    out = jax.ShapeDtypeStruct((rows, ML_WIDTH), BF16)
    return pl.pallas_call(
        _mlstm_kernel,
        out_shape=(out, out),
        grid=(batch, ncs + nb),
        in_specs=specs(fwd, 0) + specs(bwd, 1),
        out_specs=(pl.BlockSpec((CHUNK, ML_WIDTH), lambda b, s: (fwd(b, s), 0)),
                   pl.BlockSpec((CHUNK, ML_WIDTH), lambda b, s: (bwd(b, s), 0))),
        scratch_shapes=[pltpu.VMEM((2 * ML_HEADS, HEAD_DIM, 2 * HEAD_DIM), F32),
                        pltpu.VMEM((2, GATE_LANES, CHUNK), F32)],
        compiler_params=_cparams(("arbitrary", "arbitrary")),
        name="mlstm",
    )(q, kt, p, gp, q, kt, p, gp)


def _mix_update(ya_ref, yb_ref, hf_ref, hb_ref, mo_ref, og_ref, gt_ref, w_ref):
    hs = hf_ref[...].astype(F32) + hb_ref[...].astype(F32)
    parts = [ya_ref[...], yb_ref[...]]
    for h in range(ML_HEADS):
        cs = slice(h * HEAD_DIM, (h + 1) * HEAD_DIM)
        xh = hs[:, cs]
        y = xh * lax.rsqrt(jnp.mean(xh * xh, axis=-1, keepdims=True) + EPS) * og_ref[:, cs]
        parts.append((y * jax.nn.sigmoid(mo_ref[:, cs].astype(F32))).astype(BF16))
    lhs = jnp.concatenate(parts, axis=1)
    return gt_ref[0] * jnp.dot(lhs, w_ref[...], preferred_element_type=F32)


def _mixout_kernel(x_ref, ya_ref, yb_ref, hf_ref, hb_ref, mo_ref, og_ref, gt_ref, w_ref, s1_ref, s2_ref,
                   o_ref, s1_out_ref, s2_out_ref):
    o_ref[...] = x_ref[...] + _mix_update(ya_ref, yb_ref, hf_ref, hb_ref, mo_ref, og_ref, gt_ref, w_ref)
    s1_out_ref[...] = s1_ref[...].astype(BF16)
    s2_out_ref[...] = s2_ref[...].astype(BF16)


def _mixout_split_kernel(n_lat_tiles, x_ref, c_ref, ya_ref, yb_ref, hf_ref, hb_ref, mo_ref, og_ref, gt_ref, w_ref,
                         s1_ref, s2_ref, o_ref, s1_out_ref, s2_out_ref):
    upd = _mix_update(ya_ref, yb_ref, hf_ref, hb_ref, mo_ref, og_ref, gt_ref, w_ref)
    is_ctx = jnp.full((x_ref.shape[0], 1), pl.program_id(0), jnp.int32) >= n_lat_tiles
    o_ref[...] = jnp.where(is_ctx, c_ref[...], x_ref[...]) + upd
    s1_out_ref[...] = s1_ref[...].astype(BF16)
    s2_out_ref[...] = s2_ref[...].astype(BF16)


def _mixout_call(srcs, ya, yb, hf, hb, p, out_gain, mod3, w_out, side_stacks, layer, geom, layer_mod_base, n_tiles):
    d = srcs[0].shape[1]
    rows, tm, seg = geom["rows"], geom["tm_mix"], geom["seg_mix"]
    mo_blk = _P_OFF["mo"] // ML_WIDTH
    row = lambda w, blk=0: pl.BlockSpec((tm, w), lambda i: (i, blk))
    side = [_side_cast(stack, layer, n_tiles, lambda i: i) for stack in side_stacks]
    if len(srcs) == 1:
        body, src_specs, alias = _mixout_kernel, [row(d)], {0: 0}
    else:
        n_lat_tiles = geom["r_lat"] // tm
        body = functools.partial(_mixout_split_kernel, n_lat_tiles)
        src_specs = [pl.BlockSpec((tm, d), lambda i: (jnp.minimum(i, n_lat_tiles - 1), 0)),
                     pl.BlockSpec((tm, d), lambda i: (jnp.maximum(i - n_lat_tiles, 0), 0))]
        alias = {}
    return pl.pallas_call(
        body,
        out_shape=(jax.ShapeDtypeStruct((rows, d), F32), side[0][2], side[1][2]),
        grid=(n_tiles,),
        in_specs=src_specs + [row(GM_WIDTH), row(ATT_WIDTH), row(ML_WIDTH), row(ML_WIDTH), row(ML_WIDTH, mo_blk),
                              pl.BlockSpec((1, ML_WIDTH), lambda i: (0, 0)),
                              pl.BlockSpec((1, 1, d), lambda i: (layer_mod_base + seg(i) * 6 + 2, 0, 0)),
                              pl.BlockSpec((MIX_WIDTH, d), lambda i: (0, 0)), side[0][0], side[1][0]],
        out_specs=(row(d), side[0][1], side[1][1]),
        input_output_aliases=alias,
        compiler_params=_cparams(("arbitrary",)),
        name="mix_out",
    )(*srcs, ya, yb, hf, hb, p, out_gain, mod3, w_out, *side_stacks)


def _ffn_kernel(final, x_ref, g_ref, sh_ref, sc_ref, gt_ref, w1_ref, w2_ref, gf_ref, o_ref, h_scr):
    j = pl.program_id(1)

    def hidden_block():
        a = jnp.maximum(jnp.dot(h_scr[...], w1_ref[...], preferred_element_type=F32), 0.0)
        return jnp.dot((a * a).astype(BF16), w2_ref[...], preferred_element_type=F32)

    @pl.when(j == 0)
    def _():
        _norm_mod_blocked(x_ref, h_scr, g_ref[...], sc_ref[0], sh_ref[0])
        o_ref[...] = hidden_block()

    @pl.when(j > 0)
    def _():
        o_ref[...] += hidden_block()

    @pl.when(j == pl.num_programs(1) - 1)
    def _():
        xo = x_ref[...] + gt_ref[0] * o_ref[...]
        if final:
            xo = xo * lax.rsqrt(jnp.mean(xo * xo, axis=-1, keepdims=True) + EPS) * gf_ref[...]
        o_ref[...] = xo


def _ffn_call(xs, gain, mod3, w1, w2, g_final, geom, layer_mod_base, n_tiles, final):
    rows, d = xs.shape
    hidden = w1.shape[1]
    tm, seg = geom["tm_mix"], geom["seg_mix"]
    th = min(1024, hidden)
    out_rows = n_tiles * tm if final else rows
    modspec = lambda k: pl.BlockSpec((1, 1, d), lambda i, j: (layer_mod_base + seg(i) * 6 + k, 0, 0))
    return pl.pallas_call(
        functools.partial(_ffn_kernel, final),
        out_shape=jax.ShapeDtypeStruct((out_rows, d), F32),
        grid=(n_tiles, hidden // th),
        in_specs=[pl.BlockSpec((tm, d), lambda i, j: (i, 0)),
                  pl.BlockSpec((1, d), lambda i, j: (0, 0)),
                  modspec(3), modspec(4), modspec(5),
                  pl.BlockSpec((d, th), lambda i, j: (0, j)),
                  pl.BlockSpec((th, d), lambda i, j: (j, 0)),
                  pl.BlockSpec((1, d), lambda i, j: (0, 0))],
        out_specs=pl.BlockSpec((tm, d), lambda i, j: (i, 0)),
        scratch_shapes=[pltpu.VMEM((tm, d), BF16)],
        compiler_params=_cparams(("arbitrary", "arbitrary")),
        name="ffn_final" if final else "ffn",
    )(xs, gain, mod3, mod3, mod3, w1, w2, g_final)


def _rope_tables(seq):
    n_rows = seq // GRID_W
    half = HEAD_DIM // 2
    freqs = (1.0 / (ROPE_BASE ** (np.arange(0, half, 2, dtype=np.float32) / np.float32(half)))).astype(np.float32)
    ang_r = (np.arange(n_rows, dtype=np.float32)[:, None] * freqs).astype(np.float32).astype(np.float64)
    ang_c = (np.arange(GRID_W, dtype=np.float32)[:, None] * freqs).astype(np.float32).astype(np.float64)

    def full(fn, ident):
        r = jnp.repeat(jnp.asarray(fn(ang_r), F32), GRID_W, axis=0)
        c = jnp.tile(jnp.asarray(fn(ang_c), F32), (n_rows, 1))
        tab = jnp.concatenate([r, r, c, c], axis=-1)
        return jnp.concatenate([tab, jnp.full((ATT_TILE, HEAD_DIM), ident, F32)], axis=0)

    i = np.arange(HEAD_DIM)
    quarter = HEAD_DIM // 4
    prot = np.zeros((HEAD_DIM, HEAD_DIM), np.float32)
    first = (i % (2 * quarter)) < quarter
    prot[(i + quarter)[first], i[first]] = -1.0
    prot[(i - quarter)[~first], i[~first]] = 1.0
    return full(np.cos, 1.0), full(np.sin, 0.0), jnp.asarray(prot, BF16)


def _geometry(batch, seq, lc, d):
    tm = 1024 if seq % 1024 == 0 else 512
    tm_mix = 512
    assert seq % tm == 0 and seq % CHUNK == 0 and lc % CHUNK == 0 and (batch * seq) % lc == 0
    r_lat, r_ctx = batch * seq, batch * lc
    rows = r_lat + r_ctx
    assert rows % tm_mix == 0 and r_ctx % tm_mix == 0
    starts = tuple(b * seq for b in range(batch)) + tuple(r_lat + b * lc for b in range(batch))
    ends = tuple((b + 1) * seq - 1 for b in range(batch)) + tuple(r_lat + (b + 1) * lc - 1 for b in range(batch))
    n_chunks = rows // CHUNK

    def chunk_tile(limit):
        return CHUNK * max(c for c in range(1, limit + 1) if n_chunks % c == 0)

    return dict(tm=tm, tm_mix=tm_mix, rows=rows, r_lat=r_lat, r_ctx=r_ctx, batch=batch, seq=seq, lc=lc,
                conv_tile=chunk_tile(10), gmlp_tile=chunk_tile(20), prep_tile=chunk_tile(16),
                nb=seq // CHUNK, n_lat=r_lat // CHUNK,
                seg=lambda i: jnp.minimum(i // (seq // tm), batch),
                seg_mix=lambda i: jnp.minimum(i // (seq // tm_mix), batch),
                starts=starts, ends=ends)


def _permute_w_in(w):
    cols = [w[:, _SRC[n]:_SRC[n] + wd] for n, wd in _P_ORDER]
    return jnp.concatenate(cols, axis=1).astype(BF16)


def _gate_w(w):
    d = w.shape[0]
    mg = w[:, _SRC["mg"]:_SRC["mg"] + 4 * ML_HEADS].reshape(d, 2, 2, ML_HEADS)
    groups = []
    for t in range(2):
        part = mg[:, :, t, :].reshape(d, 2 * ML_HEADS)
        groups.append(jnp.pad(part, ((0, 0), (0, GATE_LANES - 2 * ML_HEADS))))
    return jnp.concatenate(groups, axis=1).T.astype(BF16)


def _gate_bias(b_i, b_f):
    pad = jnp.zeros((GATE_LANES - 2 * ML_HEADS,), F32)
    col = jnp.concatenate([b_i.reshape(-1), pad, b_f.reshape(-1), pad])
    return jnp.broadcast_to(col[:, None], (2 * GATE_LANES, CHUNK))


def kernel(x, c, ctx, c_ctx, w_ada, b_ada, g_mix, g_ffn, w_in, gm_v_gain, gm_w_s, gm_b_s, attn_sink,
           ml_conv_w, ml_conv_b, ml_b_i, ml_b_f, ml_out_gain, w_out, w_ff1, w_ff2, g_final):
    batch, seq, d = x.shape
    lc = ctx.shape[1]
    depth = w_ada.shape[0]
    geom = _geometry(batch, seq, lc, d)
    rows = geom["rows"]

    cc = jnp.concatenate([c, c_ctx[None, :], jnp.zeros((8 - batch - 1, d), F32)], axis=0)
    mod = _ada_call(cc, w_ada, b_ada)
    mod3 = mod[:, :batch + 1, :].reshape(depth * (batch + 1) * 6, 1, d)

    cos_tab, sin_tab, prot = _rope_tables(seq)

    srcs = (x.reshape(-1, d), ctx.reshape(-1, d))
    for layer in range(depth):
        last = layer == depth - 1
        base = layer * (batch + 1) * 6
        w_p = _permute_w_in(w_in[layer])
        w_g = _gate_w(w_in[layer])
        p, gates, w_out_bf = _inproj_call(srcs, g_mix[layer][None, :], mod3, w_p, w_g, w_out, layer, geom, base)

        bs_full = jnp.repeat(gm_b_s[layer].T, HEAD_DIM, axis=1)
        ya = _gmlp_call(p, gm_v_gain[layer][None, :], gm_w_s[layer].astype(BF16), bs_full, geom["gmlp_tile"])

        yb = _attn_call(p, attn_sink[layer], cos_tab, sin_tab, prot, geom)

        w8 = jnp.concatenate([ml_conv_w[layer], jnp.zeros((5, 2 * ML_WIDTH), F32)], axis=0)
        b8 = jnp.concatenate([ml_conv_b[layer][None, :], jnp.zeros((7, 2 * ML_WIDTH), F32)], axis=0)
        q_conv, kt_conv = _conv_call(p, w8, b8, geom)
        gp = _gate_prep_call(gates, _gate_bias(ml_b_i[layer], ml_b_f[layer]), geom["prep_tile"])
        hf, hb = _mlstm_call(q_conv, kt_conv, p, gp, geom)

        n_mix = (geom["r_lat"] if last else rows) // geom["tm_mix"]
        xs, w_ff1_bf, w_ff2_bf = _mixout_call(srcs, ya, yb, hf, hb, p, ml_out_gain[layer][None, :], mod3, w_out_bf,
                                              (w_ff1, w_ff2), layer, geom, base, n_mix)
        xs = _ffn_call(xs, g_ffn[layer][None, :], mod3, w_ff1_bf, w_ff2_bf, g_final[None, :], geom, base, n_mix, last)
        srcs = (xs,)
    return xs.reshape(batch, seq, d)
```

```python
import functools
import math

import jax
import jax.numpy as jnp
import numpy as np
from jax import lax
from jax.experimental import pallas as pl
from jax.experimental.pallas import tpu as pltpu

F32 = jnp.float32
BF16 = jnp.bfloat16

HEAD_DIM = 128
CHUNK = 128
GM_GROUPS = 4
ATT_HEADS = 6
ATT_KV_HEADS = 2
ATT_GROUP = ATT_HEADS // ATT_KV_HEADS
ML_HEADS = 6
GM_WIDTH = GM_GROUPS * HEAD_DIM
ATT_WIDTH = ATT_HEADS * HEAD_DIM
ATT_KV_WIDTH = ATT_KV_HEADS * HEAD_DIM
ML_WIDTH = ML_HEADS * HEAD_DIM
MIX_WIDTH = GM_WIDTH + ATT_WIDTH + ML_WIDTH
GRID_W = 64
ROPE_BASE = 10000.0
EPS = 1e-6
NEG_BIG = -1e30
LOG2E = math.log2(math.e)
LN_INV_SCALE = 0.5 * math.log(HEAD_DIM)

ATT_TILE = 2 * CHUNK
GATE_LANES = 16
GP_ROWS = 6 * GATE_LANES

_SRC = dict(gu=0, gv=512, aq=1024, ak=1792, av=2048, mq=2304, mk=3072, mv=3840, mo=4608, mg=5376)
_P_ORDER = (("aq", 768), ("mq", 768), ("gu", 512), ("gv", 512), ("ak", 256), ("av", 256),
            ("mk", 768), ("mv", 768), ("mo", 768))
P_WIDTH = sum(w for _, w in _P_ORDER)
_P_OFF = {}
_o = 0
for _n, _w in _P_ORDER:
    _P_OFF[_n] = _o
    _o += _w

VMEM_LIMIT = 56 * 1024 * 1024


def _cparams(sem):
    return pltpu.CompilerParams(dimension_semantics=sem, vmem_limit_bytes=VMEM_LIMIT)


def _any(preds):
    return functools.reduce(jnp.logical_or, preds)


def _ada_kernel(c_ref, w_ref, b_ref, o_ref):
    c = c_ref[...]
    s = (c * jax.nn.sigmoid(c)).astype(BF16)
    o_ref[...] = jnp.dot(s, w_ref[...].astype(BF16), preferred_element_type=F32) + b_ref[...]


def _ada_call(cc, w_ada, b_ada):
    depth, d, n = w_ada.shape
    tn = math.gcd(n, 2048)
    return pl.pallas_call(
        _ada_kernel,
        out_shape=jax.ShapeDtypeStruct((depth, cc.shape[0], n), F32),
        grid=(depth, n // tn),
        in_specs=[pl.BlockSpec((cc.shape[0], d), lambda l, j: (0, 0)),
                  pl.BlockSpec((None, d, tn), lambda l, j: (l, 0, j)),
                  pl.BlockSpec((None, 1, tn), lambda l, j: (l, 0, j))],
        out_specs=pl.BlockSpec((None, cc.shape[0], tn), lambda l, j: (l, 0, j)),
        compiler_params=_cparams(("arbitrary", "arbitrary")),
        name="ada_mod",
    )(cc, w_ada, b_ada.reshape(depth, 1, n))


def _norm_mod_blocked(src_ref, dst_ref, gain, scale, shift, rows_per=256):
    n, d = src_ref.shape
    g2 = gain * (1.0 + scale)
    for r0 in range(0, n, rows_per):
        rs = slice(r0, r0 + rows_per)
        ss = jnp.zeros((rows_per, HEAD_DIM), F32)
        for c0 in range(0, d, HEAD_DIM):
            xb = src_ref[rs, c0:c0 + HEAD_DIM]
            ss = ss + xb * xb
        inv = lax.rsqrt(jnp.sum(ss, axis=-1, keepdims=True) * (1.0 / d) + EPS)
        for c0 in range(0, d, HEAD_DIM):
            cs = slice(c0, c0 + HEAD_DIM)
            dst_ref[rs, cs] = (src_ref[rs, cs] * inv * g2[:, cs] + shift[:, cs]).astype(BF16)


def _inproj_fill(src_ref, g_ref, sh_ref, sc_ref, wg_ref, gate_ref, h_scr):
    n = src_ref.shape[0]
    _norm_mod_blocked(src_ref, h_scr, g_ref[...], sc_ref[0], sh_ref[0])
    gate_ref[:, 0:n] = lax.dot_general(wg_ref[...], h_scr[0:n, :], (((1,), (1,)), ((), ())),
                                       preferred_element_type=F32)


def _side_cast(stack, layer, n_steps, step):
    _, r, c = stack.shape
    nblk = 1
    while nblk * 2 <= n_steps and r % (nblk * 2) == 0 and (r // (nblk * 2)) % 16 == 0:
        nblk *= 2
    blk = lambda *ids: jnp.minimum(step(*ids), nblk - 1)
    in_spec = pl.BlockSpec((None, r // nblk, c), lambda *ids: (layer, blk(*ids), 0))
    out_spec = pl.BlockSpec((r // nblk, c), lambda *ids: (blk(*ids), 0))
    return in_spec, out_spec, jax.ShapeDtypeStruct((r, c), BF16)


def _inproj_kernel(x_ref, g_ref, sh_ref, sc_ref, w_ref, wg_ref, side_ref, p_ref, gate_ref, side_out_ref, h_scr):
    @pl.when(pl.program_id(1) == 0)
    def _():
        _inproj_fill(x_ref, g_ref, sh_ref, sc_ref, wg_ref, gate_ref, h_scr)

    p_ref[...] = jnp.dot(h_scr[...], w_ref[...], preferred_element_type=F32).astype(BF16)
    side_out_ref[...] = side_ref[...].astype(BF16)


def _inproj_split_kernel(n_lat_tiles, x_ref, c_ref, g_ref, sh_ref, sc_ref, w_ref, wg_ref, side_ref,
                         p_ref, gate_ref, side_out_ref, h_scr):
    i, j = pl.program_id(0), pl.program_id(1)

    @pl.when(jnp.logical_and(j == 0, i < n_lat_tiles))
    def _():
        _inproj_fill(x_ref, g_ref, sh_ref, sc_ref, wg_ref, gate_ref, h_scr)

    @pl.when(jnp.logical_and(j == 0, i >= n_lat_tiles))
    def _():
        _inproj_fill(c_ref, g_ref, sh_ref, sc_ref, wg_ref, gate_ref, h_scr)

    p_ref[...] = jnp.dot(h_scr[...], w_ref[...], preferred_element_type=F32).astype(BF16)
    side_out_ref[...] = side_ref[...].astype(BF16)


def _inproj_call(srcs, gain, mod3, w, wg, side_stack, layer, geom, layer_mod_base):
    d = srcs[0].shape[1]
    rows, tm, seg = geom["rows"], geom["tm"], geom["seg"]
    tn = P_WIDTH // 3
    side_in, side_out, side_shape = _side_cast(side_stack, layer, pl.cdiv(rows, tm), lambda i, j: i)
    modspec = lambda k: pl.BlockSpec((1, 1, d), lambda i, j: (layer_mod_base + seg(i) * 6 + k, 0, 0))
    if len(srcs) == 1:
        body = _inproj_kernel
        src_specs = [pl.BlockSpec((tm, d), lambda i, j: (i, 0))]
    else:
        n_lat_tiles = geom["r_lat"] // tm
        assert geom["r_ctx"] <= tm
        body = functools.partial(_inproj_split_kernel, n_lat_tiles)
        src_specs = [pl.BlockSpec((tm, d), lambda i, j: (jnp.minimum(i, n_lat_tiles - 1), 0)),
                     pl.BlockSpec((geom["r_ctx"], d), lambda i, j: (0, 0))]
    return pl.pallas_call(
        body,
        out_shape=(jax.ShapeDtypeStruct((rows, P_WIDTH), BF16),
                   jax.ShapeDtypeStruct((2 * GATE_LANES, rows), F32), side_shape),
        grid=(pl.cdiv(rows, tm), P_WIDTH // tn),
        in_specs=src_specs + [pl.BlockSpec((1, d), lambda i, j: (0, 0)), modspec(0), modspec(1),
                              pl.BlockSpec((d, tn), lambda i, j: (0, j)),
                              pl.BlockSpec((2 * GATE_LANES, d), lambda i, j: (0, 0)), side_in],
        out_specs=(pl.BlockSpec((tm, tn), lambda i, j: (i, j)),
                   pl.BlockSpec((2 * GATE_LANES, tm), lambda i, j: (0, i)), side_out),
        scratch_shapes=[pltpu.VMEM((tm, d), BF16)],
        compiler_params=_cparams(("arbitrary", "arbitrary")),
        name="in_proj",
    )(*srcs, gain, mod3, mod3, w, wg, side_stack)


def _gelu(x):
    return jax.nn.gelu(x, approximate=True)


def _gmlp_kernel(gu_ref, gv_ref, gain_ref, ws_ref, bs_ref, o_ref):
    units = [(c, g) for c in range(gu_ref.shape[0] // CHUNK) for g in range(GM_GROUPS)]
    rows = lambda c: slice(c * CHUNK, (c + 1) * CHUNK)
    cols = lambda g: slice(g * HEAD_DIM, (g + 1) * HEAD_DIM)
    vns = []
    for c, g in units:
        vg = _gelu(gv_ref[rows(c), cols(g)].astype(F32))
        vn = vg * lax.rsqrt(jnp.mean(vg * vg, axis=-1, keepdims=True) + EPS) * gain_ref[:, cols(g)]
        vns.append(vn.astype(BF16))
    ss = [jnp.dot(ws_ref[g], vn, preferred_element_type=F32) for (c, g), vn in zip(units, vns)]
    for (c, g), s in zip(units, ss):
        u = _gelu(gu_ref[rows(c), cols(g)].astype(F32))
        o_ref[rows(c), cols(g)] = (u * (s + bs_ref[:, cols(g)])).astype(BF16)


def _gmlp_call(p, v_gain, ws, bs_full, tr):
    rows = p.shape[0]
    gu_blk = _P_OFF["gu"] // GM_WIDTH
    gv_blk = _P_OFF["gv"] // GM_WIDTH
    return pl.pallas_call(
        _gmlp_kernel,
        out_shape=jax.ShapeDtypeStruct((rows, GM_WIDTH), BF16),
        grid=(rows // tr,),
        in_specs=[pl.BlockSpec((tr, GM_WIDTH), lambda i: (i, gu_blk)),
                  pl.BlockSpec((tr, GM_WIDTH), lambda i: (i, gv_blk)),
                  pl.BlockSpec((1, GM_WIDTH), lambda i: (0, 0)),
                  pl.BlockSpec((GM_GROUPS, CHUNK, CHUNK), lambda i: (0, 0, 0)),
                  pl.BlockSpec((CHUNK, GM_WIDTH), lambda i: (0, 0))],
        out_specs=pl.BlockSpec((tr, GM_WIDTH), lambda i: (i, 0)),
        compiler_params=_cparams(("arbitrary",)),
        name="gmlp",
    )(p, p, v_gain, ws, bs_full)


def _rope(x_bf, cos, sin, prot):
    rot = jnp.dot(x_bf, prot, preferred_element_type=F32)
    return (x_bf.astype(F32) * cos + rot * sin).astype(BF16)


def _attn_kernel(nt_seq, n_lat_tiles, sink_ref, q_ref, kp_ref, km_ref, kn_ref, vp_ref, vm_ref, vn_ref,
                 kx_ref, vx_ref, cp_ref, sp_ref, cm_ref, sm_ref, cn_ref, sn_ref, prot_ref, o_ref):
    t = pl.program_id(0)
    is_lat = t < n_lat_tiles
    pos = t % nt_seq
    has_cur = is_lat.astype(jnp.int32)
    has_prev = jnp.logical_and(is_lat, pos != 0).astype(jnp.int32)
    has_next = jnp.logical_and(is_lat, pos != nt_seq - 1).astype(jnp.int32)
    prot = prot_ref[...]
    nq = ATT_TILE // CHUNK
    q_scale = HEAD_DIM ** -0.5 * LOG2E

    qi = lax.broadcasted_iota(jnp.int32, (ATT_GROUP * CHUNK, 3 * CHUNK), 0) & (CHUNK - 1)
    kj = lax.broadcasted_iota(jnp.int32, (ATT_GROUP * CHUNK, 3 * CHUNK), 1)
    band = jnp.where(jnp.abs(kj - CHUNK - qi) <= CHUNK, 1, 0)
    hrow = lax.broadcasted_iota(jnp.int32, (ATT_GROUP * CHUNK, 1), 0)

    cos_k = jnp.concatenate([cp_ref[...], cm_ref[...], cn_ref[...]], axis=0)
    sin_k = jnp.concatenate([sp_ref[...], sm_ref[...], sn_ref[...]], axis=0)
    cos_q, sin_q = cm_ref[...] * q_scale, sm_ref[...] * q_scale
    head = lambda h: slice(h * HEAD_DIM, (h + 1) * HEAD_DIM)

    masks = []
    for a in range(nq):
        flags = [has_prev if a + j == 0 else (has_next if a + j == nq + 1 else has_cur) for j in range(3)]
        valid = band * jnp.where(kj < CHUNK, flags[0], jnp.where(kj < 2 * CHUNK, flags[1], flags[2])) > 0
        masks.append(jnp.where(valid, 0.0, NEG_BIG))

    k_rot = [_rope(jnp.concatenate([kp_ref[:, head(g)], km_ref[:, head(g)], kn_ref[:, head(g)]], axis=0),
                   cos_k, sin_k, prot) for g in range(ATT_KV_HEADS)]
    v_loc = [jnp.concatenate([vp_ref[:, head(g)], vm_ref[:, head(g)], vn_ref[:, head(g)]], axis=0)
             for g in range(ATT_KV_HEADS)]
    sks = [jnp.where(hrow < CHUNK, sink_ref[g * ATT_GROUP],
                     jnp.where(hrow < 2 * CHUNK, sink_ref[g * ATT_GROUP + 1], sink_ref[g * ATT_GROUP + 2])) * LOG2E
           for g in range(ATT_KV_HEADS)]
    units = [(g, a) for g in range(ATT_KV_HEADS) for a in range(nq)]
    rows = lambda a: slice(a * CHUNK, (a + 1) * CHUNK)
    q3s = [jnp.concatenate([_rope(q_ref[rows(a), head(g * ATT_GROUP + h)], cos_q[rows(a)], sin_q[rows(a)], prot)
                            for h in range(ATT_GROUP)], axis=0) for g, a in units]
    ss = [lax.dot_general(q3, jnp.concatenate([k_rot[g][a * CHUNK:(a + 3) * CHUNK], kx_ref[:, head(g)]], axis=0),
                          (((1,), (1,)), ((), ())), preferred_element_type=F32)
          for (g, a), q3 in zip(units, q3s)]
    ps, p_sinks = [], []
    for (g, a), s in zip(units, ss):
        s_loc = s[:, :3 * CHUNK] + masks[a]
        s_ctx = s[:, 3 * CHUNK:]
        m = jnp.maximum(jnp.maximum(jnp.max(s_loc, axis=1, keepdims=True),
                                    jnp.max(s_ctx, axis=1, keepdims=True)), sks[g])
        p_sinks.append(jnp.exp2(sks[g] - m))
        ps.append(jnp.concatenate([jnp.exp2(s_loc - m), jnp.exp2(s_ctx - m)], axis=1).astype(BF16))
    ones_bf = jnp.ones((3 * CHUNK + kx_ref.shape[0], HEAD_DIM), BF16)
    for (g, a), p_all, p_sink in zip(units, ps, p_sinks):
        v_all = jnp.concatenate([v_loc[g][a * CHUNK:(a + 3) * CHUNK], vx_ref[:, head(g)]], axis=0)
        r = jnp.dot(p_all, jnp.concatenate([v_all, ones_bf], axis=1), preferred_element_type=F32)
        o = r[:, :HEAD_DIM] / (r[:, HEAD_DIM:] + p_sink)
        for h in range(ATT_GROUP):
            o_ref[rows(a), head(g * ATT_GROUP + h)] = o[h * CHUNK:(h + 1) * CHUNK, :].astype(BF16)


def _attn_call(p, sink, cos_tab, sin_tab, prot, geom):
    rows = p.shape[0]
    seq, lc, r_lat = geom["seq"], geom["lc"], geom["r_lat"]
    assert seq % ATT_TILE == 0 and lc % ATT_TILE == 0
    nt_seq, n_lat_tiles, nq = seq // ATT_TILE, r_lat // ATT_TILE, ATT_TILE // CHUNK
    ak_blk = _P_OFF["ak"] // ATT_KV_WIDTH
    av_blk = _P_OFF["av"] // ATT_KV_WIDTH
    ctx_base = r_lat // lc

    def is_lat(t):
        return t < n_lat_tiles

    def prev_ok(t):
        return jnp.logical_and(is_lat(t), t % nt_seq != 0)

    def next_ok(t):
        return jnp.logical_and(is_lat(t), t % nt_seq != nt_seq - 1)

    def prev_i(t):
        return jnp.where(prev_ok(t), nq * t - 1, nq * t)

    def next_i(t):
        return jnp.where(next_ok(t), nq * t + nq, nq * t + nq - 1)

    def tab_main(t):
        return jnp.where(is_lat(t), t % nt_seq, nt_seq)

    def tab_prev(t):
        return jnp.where(prev_ok(t), nq * (t % nt_seq) - 1, nq * nt_seq)

    def tab_next(t):
        return jnp.where(next_ok(t), nq * (t % nt_seq) + nq, nq * nt_seq)

    def ctx_i(t):
        return ctx_base + jnp.where(is_lat(t), t // nt_seq, (t - n_lat_tiles) // (lc // ATT_TILE))

    halo = lambda f, blk: pl.BlockSpec((CHUNK, ATT_KV_WIDTH), lambda t: (f(t), blk))
    main = lambda blk: pl.BlockSpec((ATT_TILE, ATT_KV_WIDTH), lambda t: (t, blk))
    tb = lambda rws, f: pl.BlockSpec((rws, HEAD_DIM), lambda t: (f(t), 0))
    return pl.pallas_call(
        functools.partial(_attn_kernel, nt_seq, n_lat_tiles),
        out_shape=jax.ShapeDtypeStruct((rows, ATT_WIDTH), BF16),
        grid=(rows // ATT_TILE,),
        in_specs=[pl.BlockSpec(memory_space=pltpu.SMEM),
                  pl.BlockSpec((ATT_TILE, ATT_WIDTH), lambda t: (t, 0)),
                  halo(prev_i, ak_blk), main(ak_blk), halo(next_i, ak_blk),
                  halo(prev_i, av_blk), main(av_blk), halo(next_i, av_blk),
                  pl.BlockSpec((lc, ATT_KV_WIDTH), lambda t: (ctx_i(t), ak_blk)),
                  pl.BlockSpec((lc, ATT_KV_WIDTH), lambda t: (ctx_i(t), av_blk)),
                  tb(CHUNK, tab_prev), tb(CHUNK, tab_prev), tb(ATT_TILE, tab_main), tb(ATT_TILE, tab_main),
                  tb(CHUNK, tab_next), tb(CHUNK, tab_next),
                  pl.BlockSpec((HEAD_DIM, HEAD_DIM), lambda t: (0, 0))],
        out_specs=pl.BlockSpec((ATT_TILE, ATT_WIDTH), lambda t: (t, 0)),
        compiler_params=_cparams(("arbitrary",)),
        name="window_attn",
    )(sink, p, p, p, p, p, p, p, p, p, cos_tab, sin_tab, cos_tab, sin_tab, cos_tab, sin_tab, prot)


def _conv_kernel(start_chunks, end_chunks, q_ref, qp_ref, qn_ref, k_ref, kp_ref, kn_ref, w_ref, b_ref, sh_ref,
                 oq_ref, okt_ref):
    nc = q_ref.shape[0] // CHUNK
    i = pl.program_id(0)
    srcs = ((q_ref, qp_ref, qn_ref), (k_ref, kp_ref, kn_ref))
    units = [(part, c) for part in range(2) for c in range(nc)]
    shifted = []
    for part, c in units:
        x_ref, xp_ref, xn_ref = srcs[part]
        cg = i * nc + c
        at_start = _any([cg == s for s in start_chunks])
        at_end = _any([cg == e for e in end_chunks])
        cur = x_ref[c * CHUNK:(c + 1) * CHUNK, :]
        prv = xp_ref[...] if c == 0 else x_ref[(c - 1) * CHUNK:c * CHUNK, :]
        nxt = xn_ref[...] if c == nc - 1 else x_ref[(c + 1) * CHUNK:(c + 2) * CHUNK, :]
        s_prev = sh_ref[jnp.where(at_start, 1, 0)]
        s_next = sh_ref[jnp.where(at_end, 3, 2)]
        shifted.append((jnp.dot(s_prev, jnp.concatenate([prv, cur], axis=0), preferred_element_type=F32),
                        jnp.dot(s_next, jnp.concatenate([cur, nxt], axis=0), preferred_element_type=F32)))
    for (part, c), (x_prev, x_next) in zip(units, shifted):
        cs = slice(part * ML_WIDTH, (part + 1) * ML_WIDTH)
        cur = srcs[part][0][c * CHUNK:(c + 1) * CHUNK, :].astype(F32)
        y = b_ref[0:1, cs] + x_prev * w_ref[0:1, cs] + cur * w_ref[1:2, cs] + x_next * w_ref[2:3, cs]
        hy = 0.5 * y
        y = hy + hy * jnp.tanh(hy)
        if part == 0:
            oq_ref[c * CHUNK:(c + 1) * CHUNK, :] = y.astype(BF16)
        else:
            for h in range(ML_HEADS):
                r0 = (c * ML_HEADS + h) * HEAD_DIM
                okt_ref[r0:r0 + HEAD_DIM, :] = y[:, h * HEAD_DIM:(h + 1) * HEAD_DIM].T.astype(BF16)


def _shift_matrices():
    t = np.arange(CHUNK)
    prev = np.zeros((CHUNK, 2 * CHUNK), np.float32)
    prev[t, t + CHUNK - 1] = 1.0
    nxt = np.zeros((CHUNK, 2 * CHUNK), np.float32)
    nxt[t, t + 1] = 1.0
    prev_cut, nxt_cut = prev.copy(), nxt.copy()
    prev_cut[0] = 0.0
    nxt_cut[CHUNK - 1] = 0.0
    return jnp.asarray(np.stack([prev, prev_cut, nxt, nxt_cut]), BF16)


def _conv_call(p, w8, b8, geom):
    rows = p.shape[0]
    tr = geom["conv_tile"]
    nc = tr // CHUNK
    mq_blk = _P_OFF["mq"] // ML_WIDTH
    mk_blk = _P_OFF["mk"] // ML_WIDTH
    start_chunks = tuple(s // CHUNK for s in geom["starts"])
    end_chunks = tuple(e // CHUNK for e in geom["ends"])

    def trio(blk):
        return [pl.BlockSpec((tr, ML_WIDTH), lambda i: (i, blk)),
                pl.BlockSpec((CHUNK, ML_WIDTH), lambda i: (jnp.maximum(i * nc - 1, 0), blk)),
                pl.BlockSpec((CHUNK, ML_WIDTH), lambda i: (jnp.minimum((i + 1) * nc, rows // CHUNK - 1), blk))]

    return pl.pallas_call(
        functools.partial(_conv_kernel, start_chunks, end_chunks),
        out_shape=(jax.ShapeDtypeStruct((rows, ML_WIDTH), BF16),
                   jax.ShapeDtypeStruct((rows * ML_HEADS, CHUNK), BF16)),
        grid=(rows // tr,),
        in_specs=trio(mq_blk) + trio(mk_blk) + [
            pl.BlockSpec((8, 2 * ML_WIDTH), lambda i: (0, 0)),
            pl.BlockSpec((8, 2 * ML_WIDTH), lambda i: (0, 0)),
            pl.BlockSpec((4, CHUNK, 2 * CHUNK), lambda i: (0, 0, 0))],
        out_specs=(pl.BlockSpec((tr, ML_WIDTH), lambda i: (i, 0)),
                   pl.BlockSpec((tr * ML_HEADS, CHUNK), lambda i: (i, 0))),
        compiler_params=_cparams(("arbitrary",)),
        name="short_conv",
    )(p, p, p, p, p, p, w8, b8, _shift_matrices())


def _split3(x):
    hi = x.astype(BF16)
    r1 = x - hi.astype(F32)
    mid = r1.astype(BF16)
    lo = (r1 - mid.astype(F32)).astype(BF16)
    return hi, mid, lo


def _cummax_lanes(xs, reverses):
    n = xs[0].shape[1]
    lane = lax.broadcasted_iota(jnp.int32, xs[0].shape, 1)
    sh = 1
    while sh < n:
        xs = [jnp.maximum(x, jnp.where(lane < n - sh, pltpu.roll(x, n - sh, 1), NEG_BIG)) if rev else
              jnp.maximum(x, jnp.where(lane >= sh, pltpu.roll(x, sh, 1), NEG_BIG)) for x, rev in zip(xs, reverses)]
        sh *= 2
    return xs


def _gate_prep_kernel(g_ref, bias_ref, o_ref):
    gl = GATE_LANES
    row = lax.broadcasted_iota(jnp.int32, (CHUNK, CHUNK), 0)
    col = lax.broadcasted_iota(jnp.int32, (CHUNK, CHUNK), 1)
    chunks = range(g_ref.shape[1] // CHUNK)
    pres = [g_ref[:, c * CHUNK:(c + 1) * CHUNK] + bias_ref[...] for c in chunks]
    lis = [pre[0:gl] for pre in pres]
    lf3s = []
    for pre in pres:
        pf = pre[gl:2 * gl]
        lf3s.append(jnp.concatenate(_split3(jnp.minimum(pf, 0.0) - jnp.log1p(jnp.exp(-jnp.abs(pf)))), axis=0))
    units = [(c, d) for c in chunks for d in range(2)]
    csum = [((row <= col) if d == 0 else (row >= col)).astype(BF16) for d in range(2)]
    cs3s = [jnp.dot(lf3s[c], csum[d], preferred_element_type=F32) for c, d in units]
    bs = [cs3[0:gl] + cs3[gl:2 * gl] + cs3[2 * gl:3 * gl] for cs3 in cs3s]
    us = [lis[c] - b for (c, d), b in zip(units, bs)]
    umaxs = _cummax_lanes(us, [d == 1 for c, d in units])
    for (c, d), b, u, umax in zip(units, bs, us, umaxs):
        g_rep = jnp.broadcast_to(b[:, CHUNK - 1:CHUNK] if d == 0 else b[:, 0:1], b.shape)
        a = g_rep - b + lis[c]
        mloc = jnp.broadcast_to(jnp.max(a, axis=1, keepdims=True), b.shape)
        r0 = (c * 2 + d) * GP_ROWS
        o_ref[r0:r0 + GP_ROWS, :] = jnp.concatenate([b, u, umax, jnp.exp(a - mloc), g_rep, mloc], axis=0)


def _gate_prep_call(gates, bias, tr):
    rows = gates.shape[1]
    nc = tr // CHUNK
    return pl.pallas_call(
        _gate_prep_kernel,
        out_shape=jax.ShapeDtypeStruct((rows // CHUNK * 2 * GP_ROWS, CHUNK), F32),
        grid=(rows // tr,),
        in_specs=[pl.BlockSpec((2 * GATE_LANES, tr), lambda i: (0, i)),
                  pl.BlockSpec((2 * GATE_LANES, CHUNK), lambda i: (0, 0))],
        out_specs=pl.BlockSpec((nc * 2 * GP_ROWS, CHUNK), lambda i: (i, 0)),
        compiler_params=_cparams(("arbitrary",)),
        name="gate_prep",
    )(gates, bias)


def _mlstm_kernel(qf_ref, kf_ref, vf_ref, gf_ref, qb_ref, kb_ref, vb_ref, gb_ref, hf_ref, hb_ref, ct_scr, m_scr):
    @pl.when(pl.program_id(1) == 0)
    def _():
        ct_scr[...] = jnp.zeros_like(ct_scr)
        m_scr[...] = jnp.zeros_like(m_scr)

    row = lax.broadcasted_iota(jnp.int32, (CHUNK, CHUNK), 0)
    col = lax.broadcasted_iota(jnp.int32, (CHUNK, CHUNK), 1)
    ones_bf = jnp.ones((CHUNK, HEAD_DIM), BF16)
    gl = GATE_LANES
    dirs = ((qf_ref, kf_ref, vf_ref, gf_ref, hf_ref), (qb_ref, kb_ref, vb_ref, gb_ref, hb_ref))
    prep = []
    for d, (_, _, _, g_ref, _) in enumerate(dirs):
        b, u, umax, w, g_rep, mloc = (g_ref[k * gl:(k + 1) * gl, :] for k in range(6))
        m0 = m_scr[d]
        mx = jnp.maximum(m0, umax)
        em = jnp.exp(LN_INV_SCALE - b - mx)
        m_new = jnp.maximum(g_rep + m0, mloc)
        a_old = jnp.exp(g_rep + m0 - m_new)
        w_new = w * jnp.exp(mloc - m_new)
        m_scr[d] = m_new
        cols = jnp.concatenate([mx * -LOG2E, em, jnp.zeros((CHUNK - 2 * gl, CHUNK), F32)], axis=0).T
        prep.append((u * LOG2E, w_new, m0 * LOG2E, a_old, cols))
    units = [(d, h) for d in range(2) for h in range(ML_HEADS)]
    head = lambda h: slice(h * HEAD_DIM, (h + 1) * HEAD_DIM)
    qks = [jnp.dot(dirs[d][0][:, head(h)], dirs[d][1][head(h), :], preferred_element_type=F32) for d, h in units]
    s_exts, em_bs = [], []
    for (d, h), qk in zip(units, qks):
        order = (col <= row) if d == 0 else (col >= row)
        u2, _, m02, _, cols = prep[d]
        j = d * ML_HEADS + h
        c_b = jnp.broadcast_to(cols[:, j:j + 1], (CHUNK, CHUNK))
        em_bs.append(jnp.broadcast_to(cols[:, gl + j:gl + j + 1], (CHUNK, CHUNK)))
        d_in = jnp.where(order, c_b + u2[j:j + 1, :], NEG_BIG)
        d_st = c_b + m02[j:j + 1, :]
        qh = dirs[d][0][:, head(h)]
        s_exts.append(jnp.concatenate([qk * jnp.exp2(d_in), qh.astype(F32) * jnp.exp2(d_st)], axis=1).astype(BF16))
    for (d, h), s_ext, em_b in zip(units, s_exts, em_bs):
        j = d * ML_HEADS + h
        v_aug = jnp.concatenate([dirs[d][2][:, head(h)], ones_bf], axis=1)
        rhs = jnp.concatenate([v_aug, ct_scr[j].astype(BF16)], axis=0)
        r = jnp.dot(s_ext, rhs, preferred_element_type=F32)
        hh = r[:, :HEAD_DIM] / jnp.maximum(jnp.abs(r[:, HEAD_DIM:]), em_b)
        dirs[d][4][:, head(h)] = hh.astype(BF16)
    for d, h in units:
        _, w_new, _, a_old, _ = prep[d]
        j = d * ML_HEADS + h
        v_aug = jnp.concatenate([dirs[d][2][:, head(h)], ones_bf], axis=1)
        ktw = (dirs[d][1][head(h), :].astype(F32) * w_new[j:j + 1, :]).astype(BF16)
        dct = jnp.dot(ktw, v_aug, preferred_element_type=F32)
        ao = jnp.concatenate([a_old[j:j + 1, :], a_old[j:j + 1, :]], axis=1)
        ct_scr[j] = ao * ct_scr[j] + dct


def _mlstm_call(q, kt, p, gp, geom):
    rows = p.shape[0]
    nb, n_lat, lc, batch = geom["nb"], geom["n_lat"], geom["lc"], geom["batch"]
    ncs = lc // CHUNK
    mv_blk = _P_OFF["mv"] // ML_WIDTH

    def fwd(b, s):
        return jnp.where(s < ncs, n_lat + b * ncs + s, b * nb + (s - ncs))

    def bwd(b, s):
        return jnp.where(s < ncs, n_lat + b * ncs + (ncs - 1 - s), b * nb + (nb - 1 - (s - ncs)))

    def specs(f, d):
        return [pl.BlockSpec((CHUNK, ML_WIDTH), lambda b, s: (f(b, s), 0)),
                pl.BlockSpec((ML_WIDTH, CHUNK), lambda b, s: (f(b, s), 0)),
                pl.BlockSpec((CHUNK, ML_WIDTH), lambda b, s: (f(b, s), mv_blk)),
                pl.BlockSpec((GP_ROWS, CHUNK), lambda b, s: (2 * f(b, s) + d, 0))]

    out = jax.ShapeDtypeStruct((rows, ML_WIDTH), BF16)
    return pl.pallas_call(
        _mlstm_kernel,
        out_shape=(out, out),
        grid=(batch, ncs + nb),
        in_specs=specs(fwd, 0) + specs(bwd, 1),
        out_specs=(pl.BlockSpec((CHUNK, ML_WIDTH), lambda b, s: (fwd(b, s), 0)),
                   pl.BlockSpec((CHUNK, ML_WIDTH), lambda b, s: (bwd(b, s), 0))),
        scratch_shapes=[pltpu.VMEM((2 * ML_HEADS, HEAD_DIM, 2 * HEAD_DIM), F32),
                        pltpu.VMEM((2, GATE_LANES, CHUNK), F32)],
        compiler_params=_cparams(("arbitrary", "arbitrary")),
        name="mlstm",
    )(q, kt, p, gp, q, kt, p, gp)


def _mix_update(ya_ref, yb_ref, hf_ref, hb_ref, mo_ref, og_ref, gt_ref, w_ref):
    hs = hf_ref[...].astype(F32) + hb_ref[...].astype(F32)
    parts = [ya_ref[...], yb_ref[...]]
    for h in range(ML_HEADS):
        cs = slice(h * HEAD_DIM, (h + 1) * HEAD_DIM)
        xh = hs[:, cs]
        y = xh * lax.rsqrt(jnp.mean(xh * xh, axis=-1, keepdims=True) + EPS) * og_ref[:, cs]
        parts.append((y * jax.nn.sigmoid(mo_ref[:, cs].astype(F32))).astype(BF16))
    lhs = jnp.concatenate(parts, axis=1)
    return gt_ref[0] * jnp.dot(lhs, w_ref[...], preferred_element_type=F32)


def _mixout_kernel(x_ref, ya_ref, yb_ref, hf_ref, hb_ref, mo_ref, og_ref, gt_ref, w_ref, s1_ref, s2_ref,
                   o_ref, s1_out_ref, s2_out_ref):
    o_ref[...] = x_ref[...] + _mix_update(ya_ref, yb_ref, hf_ref, hb_ref, mo_ref, og_ref, gt_ref, w_ref)
    s1_out_ref[...] = s1_ref[...].astype(BF16)
    s2_out_ref[...] = s2_ref[...].astype(BF16)


def _mixout_split_kernel(n_lat_tiles, x_ref, c_ref, ya_ref, yb_ref, hf_ref, hb_ref, mo_ref, og_ref, gt_ref, w_ref,
                         s1_ref, s2_ref, o_ref, s1_out_ref, s2_out_ref):
    upd = _mix_update(ya_ref, yb_ref, hf_ref, hb_ref, mo_ref, og_ref, gt_ref, w_ref)
    is_ctx = jnp.full((x_ref.shape[0], 1), pl.program_id(0), jnp.int32) >= n_lat_tiles
    o_ref[...] = jnp.where(is_ctx, c_ref[...], x_ref[...]) + upd
    s1_out_ref[...] = s1_ref[...].astype(BF16)
    s2_out_ref[...] = s2_ref[...].astype(BF16)


def _mixout_call(srcs, ya, yb, hf, hb, p, out_gain, mod3, w_out, side_stacks, layer, geom, layer_mod_base, n_tiles):
    d = srcs[0].shape[1]
    rows, tm, seg = geom["rows"], geom["tm_mix"], geom["seg_mix"]
    mo_blk = _P_OFF["mo"] // ML_WIDTH
    row = lambda w, blk=0: pl.BlockSpec((tm, w), lambda i: (i, blk))
    side = [_side_cast(stack, layer, n_tiles, lambda i: i) for stack in side_stacks]
    if len(srcs) == 1:
        body, src_specs, alias = _mixout_kernel, [row(d)], {0: 0}
    else:
        n_lat_tiles = geom["r_lat"] // tm
        body = functools.partial(_mixout_split_kernel, n_lat_tiles)
        src_specs = [pl.BlockSpec((tm, d), lambda i: (jnp.minimum(i, n_lat_tiles - 1), 0)),
                     pl.BlockSpec((tm, d), lambda i: (jnp.maximum(i - n_lat_tiles, 0), 0))]
        alias = {}
    return pl.pallas_call(
        body,
        out_shape=(jax.ShapeDtypeStruct((rows, d), F32), side[0][2], side[1][2]),
        grid=(n_tiles,),
        in_specs=src_specs + [row(GM_WIDTH), row(ATT_WIDTH), row(ML_WIDTH), row(ML_WIDTH), row(ML_WIDTH, mo_blk),
                              pl.BlockSpec((1, ML_WIDTH), lambda i: (0, 0)),
                              pl.BlockSpec((1, 1, d), lambda i: (layer_mod_base + seg(i) * 6 + 2, 0, 0)),
                              pl.BlockSpec((MIX_WIDTH, d), lambda i: (0, 0)), side[0][0], side[1][0]],
        out_specs=(row(d), side[0][1], side[1][1]),
        input_output_aliases=alias,
        compiler_params=_cparams(("arbitrary",)),
        name="mix_out",
    )(*srcs, ya, yb, hf, hb, p, out_gain, mod3, w_out, *side_stacks)


def _ffn_kernel(final, x_ref, g_ref, sh_ref, sc_ref, gt_ref, w1_ref, w2_ref, gf_ref, o_ref, h_scr):
    j = pl.program_id(1)

    def hidden_block():
        a = jnp.maximum(jnp.dot(h_scr[...], w1_ref[...], preferred_element_type=F32), 0.0)
        return jnp.dot((a * a).astype(BF16), w2_ref[...], preferred_element_type=F32)

    @pl.when(j == 0)
    def _():
        _norm_mod_blocked(x_ref, h_scr, g_ref[...], sc_ref[0], sh_ref[0])
        o_ref[...] = hidden_block()

    last = pl.num_programs(1) - 1

    @pl.when(jnp.logical_and(j > 0, j < last))
    def _():
        o_ref[...] += hidden_block()

    @pl.when(j == last)
    def _():
        xo = x_ref[...] + gt_ref[0] * (o_ref[...] + hidden_block())
        if final:
            xo = xo * lax.rsqrt(jnp.mean(xo * xo, axis=-1, keepdims=True) + EPS) * gf_ref[...]
        o_ref[...] = xo


def _ffn_call(xs, gain, mod3, w1, w2, g_final, geom, layer_mod_base, n_tiles, final):
    rows, d = xs.shape
    hidden = w1.shape[1]
    tm, seg = geom["tm_mix"], geom["seg_mix"]
    th = min(1024, hidden // 2)
    assert hidden % th == 0 and hidden // th >= 2
    out_rows = n_tiles * tm if final else rows
    modspec = lambda k: pl.BlockSpec((1, 1, d), lambda i, j: (layer_mod_base + seg(i) * 6 + k, 0, 0))
    return pl.pallas_call(
        functools.partial(_ffn_kernel, final),
        out_shape=jax.ShapeDtypeStruct((out_rows, d), F32),
        grid=(n_tiles, hidden // th),
        in_specs=[pl.BlockSpec((tm, d), lambda i, j: (i, 0)),
                  pl.BlockSpec((1, d), lambda i, j: (0, 0)),
                  modspec(3), modspec(4), modspec(5),
                  pl.BlockSpec((d, th), lambda i, j: (0, j)),
                  pl.BlockSpec((th, d), lambda i, j: (j, 0)),
                  pl.BlockSpec((1, d), lambda i, j: (0, 0))],
        out_specs=pl.BlockSpec((tm, d), lambda i, j: (i, 0)),
        scratch_shapes=[pltpu.VMEM((tm, d), BF16)],
        compiler_params=_cparams(("arbitrary", "arbitrary")),
        name="ffn_final" if final else "ffn",
    )(xs, gain, mod3, mod3, mod3, w1, w2, g_final)


def _rope_tables(seq):
    n_rows = seq // GRID_W
    half = HEAD_DIM // 2
    freqs = (1.0 / (ROPE_BASE ** (np.arange(0, half, 2, dtype=np.float32) / np.float32(half)))).astype(np.float32)
    ang_r = (np.arange(n_rows, dtype=np.float32)[:, None] * freqs).astype(np.float32).astype(np.float64)
    ang_c = (np.arange(GRID_W, dtype=np.float32)[:, None] * freqs).astype(np.float32).astype(np.float64)

    def full(fn, ident):
        r = jnp.repeat(jnp.asarray(fn(ang_r), F32), GRID_W, axis=0)
        c = jnp.tile(jnp.asarray(fn(ang_c), F32), (n_rows, 1))
        tab = jnp.concatenate([r, r, c, c], axis=-1)
        return jnp.concatenate([tab, jnp.full((ATT_TILE, HEAD_DIM), ident, F32)], axis=0)

    i = np.arange(HEAD_DIM)
    quarter = HEAD_DIM // 4
    prot = np.zeros((HEAD_DIM, HEAD_DIM), np.float32)
    first = (i % (2 * quarter)) < quarter
    prot[(i + quarter)[first], i[first]] = -1.0
    prot[(i - quarter)[~first], i[~first]] = 1.0
    return full(np.cos, 1.0), full(np.sin, 0.0), jnp.asarray(prot, BF16)


def _geometry(batch, seq, lc, d):
    tm = 1024 if seq % 1024 == 0 else 512
    tm_mix = 512
    assert seq % tm == 0 and seq % CHUNK == 0 and lc % CHUNK == 0 and (batch * seq) % lc == 0
    r_lat, r_ctx = batch * seq, batch * lc
    rows = r_lat + r_ctx
    assert rows % tm_mix == 0 and r_ctx % tm_mix == 0
    starts = tuple(b * seq for b in range(batch)) + tuple(r_lat + b * lc for b in range(batch))
    ends = tuple((b + 1) * seq - 1 for b in range(batch)) + tuple(r_lat + (b + 1) * lc - 1 for b in range(batch))
    n_chunks = rows // CHUNK

    def chunk_tile(limit):
        return CHUNK * max(c for c in range(1, limit + 1) if n_chunks % c == 0)

    return dict(tm=tm, tm_mix=tm_mix, rows=rows, r_lat=r_lat, r_ctx=r_ctx, batch=batch, seq=seq, lc=lc,
                conv_tile=chunk_tile(10), gmlp_tile=chunk_tile(20), prep_tile=chunk_tile(16),
                nb=seq // CHUNK, n_lat=r_lat // CHUNK,
                seg=lambda i: jnp.minimum(i // (seq // tm), batch),
                seg_mix=lambda i: jnp.minimum(i // (seq // tm_mix), batch),
                starts=starts, ends=ends)


def _permute_w_in(w):
    cols = [w[:, _SRC[n]:_SRC[n] + wd] for n, wd in _P_ORDER]
    return jnp.concatenate(cols, axis=1).astype(BF16)


def _gate_w(w):
    d = w.shape[0]
    mg = w[:, _SRC["mg"]:_SRC["mg"] + 4 * ML_HEADS].reshape(d, 2, 2, ML_HEADS)
    groups = []
    for t in range(2):
        part = mg[:, :, t, :].reshape(d, 2 * ML_HEADS)
        groups.append(jnp.pad(part, ((0, 0), (0, GATE_LANES - 2 * ML_HEADS))))
    return jnp.concatenate(groups, axis=1).T.astype(BF16)


def _gate_bias(b_i, b_f):
    pad = jnp.zeros((GATE_LANES - 2 * ML_HEADS,), F32)
    col = jnp.concatenate([b_i.reshape(-1), pad, b_f.reshape(-1), pad])
    return jnp.broadcast_to(col[:, None], (2 * GATE_LANES, CHUNK))


def kernel(x, c, ctx, c_ctx, w_ada, b_ada, g_mix, g_ffn, w_in, gm_v_gain, gm_w_s, gm_b_s, attn_sink,
           ml_conv_w, ml_conv_b, ml_b_i, ml_b_f, ml_out_gain, w_out, w_ff1, w_ff2, g_final):
    batch, seq, d = x.shape
    lc = ctx.shape[1]
    depth = w_ada.shape[0]
    geom = _geometry(batch, seq, lc, d)
    rows = geom["rows"]

    cc = jnp.concatenate([c, c_ctx[None, :], jnp.zeros((8 - batch - 1, d), F32)], axis=0)
    mod = _ada_call(cc, w_ada, b_ada)
    mod3 = mod[:, :batch + 1, :].reshape(depth * (batch + 1) * 6, 1, d)

    cos_tab, sin_tab, prot = _rope_tables(seq)

    srcs = (x.reshape(-1, d), ctx.reshape(-1, d))
    for layer in range(depth):
        last = layer == depth - 1
        base = layer * (batch + 1) * 6
        w_p = _permute_w_in(w_in[layer])
        w_g = _gate_w(w_in[layer])
        p, gates, w_out_bf = _inproj_call(srcs, g_mix[layer][None, :], mod3, w_p, w_g, w_out, layer, geom, base)

        bs_full = jnp.repeat(gm_b_s[layer].T, HEAD_DIM, axis=1)
        ya = _gmlp_call(p, gm_v_gain[layer][None, :], gm_w_s[layer].astype(BF16), bs_full, geom["gmlp_tile"])

        yb = _attn_call(p, attn_sink[layer], cos_tab, sin_tab, prot, geom)

        w8 = jnp.concatenate([ml_conv_w[layer], jnp.zeros((5, 2 * ML_WIDTH), F32)], axis=0)
        b8 = jnp.concatenate([ml_conv_b[layer][None, :], jnp.zeros((7, 2 * ML_WIDTH), F32)], axis=0)
        q_conv, kt_conv = _conv_call(p, w8, b8, geom)
        gp = _gate_prep_call(gates, _gate_bias(ml_b_i[layer], ml_b_f[layer]), geom["prep_tile"])
        hf, hb = _mlstm_call(q_conv, kt_conv, p, gp, geom)

        n_mix = (geom["r_lat"] if last else rows) // geom["tm_mix"]
        xs, w_ff1_bf, w_ff2_bf = _mixout_call(srcs, ya, yb, hf, hb, p, ml_out_gain[layer][None, :], mod3, w_out_bf,
                                              (w_ff1, w_ff2), layer, geom, base, n_mix)
        xs = _ffn_call(xs, g_ffn[layer][None, :], mod3, w_ff1_bf, w_ff2_bf, g_final[None, :], geom, base, n_mix, last)
        srcs = (xs,)
    return xs.reshape(batch, seq, d)
```
